```python
import math
import jax, jax.numpy as jnp
from jax import lax
import numpy as np

D_MODEL = 2048
BATCH = 4
SEQ = 8192
DEPTH = 1

MIX_WIDTH = D_MODEL
RWKV_HEAD_DIM = 64
RWKV_WIDTH = MIX_WIDTH // 2
RWKV_HEADS = RWKV_WIDTH // RWKV_HEAD_DIM
RWKV_DECAY_LORA = 64
RWKV_ICLR_LORA = 64
RWKV_GATE_LORA = 160
RWKV_GN_EPS = 64e-5
SSM_HEAD_DIM = 64
SSM_WIDTH = MIX_WIDTH - RWKV_WIDTH
SSM_HEADS = SSM_WIDTH // SSM_HEAD_DIM
SSM_GROUPS = 2
SSM_HEADS_PER_GROUP = SSM_HEADS // SSM_GROUPS
SSM_STATE = 128
SSM_CONV = 4
SSM_CHUNK = 128
SSM_CONV_CH = SSM_WIDTH + 2 * SSM_GROUPS * SSM_STATE
SSM_NORM_EPS = 1e-5
RWKV_PROJ = 3 * RWKV_WIDTH + RWKV_DECAY_LORA + RWKV_ICLR_LORA + RWKV_GATE_LORA
SSM_PROJ = SSM_WIDTH + SSM_CONV_CH + SSM_HEADS
IN_PROJ = RWKV_PROJ + SSM_PROJ
PEER_HEADS = 8
PEER_N_KEYS = 128
PEER_EXPERTS = PEER_N_KEYS * PEER_N_KEYS
PEER_KEY_DIM = 256
PEER_HALF = PEER_KEY_DIM // 2
PEER_TOPK = 16
PEER_BLOCK = 128
RMS_EPS = 1e-6

kernel_name = "hymba_rwkv7_mamba2_peer_adaln"


def rms_norm(x, w, eps=RMS_EPS):
    xf = x.astype(jnp.float32)
    return xf * lax.rsqrt(jnp.mean(xf * xf, axis=-1, keepdims=True) + eps) * w.astype(jnp.float32)


def modulate(x, shift, scale):
    return x * (1.0 + scale[:, None, :]) + shift[:, None, :]


def token_shift(p):
    return jnp.pad(p, ((0, 0), (1, 0), (0, 0)))[:, :-1]


def rwkv7_recurrence(r, w, k, v, kk, a):
    bsz, _, h, n = r.shape

    def step(S, inp):
        r_t, w_t, k_t, v_t, kk_t, a_t = inp
        sa = jnp.einsum('bhvk,bhk->bhv', S, kk_t)
        S = (S * w_t[:, :, None, :]
             - sa[..., None] * (kk_t * a_t)[:, :, None, :]
             + v_t[..., None] * k_t[:, :, None, :])
        return S, jnp.einsum('bhvk,bhk->bhv', S, r_t)

    S0 = jnp.zeros((bsz, h, n, n), jnp.float32)
    xs = (jnp.moveaxis(r, 1, 0), jnp.moveaxis(w, 1, 0), jnp.moveaxis(k, 1, 0),
          jnp.moveaxis(v, 1, 0), jnp.moveaxis(kk, 1, 0), jnp.moveaxis(a, 1, 0))
    _, o = lax.scan(step, S0, xs)
    return jnp.moveaxis(o, 0, 1)


def rwkv7_mix(p, mu, w0, w_up, a0, a_up, g_up, k_k, k_a, r_k, ln_w, ln_b):
    bsz, t, _ = p.shape
    p = p + (token_shift(p) - p) * mu
    s0 = RWKV_WIDTH
    r, k, v, wd, ad, gd = jnp.split(
        p, [s0, 2 * s0, 3 * s0, 3 * s0 + RWKV_DECAY_LORA,
            3 * s0 + RWKV_DECAY_LORA + RWKV_ICLR_LORA], axis=-1)
    w_log = -jax.nn.softplus(-(w0 + jnp.tanh(wd) @ w_up)) - 0.5
    decay = jnp.exp(-jnp.exp(w_log))
    a = jax.nn.sigmoid(a0 + ad @ a_up)
    g = jax.nn.sigmoid(gd) @ g_up
    heads = lambda z: z.reshape(bsz, t, RWKV_HEADS, RWKV_HEAD_DIM)
    kk = heads(k * k_k)
    kk = kk / jnp.maximum(jnp.sqrt(jnp.sum(kk * kk, axis=-1, keepdims=True)), 1e-12)
    k = k * (1.0 + (a - 1.0) * k_a)
    r, k, v, decay, a = heads(r), heads(k), heads(v), heads(decay), heads(a)
    o = rwkv7_recurrence(r, decay, k, v, kk, a)
    mean = jnp.mean(o, axis=-1, keepdims=True)
    var = jnp.mean(jnp.square(o - mean), axis=-1, keepdims=True)
    o = ((o - mean) * lax.rsqrt(var + RWKV_GN_EPS)).reshape(bsz, t, RWKV_WIDTH) * ln_w + ln_b
    bonus = jnp.sum(r * k * r_k, axis=-1, keepdims=True) * v
    return (o + bonus.reshape(bsz, t, RWKV_WIDTH)) * g


def segsum_exp(a):
    n = a.shape[-1]
    cs = jnp.cumsum(a, axis=-1)
    diff = cs[..., :, None] - cs[..., None, :]
    mask = jnp.tril(jnp.ones((n, n), dtype=bool))
    return jnp.exp(jnp.where(mask, diff, -jnp.inf))


def ssd_chunked(x, a, b, c):
    bsz, t, g, r, p = x.shape
    n = b.shape[-1]
    nc, L = t // SSM_CHUNK, SSM_CHUNK
    x = x.reshape(bsz, nc, L, g, r, p)
    b = b.reshape(bsz, nc, L, g, n)
    c = c.reshape(bsz, nc, L, g, n)
    a = a.reshape(bsz, nc, L, g, r).transpose(0, 3, 4, 1, 2)
    a_cs = jnp.cumsum(a, axis=-1)
    cb = jnp.einsum('bclgn,bcsgn->bgcls', c, b)
    y_diag = jnp.einsum('bgrcls,bcsgrp->bclgrp', cb[:, :, None] * segsum_exp(a), x)
    decay_to_end = jnp.exp(a_cs[..., -1:] - a_cs)
    states = jnp.einsum('bclgn,bgrcl,bclgrp->bcgrpn', b, decay_to_end, x)
    states = jnp.pad(states, ((0, 0), (1, 0), (0, 0), (0, 0), (0, 0), (0, 0)))
    decay_chunk = segsum_exp(jnp.pad(a_cs[..., -1], ((0, 0), (0, 0), (0, 0), (1, 0))))
    states = jnp.einsum('bgrzc,bcgrpn->bzgrpn', decay_chunk, states)[:, :-1]
    y_off = jnp.einsum('bclgn,bcgrpn,bgrcl->bclgrp', c, states, jnp.exp(a_cs))
    return (y_diag + y_off).reshape(bsz, t, g, r, p)


def mamba2_mix(p, conv_w, conv_b, dt_bias, a_log, d_skip, norm_w):
    bsz, t, _ = p.shape
    z, xbc, dt = jnp.split(p, [SSM_WIDTH, SSM_WIDTH + SSM_CONV_CH], axis=-1)
    xbc = lax.conv_general_dilated(
        xbc, conv_w[:, None, :].astype(xbc.dtype), window_strides=(1,),
        padding=[(SSM_CONV - 1, 0)], dimension_numbers=('NWC', 'WIO', 'NWC'),
        feature_group_count=SSM_CONV_CH) + conv_b
    xbc = jax.nn.silu(xbc)
    xs, bm, cm = jnp.split(xbc, [SSM_WIDTH, SSM_WIDTH + SSM_GROUPS * SSM_STATE], axis=-1)
    dt = jax.nn.softplus(dt + dt_bias)
    A = -jnp.exp(a_log.astype(jnp.float32))
    G, R = SSM_GROUPS, SSM_HEADS_PER_GROUP
    xs = xs.reshape(bsz, t, G, R, SSM_HEAD_DIM)
    dt_g = dt.reshape(bsz, t, G, R)
    y = ssd_chunked(xs * dt_g[..., None], dt_g * A.reshape(G, R),
                    bm.reshape(bsz, t, G, SSM_STATE), cm.reshape(bsz, t, G, SSM_STATE))
    y = y + xs * d_skip.reshape(G, R)[..., None]
    y = y.reshape(bsz, t, SSM_WIDTH) * jax.nn.silu(z)
    y = y.reshape(bsz, t, G, SSM_WIDTH // G)
    y = y * lax.rsqrt(jnp.mean(y * y, axis=-1, keepdims=True) + SSM_NORM_EPS)
    return y.reshape(bsz, t, SSM_WIDTH) * norm_w


def peer_ffn(n, w_query, sub_keys, u_emb, v_emb):
    bsz, t, d = n.shape
    blocks = n.reshape(bsz * t // PEER_BLOCK, PEER_BLOCK, d)

    def one_block(xb):
        q = (xb @ w_query).reshape(PEER_BLOCK, PEER_HEADS, 2, PEER_HALF)
        s = jnp.einsum('lhpd,hpnd->lhpn', q, sub_keys).astype(jnp.float32)
        s_half, i_half = lax.top_k(s, PEER_TOPK)
        cand_s = (s_half[:, :, 0, :, None] + s_half[:, :, 1, None, :]).reshape(
            PEER_BLOCK, PEER_HEADS, PEER_TOPK * PEER_TOPK)
        cand_i = (i_half[:, :, 0, :, None] * PEER_N_KEYS + i_half[:, :, 1, None, :]).reshape(
            PEER_BLOCK, PEER_HEADS, PEER_TOPK * PEER_TOPK)
        top_s, pos = lax.top_k(cand_s, PEER_TOPK)
        eidx = jnp.take_along_axis(cand_i, pos, axis=-1)
        gates = jax.nn.softmax(top_s, axis=-1)
        u = u_emb[eidx]
        act = jax.nn.gelu(jnp.einsum('ld,lhkd->lhk', xb, u), approximate=False)
        return jnp.einsum('lhk,lhkd->ld', gates * act, v_emb[eidx])

    return lax.map(one_block, blocks).reshape(bsz, t, d)


def setup_inputs(seed: int = 0) -> dict:
    key = jax.random.key(seed)
    ks = jax.random.split(key, 32)
    nrm = lambda i, shape, s: jax.random.normal(ks[i], shape, jnp.float32) * s
    L = DEPTH
    dt0 = jnp.exp(jax.random.uniform(ks[17], (L, SSM_HEADS), jnp.float32,
                                     minval=math.log(1e-3), maxval=math.log(1e-1)))
    return {
        "x": nrm(0, (BATCH, SEQ, D_MODEL), 1.0),
        "c": nrm(1, (BATCH, D_MODEL), 1.0),
        "ada_w": nrm(2, (L, D_MODEL, 6 * D_MODEL), 0.5 * D_MODEL ** -0.5),
        "ada_b": nrm(3, (L, 6 * D_MODEL), 0.02),
        "norm1_w": 1.0 + nrm(4, (L, D_MODEL), 0.02),
        "w_in": nrm(5, (L, D_MODEL, IN_PROJ), D_MODEL ** -0.5),
        "rwkv_mu": jax.random.uniform(ks[6], (L, RWKV_PROJ), jnp.float32),
        "rwkv_w0": jax.random.uniform(ks[7], (L, RWKV_WIDTH), jnp.float32, minval=-6.5, maxval=-1.5),
        "rwkv_w_up": nrm(8, (L, RWKV_DECAY_LORA, RWKV_WIDTH), 0.5 * RWKV_DECAY_LORA ** -0.5),
        "rwkv_a0": nrm(9, (L, RWKV_WIDTH), 0.1),
        "rwkv_a_up": nrm(10, (L, RWKV_ICLR_LORA, RWKV_WIDTH), 0.5 * RWKV_ICLR_LORA ** -0.5),
        "rwkv_g_up": nrm(11, (L, RWKV_GATE_LORA, RWKV_WIDTH), RWKV_GATE_LORA ** -0.5),
        "rwkv_k_k": 0.85 + nrm(12, (L, RWKV_WIDTH), 0.02),
        "rwkv_k_a": 1.0 + nrm(13, (L, RWKV_WIDTH), 0.02),
        "rwkv_r_k": nrm(14, (L, RWKV_HEADS, RWKV_HEAD_DIM), 0.1),
        "rwkv_ln_w": 1.0 + nrm(15, (L, RWKV_WIDTH), 0.02),
        "rwkv_ln_b": nrm(16, (L, RWKV_WIDTH), 0.02),
        "mamba_conv_w": nrm(18, (L, SSM_CONV, SSM_CONV_CH), SSM_CONV ** -0.5),
        "mamba_conv_b": nrm(19, (L, SSM_CONV_CH), 0.02),
        "mamba_dt_bias": dt0 + jnp.log(-jnp.expm1(-dt0)),
        "mamba_a_log": jnp.log(jax.random.uniform(ks[20], (L, SSM_HEADS), jnp.float32, minval=1.0, maxval=16.0)),
        "mamba_d": 1.0 + nrm(21, (L, SSM_HEADS), 0.02),
        "mamba_norm_w": 1.0 + nrm(22, (L, SSM_WIDTH), 0.02),
        "w_out": nrm(23, (L, MIX_WIDTH, D_MODEL), MIX_WIDTH ** -0.5),
        "norm2_w": 1.0 + nrm(24, (L, D_MODEL), 0.02),
        "peer_w_query": nrm(25, (L, D_MODEL, PEER_HEADS * PEER_KEY_DIM), D_MODEL ** -0.5),
        "peer_sub_keys": nrm(26, (L, PEER_HEADS, 2, PEER_N_KEYS, PEER_HALF), PEER_HALF ** -0.5),
        "peer_u": nrm(27, (L, PEER_EXPERTS, D_MODEL), D_MODEL ** -0.5),
        "peer_v": nrm(28, (L, PEER_EXPERTS, D_MODEL), 1.0),
        "final_norm_w": 1.0 + nrm(29, (D_MODEL,), 0.02),
    }


def reference(x, c, ada_w, ada_b, norm1_w, w_in, rwkv_mu, rwkv_w0, rwkv_w_up, rwkv_a0,
              rwkv_a_up, rwkv_g_up, rwkv_k_k, rwkv_k_a, rwkv_r_k, rwkv_ln_w, rwkv_ln_b,
              mamba_conv_w, mamba_conv_b, mamba_dt_bias, mamba_a_log, mamba_d, mamba_norm_w,
              w_out, norm2_w, peer_w_query, peer_sub_keys, peer_u, peer_v, final_norm_w):
    out_dtype = x.dtype
    h = x.astype(jnp.float32)
    cond = jax.nn.silu(c.astype(jnp.float32))
    for i in range(DEPTH):
        mod = cond @ ada_w[i] + ada_b[i]
        shift1, scale1, gate1, shift2, scale2, gate2 = jnp.split(mod, 6, axis=-1)
        n = modulate(rms_norm(h, norm1_w[i]), shift1, scale1)
        proj = n @ w_in[i]
        o_rwkv = rwkv7_mix(proj[..., :RWKV_PROJ], rwkv_mu[i], rwkv_w0[i], rwkv_w_up[i],
                           rwkv_a0[i], rwkv_a_up[i], rwkv_g_up[i], rwkv_k_k[i], rwkv_k_a[i],
                           rwkv_r_k[i], rwkv_ln_w[i], rwkv_ln_b[i])
        o_ssm = mamba2_mix(proj[..., RWKV_PROJ:], mamba_conv_w[i], mamba_conv_b[i],
                           mamba_dt_bias[i], mamba_a_log[i], mamba_d[i], mamba_norm_w[i])
        mix = jnp.concatenate([o_rwkv, o_ssm], axis=-1) @ w_out[i]
        h = h + gate1[:, None, :] * mix
        n = modulate(rms_norm(h, norm2_w[i]), shift2, scale2)
        h = h + gate2[:, None, :] * peer_ffn(n, peer_w_query[i], peer_sub_keys[i], peer_u[i], peer_v[i])
    return rms_norm(h, final_norm_w).astype(out_dtype)
```

```python
import functools

import jax
import jax.numpy as jnp
from jax import lax
from jax.experimental import pallas as pl
from jax.experimental.pallas import tpu as pltpu

F32 = jnp.float32
BF16 = jnp.bfloat16

D_MODEL = 2048
RW = 1024
RW_HEAD = 64
SSM_W = 1024
SSM_HEAD = 64
SSM_HEADS = 16
SSM_STATE = 128
SSM_CONV = 4
PEER_HEADS = 8
PEER_KEYS = 128
PEER_HALF = 128
PEER_TOPK = 16
RMS_EPS = 1e-6
GN_EPS = 64e-5
SSM_EPS = 1e-5

P_R, P_K, P_V, P_Z, P_XS, P_BC, P_LORA = 0, 1024, 2048, 3072, 4096, 5120, 5632
P_TOT = 6144
LORA_W = 512
LORA_DT = 384

RWKV_CHUNK = 64
SSD_CHUNK = 128
LANES = 128

VMEM_LIMIT = 56 * 1024 * 1024


def _cparams(n_axes):
    return pltpu.CompilerParams(
        dimension_semantics=("arbitrary",) * n_axes, vmem_limit_bytes=VMEM_LIMIT)


def _split3(x):
    h1 = x.astype(BF16)
    r1 = x - h1.astype(F32)
    h2 = r1.astype(BF16)
    r2 = r1 - h2.astype(F32)
    return h1, h2, r2.astype(BF16)


def _mm(a, b, dims):
    return lax.dot_general(a.astype(BF16), b.astype(BF16), (dims, ((), ())),
                           preferred_element_type=F32)


def _dot(a, b):
    return _mm(a, b, ((1,), (0,)))


def _dot_nt(a, b):
    return _mm(a, b, ((1,), (1,)))


def _dot_tn(a, b):
    return _mm(a, b, ((0,), (0,)))


def _dot_xe(x, e_bf16):
    h1, h2, h3 = _split3(x)
    return _dot(h1, e_bf16) + _dot(h2, e_bf16) + _dot(h3, e_bf16)


def _dot_ex(e_bf16, x):
    h1, h2, h3 = _split3(x)
    return _dot(e_bf16, h1) + _dot(e_bf16, h2) + _dot(e_bf16, h3)


def _dot_hi(a, b):
    ah = a.astype(BF16)
    al = (a - ah.astype(F32)).astype(BF16)
    bh = b.astype(BF16)
    bl = (b - bh.astype(F32)).astype(BF16)
    return _dot(ah, bh) + _dot(al, bh) + _dot(ah, bl)


def _silu(x):
    return x * jax.nn.sigmoid(x)


def _ada_kernel(cb_ref, w_ref, b_ref, o_ref):
    w = w_ref[...]
    reps = w.shape[1] // LANES
    rows = []
    for b in range(cb_ref.shape[0]):
        cb = _silu(cb_ref[b])
        rows.append(jnp.sum(w * pltpu.repeat(cb, reps, axis=1), axis=0, keepdims=True))
    o_ref[...] = jnp.concatenate(rows, axis=0) + b_ref[...]


def _ada_mod(c, ada_w, ada_b):
    bsz, d = c.shape
    n = ada_w.shape[1]
    tn = 1024
    cb = jnp.broadcast_to(c[:, :, None], (bsz, d, LANES))
    return pl.pallas_call(
        _ada_kernel,
        grid=(n // tn,),
        in_specs=[pl.BlockSpec((bsz, d, LANES), lambda j: (0, 0, 0)),
                  pl.BlockSpec((d, tn), lambda j: (0, j)),
                  pl.BlockSpec((1, tn), lambda j: (0, j))],
        out_specs=pl.BlockSpec((bsz, tn), lambda j: (0, j)),
        out_shape=jax.ShapeDtypeStruct((bsz, n), F32),
        compiler_params=_cparams(1),
        name="ada_mod",
    )(cb, ada_w, ada_b.reshape(1, n))


def _rms_mod(x, w, shift, scale):
    ms = jnp.mean(x * x, axis=-1, keepdims=True)
    return (x * lax.rsqrt(ms + RMS_EPS) * w) * (1.0 + scale) + shift


def _inproj_kernel(x_ref, nw_ref, sh_ref, sc_ref, w_ref, mu_ref, o_ref, n_scr, carry_scr):
    i = pl.program_id(1)
    j = pl.program_id(2)

    @pl.when(j == 0)
    def _():
        n_scr[...] = _rms_mod(x_ref[0], nw_ref[...], sh_ref[0], sc_ref[0]).astype(BF16)

    p = _dot(n_scr[...], w_ref[...])
    tm = p.shape[0]
    prev_last = jnp.where(i == 0, 0.0, carry_scr[j][7:8, :])
    row = lax.broadcasted_iota(jnp.int32, p.shape, 0)
    shifted = jnp.where(row == 0, prev_last, pltpu.roll(p, 1, 0))
    carry_scr[j] = p[tm - 8:tm, :]
    o_ref[0] = p + (shifted - p) * mu_ref[...]


def _in_proj(x, norm_w, shift, scale, w_packed, mu_packed, tm):
    bsz, t, d = x.shape
    tn = 1536
    nb = P_TOT // tn
    return pl.pallas_call(
        _inproj_kernel,
        grid=(bsz, t // tm, nb),
        in_specs=[pl.BlockSpec((1, tm, d), lambda b, i, j: (b, i, 0)),
                  pl.BlockSpec((1, d), lambda b, i, j: (0, 0)),
                  pl.BlockSpec((1, 1, d), lambda b, i, j: (b, 0, 0)),
                  pl.BlockSpec((1, 1, d), lambda b, i, j: (b, 0, 0)),
                  pl.BlockSpec((d, tn), lambda b, i, j: (0, j)),
                  pl.BlockSpec((1, tn), lambda b, i, j: (0, j))],
        out_specs=pl.BlockSpec((1, tm, tn), lambda b, i, j: (b, i, j)),
        out_shape=jax.ShapeDtypeStruct((bsz, t, P_TOT), F32),
        scratch_shapes=[pltpu.VMEM((tm, d), BF16), pltpu.VMEM((nb, 8, tn), F32)],
        compiler_params=_cparams(3),
        name="in_proj",
    )(x, norm_w.reshape(1, d), shift.reshape(bsz, 1, d), scale.reshape(bsz, 1, d),
      w_packed, mu_packed)


def _rwkv_kernel(r_ref, k_ref, v_ref, lora_ref, w0_ref, a0_ref, kk_ref, ka_ref, rk_ref,
                 lnw_ref, lnb_ref, wup_ref, aup_ref, gup_ref, o_ref, s_scr):
    C = RWKV_CHUNK
    n_chunks = r_ref.shape[1] // C

    @pl.when(pl.program_id(2) == 0)
    def _():
        s_scr[...] = jnp.zeros_like(s_scr)

    lane = lax.broadcasted_iota(jnp.int32, (1, LANES), 1)
    m0 = (lane < RW_HEAD).astype(F32)
    m1 = 1.0 - m0
    row = lax.broadcasted_iota(jnp.int32, (LANES, LANES), 0)
    col = lax.broadcasted_iota(jnp.int32, (LANES, LANES), 1)
    head_ones = jnp.where((row // RW_HEAD) == (col // RW_HEAD), 1.0, 0.0).astype(BF16)
    r64 = lax.broadcasted_iota(jnp.int32, (C, C), 0)
    c64 = lax.broadcasted_iota(jnp.int32, (C, C), 1)
    tri64 = jnp.where(r64 >= c64, 1.0, 0.0).astype(BF16)
    strict = row > col
    incl = row >= col
    blk16 = (row // 16) == (col // 16)
    eye = jnp.where(row == col, 1.0, 0.0)

    w0, a0, kkp, kap, rkp = w0_ref[...], a0_ref[...], kk_ref[...], ka_ref[...], rk_ref[...]
    lnw, lnb = lnw_ref[...], lnb_ref[...]
    wup, aup, gup = wup_ref[...], aup_ref[...], gup_ref[...]

    def stack(x):
        return jnp.concatenate([x * m0, x * m1], axis=0)

    def chunk(ci, carry):
        sl = pl.ds(pl.multiple_of(ci * C, C), C)
        r = r_ref[0, sl, :]
        k = k_ref[0, sl, :]
        v = v_ref[0, sl, :]
        lora = lora_ref[0, sl, :]
        wa = lora[:, 0:128]
        w_lin = w0 + _dot_hi(jnp.tanh(wa), wup)
        logw = -jnp.exp(-jax.nn.softplus(-w_lin) - 0.5)
        a = jax.nn.sigmoid(a0 + _dot_hi(wa, aup))
        g = _dot_hi(jax.nn.sigmoid(lora[:, 128:384]), gup)
        kk = k * kkp
        ss = _dot_xe(kk * kk, head_ones)
        kkn = kk / jnp.maximum(jnp.sqrt(ss), 1e-12)
        k2 = k * (1.0 + (a - 1.0) * kap)
        beta = kkn * a

        cum = _dot_ex(tri64, logw)
        cum_last = cum[C - 1:C, :]
        rt = r * jnp.exp(cum)
        kap_t = kkn * jnp.exp(cum - logw)
        ig = jnp.exp(-cum)
        kt = k2 * ig
        bt = beta * ig
        d_end = jnp.exp(cum_last - cum)
        kh = k2 * d_end
        bh = beta * d_end

        kap_s, rt_s, bt_s, kt_s, v_s = stack(kap_t), stack(rt), stack(bt), stack(kt), stack(v)
        lhs = jnp.concatenate([kap_s, rt_s], axis=0)
        gb = _dot_nt(lhs, bt_s)
        gk = _dot_nt(lhs, kt_s)
        lb = jnp.where(strict, gb[:2 * C], 0.0)
        lk = jnp.where(strict, gk[:2 * C], 0.0)
        arb = jnp.where(incl, gb[2 * C:], 0.0)
        ark = jnp.where(incl, gk[2 * C:], 0.0)

        ld = jnp.where(blk16, lb, 0.0)
        off = lb - ld
        m = -ld
        m2 = _dot(m, m)
        m3 = _dot(m, m2)
        m4 = _dot(m2, m2)
        m8 = _dot(m4, m4)
        m12 = _dot(m4, m8)
        dinv = _dot(eye + m + m2 + m3, eye + m4 + m8 + m12)
        n1 = _dot(dinv, off)
        n2 = _dot(n1, n1)
        x1 = dinv + _dot(n2, dinv)
        tinv = x1 - _dot(n1, x1)

        s = s_scr[...]
        ks = _dot_nt(lhs, s)
        y = _dot(tinv, ks[:2 * C] + _dot(lk, v_s))
        o_s = ks[2 * C:] + _dot(ark, v_s) - _dot(arb, y)
        o = o_s[:C] + o_s[C:]
        s_scr[...] = s * jnp.exp(cum_last) + _dot_tn(
            jnp.concatenate([v_s, y], axis=0), jnp.concatenate([stack(kh), -stack(bh)], axis=0))

        mean = _dot_xe(o, head_ones) * (1.0 / RW_HEAD)
        dlt = o - mean
        var = _dot_xe(dlt * dlt, head_ones) * (1.0 / RW_HEAD)
        on = dlt * lax.rsqrt(var + GN_EPS) * lnw + lnb
        bonus = _dot_xe(r * k2 * rkp, head_ones) * v
        o_ref[0, sl, :] = ((on + bonus) * g).astype(o_ref.dtype)
        return carry

    lax.fori_loop(0, n_chunks, chunk, 0)


def _rwkv(proj, w0, a0, k_k, k_a, r_k, ln_w, ln_b, w_up, a_up, g_up, tc):
    bsz, t, _ = proj.shape
    nh2 = RW // LANES
    row = lambda p: p.reshape(1, RW)
    wup_p = jnp.concatenate([w_up, jnp.zeros_like(a_up)], axis=0)
    aup_p = jnp.concatenate([jnp.zeros_like(w_up), a_up], axis=0)
    gup_p = jnp.concatenate([g_up, jnp.zeros((256 - g_up.shape[0], RW), F32)], axis=0)
    pspec = lambda off: pl.BlockSpec((1, tc, LANES), lambda b, h, i: (b, i, off // LANES + h))
    vec = pl.BlockSpec((1, LANES), lambda b, h, i: (0, h))
    return pl.pallas_call(
        _rwkv_kernel,
        grid=(bsz, nh2, t // tc),
        in_specs=[pspec(P_R), pspec(P_K), pspec(P_V),
                  pl.BlockSpec((1, tc, LORA_W), lambda b, h, i: (b, i, P_LORA // LORA_W)),
                  vec, vec, vec, vec, vec, vec, vec,
                  pl.BlockSpec((128, LANES), lambda b, h, i: (0, h)),
                  pl.BlockSpec((128, LANES), lambda b, h, i: (0, h)),
                  pl.BlockSpec((256, LANES), lambda b, h, i: (0, h))],
        out_specs=pl.BlockSpec((1, tc, LANES), lambda b, h, i: (b, i, h)),
        out_shape=jax.ShapeDtypeStruct((bsz, t, RW), BF16),
        scratch_shapes=[pltpu.VMEM((LANES, LANES), F32)],
        compiler_params=_cparams(3),
        name="rwkv",
    )(proj, proj, proj, proj, row(w0), row(a0), row(k_k), row(k_a), row(r_k), row(ln_w),
      row(ln_b), wup_p, aup_p, gup_p)


def _ssd_kernel(z_ref, x_ref, bc_ref, lora_ref, cwx_ref, cbx_ref, cwb_ref, cbb_ref, dtb_ref,
                alog_ref, dsk_ref, nw_ref, ex_ref, o_ref, extx, extb, st_scr):
    L = SSD_CHUNK

    @pl.when(pl.program_id(1) == 0)
    def _():
        extx[0:8, :] = jnp.zeros((8, extx.shape[1]), F32)
        extb[0:8, :] = jnp.zeros((8, extb.shape[1]), F32)
        st_scr[...] = jnp.zeros_like(st_scr)

    def conv(ext, cur, w_ref, b_ref):
        ext[8:8 + L, :] = cur
        acc = b_ref[...] + w_ref[0:1, :] * ext[5:5 + L, :]
        for j in range(1, SSM_CONV):
            acc = acc + w_ref[j:j + 1, :] * ext[5 + j:5 + j + L, :]
        ext[0:8, :] = ext[L:L + 8, :]
        return _silu(acc)

    xs = conv(extx, x_ref[0], cwx_ref, cbx_ref)
    bc = conv(extb, bc_ref[0], cwb_ref, cbb_ref)

    row = lax.broadcasted_iota(jnp.int32, (L, L), 0)
    col = lax.broadcasted_iota(jnp.int32, (L, L), 1)
    incl = row >= col
    tri = jnp.where(incl, 1.0, 0.0).astype(BF16)
    lane = lax.broadcasted_iota(jnp.int32, (1, LANES), 1)
    m_par = [(lane < SSM_HEAD).astype(F32), (lane >= SSM_HEAD).astype(F32)]

    ex = ex_ref[...]
    dts = jax.nn.softplus(lora_ref[0][:, LORA_DT:LORA_DT + LANES] + dtb_ref[...])
    a_slab = dts * (-jnp.exp(alog_ref[...]))
    acs_slab = _dot_ex(tri, a_slab)
    acs_t = acs_slab.T
    dtx = _dot_xe(dts, ex)
    acs_x = _dot_xe(acs_slab, ex)
    last = acs_x[L - 1:L, :]
    xdt = xs * dtx
    xdte = xdt * jnp.exp(last - acs_x)
    eacs = jnp.exp(acs_x)
    elast = jnp.exp(last)

    ys = []
    for j in range(SSM_HEADS // 2):
        g = j // (SSM_HEADS // 4)
        bm = bc[:, g * LANES:(g + 1) * LANES]
        cm = bc[:, (2 + g) * LANES:(3 + g) * LANES]
        pr = slice(j * LANES, (j + 1) * LANES)
        cb = _dot_nt(cm, bm)
        st = st_scr[j]
        y = eacs[:, pr] * _dot(cm, st)
        xp = xdt[:, pr]
        for par in range(2):
            h = 2 * j + par
            diff = acs_slab[:, h:h + 1] - acs_t[h:h + 1, :]
            mh = jnp.exp(jnp.where(incl, diff, -jnp.inf)) * cb
            y = y + _dot(mh, xp * m_par[par])
        st_scr[j] = st * elast[:, pr] + _dot_tn(bm, xdte[:, pr])
        ys.append(y)
    y = jnp.concatenate(ys, axis=1) + xs * dsk_ref[...]
    y = y * _silu(z_ref[0])
    half = SSM_W // 2
    outs = []
    for g in range(2):
        seg = y[:, g * half:(g + 1) * half]
        outs.append(seg * lax.rsqrt(jnp.mean(seg * seg, axis=-1, keepdims=True) + SSM_EPS))
    o_ref[0] = (jnp.concatenate(outs, axis=1) * nw_ref[...]).astype(o_ref.dtype)


def _ssd(proj, conv_w, conv_b, dt_bias, a_log, d_skip, norm_w):
    bsz, t, _ = proj.shape
    L = SSD_CHUNK
    pad_slab = lambda p: jnp.concatenate([p, jnp.zeros((LANES - SSM_HEADS,), F32)]).reshape(1, LANES)
    hh = jnp.arange(LANES)[:, None]
    ll = jnp.arange(SSM_W)[None, :]
    ex = (hh == ll // SSM_HEAD).astype(BF16)
    cwx, cwb = conv_w[:, :SSM_W], conv_w[:, SSM_W:]
    cbx, cbb = conv_b[:SSM_W].reshape(1, -1), conv_b[SSM_W:].reshape(1, -1)
    full = lambda a: pl.BlockSpec(a.shape, lambda b, i: (0,) * a.ndim)
    args = [cwx, cbx, cwb, cbb, pad_slab(dt_bias), pad_slab(a_log),
            jnp.repeat(d_skip, SSM_HEAD).reshape(1, SSM_W), norm_w.reshape(1, SSM_W), ex]
    return pl.pallas_call(
        _ssd_kernel,
        grid=(bsz, t // L),
        in_specs=[pl.BlockSpec((1, L, SSM_W), lambda b, i: (b, i, P_Z // SSM_W)),
                  pl.BlockSpec((1, L, SSM_W), lambda b, i: (b, i, P_XS // SSM_W)),
                  pl.BlockSpec((1, L, 512), lambda b, i: (b, i, P_BC // 512)),
                  pl.BlockSpec((1, L, LORA_W), lambda b, i: (b, i, P_LORA // LORA_W))]
                 + [full(a) for a in args],
        out_specs=pl.BlockSpec((1, L, SSM_W), lambda b, i: (b, i, 0)),
        out_shape=jax.ShapeDtypeStruct((bsz, t, SSM_W), BF16),
        scratch_shapes=[pltpu.VMEM((L + 8, SSM_W), F32), pltpu.VMEM((L + 8, 512), F32),
                        pltpu.VMEM((SSM_HEADS // 2, SSM_STATE, LANES), F32)],
        compiler_params=_cparams(2),
        name="ssd",
    )(proj, proj, proj, proj, *args)


def _outproj_kernel(orw_ref, oss_ref, x_ref, g1_ref, nw_ref, sh_ref, sc_ref, wo1_ref, wo2_ref,
                    wq_ref, h1_ref, n2_ref, q_ref):
    mix = _dot(orw_ref[0], wo1_ref[...]) + _dot(oss_ref[0], wo2_ref[...])
    h1 = x_ref[0] + g1_ref[0] * mix
    h1_ref[0] = h1
    n2 = _rms_mod(h1, nw_ref[...], sh_ref[0], sc_ref[0]).astype(BF16)
    n2_ref[0] = n2
    q_ref[0] = _dot(n2, wq_ref[...])


def _out_proj(o_rw, o_ss, x, gate1, norm_w, shift, scale, w_out, w_query_p, tm):
    bsz, t, d = x.shape
    wo1 = w_out[:RW].astype(BF16)
    wo2 = w_out[RW:].astype(BF16)
    tok = lambda w: pl.BlockSpec((1, tm, w), lambda b, i: (b, i, 0))
    per_b = pl.BlockSpec((1, 1, d), lambda b, i: (b, 0, 0))
    const = lambda a: pl.BlockSpec(a.shape, lambda b, i: (0,) * a.ndim, pipeline_mode=pl.Buffered(1))
    nw = norm_w.reshape(1, d)
    return pl.pallas_call(
        _outproj_kernel,
        grid=(bsz, t // tm),
        in_specs=[tok(RW), tok(SSM_W), tok(d), per_b, const(nw), per_b, per_b,
                  const(wo1), const(wo2), const(w_query_p)],
        out_specs=[tok(d), tok(d), tok(d)],
        out_shape=[jax.ShapeDtypeStruct((bsz, t, d), F32), jax.ShapeDtypeStruct((bsz, t, d), BF16),
                   jax.ShapeDtypeStruct((bsz, t, d), F32)],
        compiler_params=_cparams(2),
        name="out_proj",
    )(o_rw, o_ss, x, gate1.reshape(bsz, 1, d), nw, shift.reshape(bsz, 1, d),
      scale.reshape(bsz, 1, d), wo1, wo2, w_query_p)


_CANDS = [(i, j) for i in range(PEER_TOPK) for j in range(PEER_TOPK) if (i + 1) * (j + 1) <= PEER_TOPK]


def _route_kernel(q_ref, kbd_ref, khn_ref, th_ref, e1_ref, s2_ref, e2_ref):
    q = q_ref[...]
    tm = q.shape[0]
    hw = PEER_HEADS * PEER_HALF
    q1, q2 = q[:, :hw], q[:, hw:]
    nk, nh = PEER_KEYS, PEER_HEADS
    s1 = _dot_nt(kbd_ref[0], q1).reshape(nk, nh, tm)
    s2 = _dot_nt(kbd_ref[1], q2).reshape(nk, nh, tm)

    def top_vals(s, n):
        vals, prev = [], None
        for _ in range(n):
            cand = s if prev is None else jnp.where(s < prev[None], s, -jnp.inf)
            prev = jnp.max(cand, axis=0)
            vals.append(prev)
        return vals

    a = top_vals(s1, PEER_TOPK)
    b = top_vals(s2, PEER_TOPK + 1)
    sums = {c: a[c[0]] + b[c[1]] for c in _CANDS}
    tau = None
    for c in _CANDS:
        cnt = jnp.zeros_like(sums[c])
        for c2 in _CANDS:
            cnt = cnt + jnp.where(sums[c2] >= sums[c], 1.0, 0.0)
        tc = jnp.where(cnt <= float(PEER_TOPK), sums[c], jnp.inf)
        tau = tc if tau is None else jnp.minimum(tau, tc)
    ea = [jnp.exp(ai - a[0]) for ai in a]
    eb = [jnp.exp(bj - b[0]) for bj in b[:PEER_TOPK]]
    zsum = jnp.zeros_like(tau)
    theta = [jnp.full_like(tau, jnp.inf) for _ in range(PEER_TOPK)]
    for (i, j) in _CANDS:
        sel = sums[(i, j)] >= tau
        zsum = zsum + jnp.where(sel, ea[i] * eb[j], 0.0)
        theta[i] = jnp.where(sel, 0.5 * (b[j] + b[j + 1]), theta[i])
    th3 = jnp.full_like(s1, jnp.inf)
    for i in range(PEER_TOPK):
        th3 = jnp.where(s1 == a[i][None], theta[i][None], th3)
    th_ref[...] = th3
    e1_ref[...] = jnp.exp(s1 - a[0][None])
    zinv = 1.0 / zsum
    for h in range(nh):
        s2h = _dot_nt(khn_ref[h], q2[:, h * PEER_HALF:(h + 1) * PEER_HALF])
        s2_ref[h] = s2h
        e2_ref[h] = jnp.exp(s2h - b[0][h:h + 1, :]) * zinv[h:h + 1, :]


def _peer_route(q, sub_keys, tm):
    ntok = q.shape[0]
    nk, nh, hd = PEER_KEYS, PEER_HEADS, PEER_HALF
    eye = jnp.eye(nh, dtype=F32)
    kbd = jnp.einsum('hpnd,hg->pnhgd', sub_keys, eye).reshape(2, nk * nh, nh * hd).astype(BF16)
    khn = sub_keys[:, 1].astype(BF16)
    out3 = jax.ShapeDtypeStruct((nk, nh, ntok), F32)
    out3h = jax.ShapeDtypeStruct((nh, nk, ntok), F32)
    return pl.pallas_call(
        _route_kernel,
        grid=(ntok // tm,),
        in_specs=[pl.BlockSpec((tm, q.shape[1]), lambda i: (i, 0)),
                  pl.BlockSpec(kbd.shape, lambda i: (0, 0, 0)),
                  pl.BlockSpec(khn.shape, lambda i: (0, 0, 0))],
        out_specs=[pl.BlockSpec((nk, nh, tm), lambda i: (0, 0, i)),
                   pl.BlockSpec((nk, nh, tm), lambda i: (0, 0, i)),
                   pl.BlockSpec((nh, nk, tm), lambda i: (0, 0, i)),
                   pl.BlockSpec((nh, nk, tm), lambda i: (0, 0, i))],
        out_shape=[out3, out3, out3h, out3h],
        compiler_params=_cparams(1),
        name="peer_route",
    )(q, kbd, khn)


def _dense_kernel(n2_ref, u_ref, vt_ref, th_ref, e1_ref, s2_ref, e2_ref, o_ref, w_scr):
    j = pl.program_id(1)
    act = _dot_nt(u_ref[...], n2_ref[...])
    n_slab = act.shape[0] // PEER_KEYS
    for sl in range(n_slab):
        a = act[sl * PEER_KEYS:(sl + 1) * PEER_KEYS, :]
        th = th_ref[sl]
        e1 = e1_ref[sl]
        gsum = jnp.zeros_like(a)
        for h in range(PEER_HEADS):
            sel = s2_ref[h] >= th[h:h + 1, :]
            gsum = gsum + jnp.where(sel, e2_ref[h], 0.0) * e1[h:h + 1, :]
        w = gsum * (0.5 * a * (1.0 + lax.erf(a * (2.0 ** -0.5))))
        w_scr[sl * PEER_KEYS:(sl + 1) * PEER_KEYS, :] = w.astype(BF16)
    contrib = _dot(vt_ref[...], w_scr[...])

    @pl.when(j == 0)
    def _():
        o_ref[...] = contrib

    @pl.when(j > 0)
    def _():
        o_ref[...] += contrib


def _peer_dense(n2, u_bf, vt_bf, th, e1, s2, e2, tm, ec):
    ntok, d = n2.shape
    n_exp = u_bf.shape[0]
    ns = ec // PEER_KEYS
    return pl.pallas_call(
        _dense_kernel,
        grid=(ntok // tm, n_exp // ec),
        in_specs=[pl.BlockSpec((tm, d), lambda i, j: (i, 0)),
                  pl.BlockSpec((ec, d), lambda i, j: (j, 0)),
                  pl.BlockSpec((d, ec), lambda i, j: (0, j)),
                  pl.BlockSpec((ns, PEER_HEADS, tm), lambda i, j: (j, 0, i)),
                  pl.BlockSpec((ns, PEER_HEADS, tm), lambda i, j: (j, 0, i)),
                  pl.BlockSpec((PEER_HEADS, PEER_KEYS, tm), lambda i, j: (0, 0, i)),
                  pl.BlockSpec((PEER_HEADS, PEER_KEYS, tm), lambda i, j: (0, 0, i))],
        out_specs=pl.BlockSpec((d, tm), lambda i, j: (0, i)),
        out_shape=jax.ShapeDtypeStruct((d, ntok), F32),
        scratch_shapes=[pltpu.VMEM((ec, tm), BF16)],
        compiler_params=_cparams(2),
        name="peer_dense",
    )(n2, u_bf, vt_bf, th, e1, s2, e2)


def _final_kernel(h1_ref, pt_ref, g2_ref, nw_ref, o_ref):
    h = h1_ref[0] + g2_ref[0] * pt_ref[...].T
    ms = jnp.mean(h * h, axis=-1, keepdims=True)
    o_ref[0] = (h * lax.rsqrt(ms + RMS_EPS) * nw_ref[...]).astype(o_ref.dtype)


def _final(h1, peer_t, gate2, norm_w, out_dtype, tm):
    bsz, t, d = h1.shape
    nt = t // tm
    return pl.pallas_call(
        _final_kernel,
        grid=(bsz, nt),
        in_specs=[pl.BlockSpec((1, tm, d), lambda b, i: (b, i, 0)),
                  pl.BlockSpec((d, tm), lambda b, i: (0, b * nt + i)),
                  pl.BlockSpec((1, 1, d), lambda b, i: (b, 0, 0)),
                  pl.BlockSpec((1, d), lambda b, i: (0, 0))],
        out_specs=pl.BlockSpec((1, tm, d), lambda b, i: (b, i, 0)),
        out_shape=jax.ShapeDtypeStruct((bsz, t, d), out_dtype),
        compiler_params=_cparams(2),
        name="final_norm",
    )(h1, peer_t, gate2.reshape(bsz, 1, d), norm_w.reshape(1, d))


def _pack_in_proj(w_in, mu):
    d = w_in.shape[0]
    rp = 3 * RW + 64 + 64 + 160
    zc = lambda n: jnp.zeros((d, n), w_in.dtype)
    z0 = rp
    dt0 = rp + SSM_W + SSM_W + 512
    w = jnp.concatenate([
        w_in[:, :3 * RW],
        w_in[:, z0:z0 + SSM_W],
        w_in[:, z0 + SSM_W:z0 + 2 * SSM_W],
        w_in[:, z0 + 2 * SSM_W:z0 + 2 * SSM_W + 512],
        w_in[:, 3 * RW:rp], zc(LORA_DT - 288),
        w_in[:, dt0:dt0 + SSM_HEADS], zc(LORA_W - LORA_DT - SSM_HEADS)], axis=1)
    m = jnp.concatenate([mu[:3 * RW], jnp.zeros((P_LORA - 3 * RW,), mu.dtype), mu[3 * RW:rp],
                         jnp.zeros((LORA_W - 288,), mu.dtype)])
    return w.astype(BF16), m.reshape(1, P_TOT)


def _tile(t, pref):
    return pref if t % pref == 0 else t


def kernel(x, c, ada_w, ada_b, norm1_w, w_in, rwkv_mu, rwkv_w0, rwkv_w_up, rwkv_a0, rwkv_a_up, rwkv_g_up, rwkv_k_k, rwkv_k_a, rwkv_r_k, rwkv_ln_w, rwkv_ln_b, mamba_conv_w, mamba_conv_b, mamba_dt_bias, mamba_a_log, mamba_d, mamba_norm_w, w_out, norm2_w, peer_w_query, peer_sub_keys, peer_u, peer_v, final_norm_w):
    out_dtype = x.dtype
    bsz, t, d = x.shape
    h = x.astype(F32)
    depth = ada_w.shape[0]
    for i in range(depth):
        mod = _ada_mod(c.astype(F32), ada_w[i], ada_b[i])
        shift1, scale1, gate1, shift2, scale2, gate2 = jnp.split(mod, 6, axis=-1)
        w_packed, mu_packed = _pack_in_proj(w_in[i], rwkv_mu[i])
        proj = _in_proj(h, norm1_w[i], shift1, scale1, w_packed, mu_packed, _tile(t, 512))
        o_rw = _rwkv(proj, rwkv_w0[i], rwkv_a0[i], rwkv_k_k[i], rwkv_k_a[i], rwkv_r_k[i].reshape(-1),
                     rwkv_ln_w[i], rwkv_ln_b[i], rwkv_w_up[i], rwkv_a_up[i], rwkv_g_up[i],
                     _tile(t, 512))
        o_ss = _ssd(proj, mamba_conv_w[i], mamba_conv_b[i], mamba_dt_bias[i], mamba_a_log[i],
                    mamba_d[i], mamba_norm_w[i])
        wq = peer_w_query[i].reshape(d, PEER_HEADS, 2, PEER_HALF).transpose(0, 2, 1, 3)
        wq = wq.reshape(d, 2 * PEER_HEADS * PEER_HALF).astype(BF16)
        h1, n2, q = _out_proj(o_rw, o_ss, h, gate1, norm2_w[i], shift2, scale2, w_out[i], wq,
                              _tile(t, 256))
        ntok = bsz * t
        th, e1, s2, e2 = _peer_route(q.reshape(ntok, -1), peer_sub_keys[i], _tile(ntok, 256))
        peer_t = _peer_dense(n2.reshape(ntok, d), peer_u[i].astype(BF16), peer_v[i].T.astype(BF16),
                             th, e1, s2, e2, _tile(ntok, 512), 1024)
        if i + 1 < depth:
            h = h1 + gate2[:, None, :] * peer_t.T.reshape(bsz, t, d)
        else:
            return _final(h1, peer_t, gate2, final_norm_w, out_dtype, _tile(t, 512))
```

```python
import functools

import jax
import jax.numpy as jnp
from jax import lax
from jax.experimental import pallas as pl
from jax.experimental.pallas import tpu as pltpu

F32 = jnp.float32
BF16 = jnp.bfloat16

D_MODEL = 2048
RW = 1024
RW_HEAD = 64
SSM_W = 1024
SSM_HEAD = 64
SSM_HEADS = 16
SSM_STATE = 128
SSM_CONV = 4
PEER_HEADS = 8
PEER_KEYS = 128
PEER_HALF = 128
PEER_TOPK = 16
RMS_EPS = 1e-6
GN_EPS = 64e-5
SSM_EPS = 1e-5

P_R, P_K, P_V, P_Z, P_XS, P_BC, P_LORA = 0, 1024, 2048, 3072, 4096, 5120, 5632
P_TOT = 6144
LORA_W = 512
LORA_DT = 384

RWKV_CHUNK = 64
SSD_CHUNK = 128
LANES = 128

VMEM_LIMIT = 56 * 1024 * 1024


def _cparams(n_axes):
    return pltpu.CompilerParams(
        dimension_semantics=("arbitrary",) * n_axes, vmem_limit_bytes=VMEM_LIMIT)


def _split3(x):
    h1 = x.astype(BF16)
    r1 = x - h1.astype(F32)
    h2 = r1.astype(BF16)
    r2 = r1 - h2.astype(F32)
    return h1, h2, r2.astype(BF16)


def _mm(a, b, dims):
    return lax.dot_general(a.astype(BF16), b.astype(BF16), (dims, ((), ())),
                           preferred_element_type=F32)


def _dot(a, b):
    return _mm(a, b, ((1,), (0,)))


def _dot_nt(a, b):
    return _mm(a, b, ((1,), (1,)))


def _dot_tn(a, b):
    return _mm(a, b, ((0,), (0,)))


def _dot_xe(x, e_bf16):
    h1, h2, h3 = _split3(x)
    return _dot(h1, e_bf16) + _dot(h2, e_bf16) + _dot(h3, e_bf16)


def _dot_ex(e_bf16, x):
    h1, h2, h3 = _split3(x)
    return _dot(e_bf16, h1) + _dot(e_bf16, h2) + _dot(e_bf16, h3)


def _dot_hi(a, b):
    ah = a.astype(BF16)
    al = (a - ah.astype(F32)).astype(BF16)
    bh = b.astype(BF16)
    bl = (b - bh.astype(F32)).astype(BF16)
    return _dot(ah, bh) + _dot(al, bh) + _dot(ah, bl)


def _silu(x):
    return x * jax.nn.sigmoid(x)


def _ada_kernel(cb_ref, w_ref, b_ref, o_ref):
    w = w_ref[...]
    reps = w.shape[1] // LANES
    rows = []
    for b in range(cb_ref.shape[0]):
        cb = _silu(cb_ref[b])
        rows.append(jnp.sum(w * pltpu.repeat(cb, reps, axis=1), axis=0, keepdims=True))
    o_ref[...] = jnp.concatenate(rows, axis=0) + b_ref[...]


def _ada_mod(c, ada_w, ada_b):
    bsz, d = c.shape
    n = ada_w.shape[1]
    tn = 1024
    cb = jnp.broadcast_to(c[:, :, None], (bsz, d, LANES))
    return pl.pallas_call(
        _ada_kernel,
        grid=(n // tn,),
        in_specs=[pl.BlockSpec((bsz, d, LANES), lambda j: (0, 0, 0)),
                  pl.BlockSpec((d, tn), lambda j: (0, j)),
                  pl.BlockSpec((1, tn), lambda j: (0, j))],
        out_specs=pl.BlockSpec((bsz, tn), lambda j: (0, j)),
        out_shape=jax.ShapeDtypeStruct((bsz, n), F32),
        compiler_params=_cparams(1),
        name="ada_mod",
    )(cb, ada_w, ada_b.reshape(1, n))


def _rms_mod(x, w, shift, scale):
    ms = jnp.mean(x * x, axis=-1, keepdims=True)
    return (x * lax.rsqrt(ms + RMS_EPS) * w) * (1.0 + scale) + shift


def _inproj_kernel(x_ref, nw_ref, sh_ref, sc_ref, w_ref, mu_ref, o_ref, n_scr, carry_scr):
    i = pl.program_id(1)
    j = pl.program_id(2)

    @pl.when(j == 0)
    def _():
        n_scr[...] = _rms_mod(x_ref[0], nw_ref[...], sh_ref[0], sc_ref[0]).astype(BF16)

    p = _dot(n_scr[...], w_ref[...])
    tm = p.shape[0]
    prev_last = jnp.where(i == 0, 0.0, carry_scr[j][7:8, :])
    row = lax.broadcasted_iota(jnp.int32, p.shape, 0)
    shifted = jnp.where(row == 0, prev_last, pltpu.roll(p, 1, 0))
    carry_scr[j] = p[tm - 8:tm, :]
    o_ref[0] = p + (shifted - p) * mu_ref[...]


def _in_proj(x, norm_w, shift, scale, w_packed, mu_packed, tm):
    bsz, t, d = x.shape
    tn = 1536
    nb = P_TOT // tn
    return pl.pallas_call(
        _inproj_kernel,
        grid=(bsz, t // tm, nb),
        in_specs=[pl.BlockSpec((1, tm, d), lambda b, i, j: (b, i, 0)),
                  pl.BlockSpec((1, d), lambda b, i, j: (0, 0)),
                  pl.BlockSpec((1, 1, d), lambda b, i, j: (b, 0, 0)),
                  pl.BlockSpec((1, 1, d), lambda b, i, j: (b, 0, 0)),
                  pl.BlockSpec((d, tn), lambda b, i, j: (0, j)),
                  pl.BlockSpec((1, tn), lambda b, i, j: (0, j))],
        out_specs=pl.BlockSpec((1, tm, tn), lambda b, i, j: (b, i, j)),
        out_shape=jax.ShapeDtypeStruct((bsz, t, P_TOT), F32),
        scratch_shapes=[pltpu.VMEM((tm, d), BF16), pltpu.VMEM((nb, 8, tn), F32)],
        compiler_params=_cparams(3),
        name="in_proj",
    )(x, norm_w.reshape(1, d), shift.reshape(bsz, 1, d), scale.reshape(bsz, 1, d),
      w_packed, mu_packed)


RWKV_PAIRS = 8


def _rwkv_kernel(r_ref, k_ref, v_ref, lora_ref, w0_ref, a0_ref, kk_ref, ka_ref, rk_ref,
                 lnw_ref, lnb_ref, wup_ref, aup_ref, gup_ref, o_ref,
                 s_scr, kkn_scr, k2_scr, beta_scr, logw_scr, cum_scr, o_scr):
    C = RWKV_CHUNK
    tc = r_ref.shape[1]
    n_chunks = tc // C
    n_pairs = r_ref.shape[2] // LANES

    @pl.when(pl.program_id(2) == 0)
    def _():
        s_scr[...] = jnp.zeros_like(s_scr)

    lane = lax.broadcasted_iota(jnp.int32, (1, LANES), 1)
    first = lane < RW_HEAD
    m0 = first.astype(F32)
    m1 = 1.0 - m0
    row = lax.broadcasted_iota(jnp.int32, (LANES, LANES), 0)
    col = lax.broadcasted_iota(jnp.int32, (LANES, LANES), 1)
    strict = row > col
    incl = row >= col
    blk16 = (row // 16) == (col // 16)
    eye = jnp.where(row == col, 1.0, 0.0)
    r64 = lax.broadcasted_iota(jnp.int32, (C, C), 0)
    c64 = lax.broadcasted_iota(jnp.int32, (C, C), 1)
    tri64 = jnp.where(r64 >= c64, 1.0, 0.0).astype(BF16)

    def per_head(x):
        cols = []
        for p in range(n_pairs):
            xp = x[:, p * LANES:(p + 1) * LANES]
            s0 = jnp.sum(xp * m0, axis=-1, keepdims=True)
            s1 = jnp.sum(xp * m1, axis=-1, keepdims=True)
            cols.append(jnp.where(first, s0, s1))
        return jnp.concatenate(cols, axis=1)

    lora = lora_ref[0]
    wa = lora[:, 0:128]
    w_lin = w0_ref[...] + _dot_hi(jnp.tanh(wa), wup_ref[...])
    logw = -jnp.exp(-jax.nn.softplus(-w_lin) - 0.5)
    a = jax.nn.sigmoid(a0_ref[...] + _dot_hi(wa, aup_ref[...]))
    k = k_ref[0]
    kk = k * kk_ref[...]
    kkn = kk / jnp.maximum(jnp.sqrt(per_head(kk * kk)), 1e-12)
    kkn_scr[...] = kkn
    k2_scr[...] = k * (1.0 + (a - 1.0) * ka_ref[...])
    beta_scr[...] = kkn * a
    logw_scr[...] = logw
    for c in range(n_chunks):
        cum_scr[c * C:(c + 1) * C, :] = _dot_ex(tri64, logw[c * C:(c + 1) * C, :])

    def stack(x):
        return jnp.concatenate([x * m0, x * m1], axis=0)

    def chunk(ci, carry):
        sl = pl.ds(pl.multiple_of(ci * C, C), C)
        P = range(n_pairs)
        lns = [slice(p * LANES, (p + 1) * LANES) for p in P]
        cum = [cum_scr[sl, ln] for ln in lns]
        cum_last = [c[C - 1:C, :] for c in cum]
        ig = [jnp.exp(-c) for c in cum]
        d_end = [jnp.exp(cl - c) for cl, c in zip(cum_last, cum)]
        k2 = [k2_scr[sl, ln] for ln in lns]
        beta = [beta_scr[sl, ln] for ln in lns]
        v_s = [stack(v_ref[0, sl, ln]) for ln in lns]
        rt_s = [stack(r_ref[0, sl, ln] * jnp.exp(c)) for ln, c in zip(lns, cum)]
        kap_s = [stack(kkn_scr[sl, ln] * jnp.exp(c - logw_scr[sl, ln])) for ln, c in zip(lns, cum)]
        bt_s = [stack(b * g) for b, g in zip(beta, ig)]
        kt_s = [stack(k * g) for k, g in zip(k2, ig)]
        upd = [jnp.concatenate([stack(k * d), -stack(b * d)], axis=0)
               for k, b, d in zip(k2, beta, d_end)]
        lhs = [jnp.concatenate([a, b], axis=0) for a, b in zip(kap_s, rt_s)]
        gb = [_dot_nt(l, b) for l, b in zip(lhs, bt_s)]
        gk = [_dot_nt(l, k) for l, k in zip(lhs, kt_s)]
        lb = [jnp.where(strict, g[:2 * C], 0.0) for g in gb]
        lk = [jnp.where(strict, g[:2 * C], 0.0) for g in gk]
        arb = [jnp.where(incl, g[2 * C:], 0.0) for g in gb]
        ark = [jnp.where(incl, g[2 * C:], 0.0) for g in gk]

        ld = [jnp.where(blk16, x, 0.0) for x in lb]
        off = [x - d for x, d in zip(lb, ld)]
        m = [-d for d in ld]
        m2 = [_dot(a, a) for a in m]
        m3 = [_dot(a, b) for a, b in zip(m, m2)]
        m4 = [_dot(a, a) for a in m2]
        m8 = [_dot(a, a) for a in m4]
        m12 = [_dot(a, b) for a, b in zip(m4, m8)]
        dinv = [_dot(eye + a + b + c, eye + d + e + f)
                for a, b, c, d, e, f in zip(m, m2, m3, m4, m8, m12)]
        n1 = [_dot(d, o) for d, o in zip(dinv, off)]
        n2 = [_dot(a, a) for a in n1]
        x1 = [d + _dot(a, d) for d, a in zip(dinv, n2)]
        tinv = [x - _dot(a, x) for x, a in zip(x1, n1)]

        s = [s_scr[p] for p in P]
        ks = [_dot_nt(l, st) for l, st in zip(lhs, s)]
        lkv = [_dot(a, b) for a, b in zip(lk, v_s)]
        y = [_dot(t, k[:2 * C] + x) for t, k, x in zip(tinv, ks, lkv)]
        sn = [_dot_tn(jnp.concatenate([vs, yy], axis=0), u) for vs, yy, u in zip(v_s, y, upd)]
        for p in P:
            s_scr[p] = s[p] * jnp.exp(cum_last[p]) + sn[p]
        arkv = [_dot(a, b) for a, b in zip(ark, v_s)]
        arby = [_dot(a, b) for a, b in zip(arb, y)]
        for p in P:
            o_s = ks[p][2 * C:] + arkv[p] - arby[p]
            o_scr[sl, lns[p]] = o_s[:C] + o_s[C:]
        return carry

    lax.fori_loop(0, n_chunks, chunk, 0)

    o = o_scr[...]
    mean = per_head(o) * (1.0 / RW_HEAD)
    dlt = o - mean
    var = per_head(dlt * dlt) * (1.0 / RW_HEAD)
    on = dlt * lax.rsqrt(var + GN_EPS) * lnw_ref[...] + lnb_ref[...]
    bonus = per_head(r_ref[0] * k2_scr[...] * rk_ref[...]) * v_ref[0]
    g = _dot_hi(jax.nn.sigmoid(lora[:, 128:384]), gup_ref[...])
    o_ref[0] = ((on + bonus) * g).astype(o_ref.dtype)


def _rwkv(proj, w0, a0, k_k, k_a, r_k, ln_w, ln_b, w_up, a_up, g_up, tc):
    bsz, t, _ = proj.shape
    wl = RWKV_PAIRS * LANES
    row = lambda p: p.reshape(1, RW)
    wup_p = jnp.concatenate([w_up, jnp.zeros_like(a_up)], axis=0)
    aup_p = jnp.concatenate([jnp.zeros_like(w_up), a_up], axis=0)
    gup_p = jnp.concatenate([g_up, jnp.zeros((256 - g_up.shape[0], RW), F32)], axis=0)
    pspec = lambda off: pl.BlockSpec((1, tc, wl), lambda b, h, i: (b, i, off // wl + h))
    vec = pl.BlockSpec((1, wl), lambda b, h, i: (0, h))
    tile = pltpu.VMEM((tc, wl), F32)
    return pl.pallas_call(
        _rwkv_kernel,
        grid=(bsz, RW // wl, t // tc),
        in_specs=[pspec(P_R), pspec(P_K), pspec(P_V),
                  pl.BlockSpec((1, tc, LORA_W), lambda b, h, i: (b, i, P_LORA // LORA_W)),
                  vec, vec, vec, vec, vec, vec, vec,
                  pl.BlockSpec((128, wl), lambda b, h, i: (0, h)),
                  pl.BlockSpec((128, wl), lambda b, h, i: (0, h)),
                  pl.BlockSpec((256, wl), lambda b, h, i: (0, h))],
        out_specs=pl.BlockSpec((1, tc, wl), lambda b, h, i: (b, i, h)),
        out_shape=jax.ShapeDtypeStruct((bsz, t, RW), BF16),
        scratch_shapes=[pltpu.VMEM((RWKV_PAIRS, LANES, LANES), F32),
                        tile, tile, tile, tile, tile, tile],
        compiler_params=_cparams(3),
        name="rwkv",
    )(proj, proj, proj, proj, row(w0), row(a0), row(k_k), row(k_a), row(r_k), row(ln_w),
      row(ln_b), wup_p, aup_p, gup_p)


def _ssd_kernel(z_ref, x_ref, bc_ref, lora_ref, cwx_ref, cbx_ref, cwb_ref, cbb_ref, dtb_ref,
                alog_ref, dsk_ref, nw_ref, ex_ref, o_ref, extx, extb, st_scr):
    L = SSD_CHUNK

    @pl.when(pl.program_id(1) == 0)
    def _():
        extx[0:8, :] = jnp.zeros((8, extx.shape[1]), F32)
        extb[0:8, :] = jnp.zeros((8, extb.shape[1]), F32)
        st_scr[...] = jnp.zeros_like(st_scr)

    def conv(ext, cur, w_ref, b_ref):
        ext[8:8 + L, :] = cur
        acc = b_ref[...] + w_ref[0:1, :] * ext[5:5 + L, :]
        for j in range(1, SSM_CONV):
            acc = acc + w_ref[j:j + 1, :] * ext[5 + j:5 + j + L, :]
        ext[0:8, :] = ext[L:L + 8, :]
        return _silu(acc)

    xs = conv(extx, x_ref[0], cwx_ref, cbx_ref)
    bc = conv(extb, bc_ref[0], cwb_ref, cbb_ref)

    row = lax.broadcasted_iota(jnp.int32, (L, L), 0)
    col = lax.broadcasted_iota(jnp.int32, (L, L), 1)
    incl = row >= col
    tri = jnp.where(incl, 1.0, 0.0).astype(BF16)
    lane = lax.broadcasted_iota(jnp.int32, (1, LANES), 1)
    m_par = [(lane < SSM_HEAD).astype(F32), (lane >= SSM_HEAD).astype(F32)]

    ex = ex_ref[...]
    dts = jax.nn.softplus(lora_ref[0][:, LORA_DT:LORA_DT + LANES] + dtb_ref[...])
    a_slab = dts * (-jnp.exp(alog_ref[...]))
    acs_slab = _dot_ex(tri, a_slab)
    acs_t = acs_slab.T
    dtx = _dot_xe(dts, ex)
    acs_x = _dot_xe(acs_slab, ex)
    last = acs_x[L - 1:L, :]
    xdt = xs * dtx
    xdte = xdt * jnp.exp(last - acs_x)
    eacs = jnp.exp(acs_x)
    elast = jnp.exp(last)

    ys = []
    for j in range(SSM_HEADS // 2):
        g = j // (SSM_HEADS // 4)
        bm = bc[:, g * LANES:(g + 1) * LANES]
        cm = bc[:, (2 + g) * LANES:(3 + g) * LANES]
        pr = slice(j * LANES, (j + 1) * LANES)
        cb = _dot_nt(cm, bm)
        st = st_scr[j]
        y = eacs[:, pr] * _dot(cm, st)
        xp = xdt[:, pr]
        for par in range(2):
            h = 2 * j + par
            diff = acs_slab[:, h:h + 1] - acs_t[h:h + 1, :]
            mh = jnp.exp(jnp.where(incl, diff, -jnp.inf)) * cb
            y = y + _dot(mh, xp * m_par[par])
        st_scr[j] = st * elast[:, pr] + _dot_tn(bm, xdte[:, pr])
        ys.append(y)
    y = jnp.concatenate(ys, axis=1) + xs * dsk_ref[...]
    y = y * _silu(z_ref[0])
    half = SSM_W // 2
    outs = []
    for g in range(2):
        seg = y[:, g * half:(g + 1) * half]
        outs.append(seg * lax.rsqrt(jnp.mean(seg * seg, axis=-1, keepdims=True) + SSM_EPS))
    o_ref[0] = (jnp.concatenate(outs, axis=1) * nw_ref[...]).astype(o_ref.dtype)


def _ssd(proj, conv_w, conv_b, dt_bias, a_log, d_skip, norm_w):
    bsz, t, _ = proj.shape
    L = SSD_CHUNK
    pad_slab = lambda p: jnp.concatenate([p, jnp.zeros((LANES - SSM_HEADS,), F32)]).reshape(1, LANES)
    hh = jnp.arange(LANES)[:, None]
    ll = jnp.arange(SSM_W)[None, :]
    ex = (hh == ll // SSM_HEAD).astype(BF16)
    cwx, cwb = conv_w[:, :SSM_W], conv_w[:, SSM_W:]
    cbx, cbb = conv_b[:SSM_W].reshape(1, -1), conv_b[SSM_W:].reshape(1, -1)
    full = lambda a: pl.BlockSpec(a.shape, lambda b, i: (0,) * a.ndim)
    args = [cwx, cbx, cwb, cbb, pad_slab(dt_bias), pad_slab(a_log),
            jnp.repeat(d_skip, SSM_HEAD).reshape(1, SSM_W), norm_w.reshape(1, SSM_W), ex]
    return pl.pallas_call(
        _ssd_kernel,
        grid=(bsz, t // L),
        in_specs=[pl.BlockSpec((1, L, SSM_W), lambda b, i: (b, i, P_Z // SSM_W)),
                  pl.BlockSpec((1, L, SSM_W), lambda b, i: (b, i, P_XS // SSM_W)),
                  pl.BlockSpec((1, L, 512), lambda b, i: (b, i, P_BC // 512)),
                  pl.BlockSpec((1, L, LORA_W), lambda b, i: (b, i, P_LORA // LORA_W))]
                 + [full(a) for a in args],
        out_specs=pl.BlockSpec((1, L, SSM_W), lambda b, i: (b, i, 0)),
        out_shape=jax.ShapeDtypeStruct((bsz, t, SSM_W), BF16),
        scratch_shapes=[pltpu.VMEM((L + 8, SSM_W), F32), pltpu.VMEM((L + 8, 512), F32),
                        pltpu.VMEM((SSM_HEADS // 2, SSM_STATE, LANES), F32)],
        compiler_params=_cparams(2),
        name="ssd",
    )(proj, proj, proj, proj, *args)


def _outproj_kernel(orw_ref, oss_ref, x_ref, g1_ref, nw_ref, sh_ref, sc_ref, wo1_ref, wo2_ref,
                    wq_ref, h1_ref, n2_ref, q_ref):
    mix = _dot(orw_ref[0], wo1_ref[...]) + _dot(oss_ref[0], wo2_ref[...])
    h1 = x_ref[0] + g1_ref[0] * mix
    h1_ref[0] = h1
    n2 = _rms_mod(h1, nw_ref[...], sh_ref[0], sc_ref[0]).astype(BF16)
    n2_ref[0] = n2
    q_ref[0] = _dot(n2, wq_ref[...])


def _out_proj(o_rw, o_ss, x, gate1, norm_w, shift, scale, w_out, w_query_p, tm):
    bsz, t, d = x.shape
    wo1 = w_out[:RW].astype(BF16)
    wo2 = w_out[RW:].astype(BF16)
    tok = lambda w: pl.BlockSpec((1, tm, w), lambda b, i: (b, i, 0))
    per_b = pl.BlockSpec((1, 1, d), lambda b, i: (b, 0, 0))
    const = lambda a: pl.BlockSpec(a.shape, lambda b, i: (0,) * a.ndim, pipeline_mode=pl.Buffered(1))
    nw = norm_w.reshape(1, d)
    return pl.pallas_call(
        _outproj_kernel,
        grid=(bsz, t // tm),
        in_specs=[tok(RW), tok(SSM_W), tok(d), per_b, const(nw), per_b, per_b,
                  const(wo1), const(wo2), const(w_query_p)],
        out_specs=[tok(d), tok(d), tok(d)],
        out_shape=[jax.ShapeDtypeStruct((bsz, t, d), F32), jax.ShapeDtypeStruct((bsz, t, d), BF16),
                   jax.ShapeDtypeStruct((bsz, t, d), F32)],
        compiler_params=_cparams(2),
        name="out_proj",
    )(o_rw, o_ss, x, gate1.reshape(bsz, 1, d), nw, shift.reshape(bsz, 1, d),
      scale.reshape(bsz, 1, d), wo1, wo2, w_query_p)


_CANDS = [(i, j) for i in range(PEER_TOPK) for j in range(PEER_TOPK) if (i + 1) * (j + 1) <= PEER_TOPK]


def _route_kernel(q_ref, k1_ref, k2_ref, cnt_ref, e1_ref, rk_ref, e2_ref):
    nh, hd, K = PEER_HEADS, PEER_HALF, PEER_TOPK
    q = q_ref[...]

    def top_vals(s):
        vals, prev = [], None
        for _ in range(K):
            cand = s if prev is None else jnp.where(s < prev, s, -jnp.inf)
            prev = jnp.max(cand, axis=0, keepdims=True)
            vals.append(prev)
        return vals

    s1 = [_dot_nt(k1_ref[h], q[:, h * hd:(h + 1) * hd]) for h in range(nh)]
    s2 = [_dot_nt(k2_ref[h], q[:, (nh + h) * hd:(nh + h + 1) * hd]) for h in range(nh)]
    a_h = [top_vals(s) for s in s1]
    b_h = [top_vals(s) for s in s2]
    a = [jnp.concatenate([a_h[h][i] for h in range(nh)], axis=0) for i in range(K)]
    b = [jnp.concatenate([b_h[h][i] for h in range(nh)], axis=0) for i in range(K)]

    sums = {c: a[c[0]] + b[c[1]] for c in _CANDS}
    tau = None
    for c in _CANDS:
        n_ge = jnp.zeros_like(sums[c])
        for c2 in _CANDS:
            n_ge = n_ge + jnp.where(sums[c2] >= sums[c], 1.0, 0.0)
        tc = jnp.where(n_ge <= float(K), sums[c], jnp.inf)
        tau = tc if tau is None else jnp.minimum(tau, tc)
    ea = [jnp.exp(ai - a[0]) for ai in a]
    eb = [jnp.exp(bj - b[0]) for bj in b]
    zsum = jnp.zeros_like(tau)
    cnt = [jnp.zeros_like(tau) for _ in range(K)]
    for (i, j) in _CANDS:
        sel = sums[(i, j)] >= tau
        zsum = zsum + jnp.where(sel, ea[i] * eb[j], 0.0)
        cnt[i] = cnt[i] + jnp.where(sel, 1.0, 0.0)
    zinv = 1.0 / zsum
    for h in range(nh):
        row = slice(h, h + 1)
        cfull = jnp.zeros_like(s1[h])
        for i in range(K):
            cfull = jnp.where(s1[h] == a_h[h][i], cnt[i][row, :], cfull)
        cnt_ref[h] = cfull
        e1_ref[h] = jnp.exp(s1[h] - a_h[h][0])
        rank = jnp.zeros_like(s2[h])
        for j in range(K):
            rank = rank + jnp.where(s2[h] < b_h[h][j], 1.0, 0.0)
        rk_ref[h] = rank.astype(BF16)
        e2_ref[h] = (jnp.exp(s2[h] - b_h[h][0]) * zinv[row, :]).astype(BF16)


def _peer_route(q, sub_keys, tm):
    ntok = q.shape[0]
    nk, nh = PEER_KEYS, PEER_HEADS
    k1 = sub_keys[:, 0].astype(BF16)
    k2 = sub_keys[:, 1].astype(BF16)
    blk = pl.BlockSpec((nh, nk, tm), lambda i: (0, 0, i))
    f32_out = jax.ShapeDtypeStruct((nh, nk, ntok), F32)
    bf16_out = jax.ShapeDtypeStruct((nh, nk, ntok), BF16)
    return pl.pallas_call(
        _route_kernel,
        grid=(ntok // tm,),
        in_specs=[pl.BlockSpec((tm, q.shape[1]), lambda i: (i, 0)),
                  pl.BlockSpec(k1.shape, lambda i: (0, 0, 0)),
                  pl.BlockSpec(k2.shape, lambda i: (0, 0, 0))],
        out_specs=[blk, blk, blk, blk],
        out_shape=[f32_out, f32_out, bf16_out, bf16_out],
        compiler_params=_cparams(1),
        name="peer_route",
    )(q, k1, k2)


def _dense_kernel(n2_ref, u_ref, vt_ref, cnt_ref, e1_ref, rk_ref, e2_ref, o_ref, w_scr):
    j = pl.program_id(1)
    act = _dot_nt(u_ref[...], n2_ref[...])
    nk = PEER_KEYS
    n_slab = act.shape[0] // nk
    tm = act.shape[1]
    zero = jnp.zeros((), BF16)
    for sl in range(n_slab):
        a = act[sl * nk:(sl + 1) * nk, :]
        gsum = jnp.zeros((nk, tm), BF16)
        for h in range(PEER_HEADS):
            cb = jnp.broadcast_to(cnt_ref[h, sl:sl + 1, :], (nk, tm)).astype(BF16)
            eb = jnp.broadcast_to(e1_ref[h, sl:sl + 1, :], (nk, tm)).astype(BF16)
            gsum = gsum + jnp.where(rk_ref[h] < cb, e2_ref[h], zero) * eb
        gelu = 0.5 * a * (1.0 + lax.erf(a * (2.0 ** -0.5)))
        w_scr[sl * nk:(sl + 1) * nk, :] = gsum * gelu.astype(BF16)
    contrib = _dot(vt_ref[...], w_scr[...])

    @pl.when(j == 0)
    def _():
        o_ref[...] = contrib

    @pl.when(j > 0)
    def _():
        o_ref[...] += contrib


def _peer_dense(n2, u_bf, vt_bf, cnt, e1, rk, e2, tm, ec):
    ntok, d = n2.shape
    n_exp = u_bf.shape[0]
    ns = ec // PEER_KEYS
    rows = pl.BlockSpec((PEER_HEADS, ns, tm), lambda i, j: (0, j, i))
    slab = pl.BlockSpec((PEER_HEADS, PEER_KEYS, tm), lambda i, j: (0, 0, i))
    return pl.pallas_call(
        _dense_kernel,
        grid=(ntok // tm, n_exp // ec),
        in_specs=[pl.BlockSpec((tm, d), lambda i, j: (i, 0)),
                  pl.BlockSpec((ec, d), lambda i, j: (j, 0)),
                  pl.BlockSpec((d, ec), lambda i, j: (0, j)),
                  rows, rows, slab, slab],
        out_specs=pl.BlockSpec((d, tm), lambda i, j: (0, i)),
        out_shape=jax.ShapeDtypeStruct((d, ntok), F32),
        scratch_shapes=[pltpu.VMEM((ec, tm), BF16)],
        compiler_params=_cparams(2),
        name="peer_dense",
    )(n2, u_bf, vt_bf, cnt, e1, rk, e2)


def _final_kernel(h1_ref, pt_ref, g2_ref, nw_ref, o_ref):
    h = h1_ref[0] + g2_ref[0] * pt_ref[...].T
    ms = jnp.mean(h * h, axis=-1, keepdims=True)
    o_ref[0] = (h * lax.rsqrt(ms + RMS_EPS) * nw_ref[...]).astype(o_ref.dtype)


def _final(h1, peer_t, gate2, norm_w, out_dtype, tm):
    bsz, t, d = h1.shape
    nt = t // tm
    return pl.pallas_call(
        _final_kernel,
        grid=(bsz, nt),
        in_specs=[pl.BlockSpec((1, tm, d), lambda b, i: (b, i, 0)),
                  pl.BlockSpec((d, tm), lambda b, i: (0, b * nt + i)),
                  pl.BlockSpec((1, 1, d), lambda b, i: (b, 0, 0)),
                  pl.BlockSpec((1, d), lambda b, i: (0, 0))],
        out_specs=pl.BlockSpec((1, tm, d), lambda b, i: (b, i, 0)),
        out_shape=jax.ShapeDtypeStruct((bsz, t, d), out_dtype),
        compiler_params=_cparams(2),
        name="final_norm",
    )(h1, peer_t, gate2.reshape(bsz, 1, d), norm_w.reshape(1, d))


def _pack_in_proj(w_in, mu):
    d = w_in.shape[0]
    rp = 3 * RW + 64 + 64 + 160
    zc = lambda n: jnp.zeros((d, n), w_in.dtype)
    z0 = rp
    dt0 = rp + SSM_W + SSM_W + 512
    w = jnp.concatenate([
        w_in[:, :3 * RW],
        w_in[:, z0:z0 + SSM_W],
        w_in[:, z0 + SSM_W:z0 + 2 * SSM_W],
        w_in[:, z0 + 2 * SSM_W:z0 + 2 * SSM_W + 512],
        w_in[:, 3 * RW:rp], zc(LORA_DT - 288),
        w_in[:, dt0:dt0 + SSM_HEADS], zc(LORA_W - LORA_DT - SSM_HEADS)], axis=1)
    m = jnp.concatenate([mu[:3 * RW], jnp.zeros((P_LORA - 3 * RW,), mu.dtype), mu[3 * RW:rp],
                         jnp.zeros((LORA_W - 288,), mu.dtype)])
    return w.astype(BF16), m.reshape(1, P_TOT)


def _tile(t, pref):
    return pref if t % pref == 0 else t


def kernel(x, c, ada_w, ada_b, norm1_w, w_in, rwkv_mu, rwkv_w0, rwkv_w_up, rwkv_a0, rwkv_a_up, rwkv_g_up, rwkv_k_k, rwkv_k_a, rwkv_r_k, rwkv_ln_w, rwkv_ln_b, mamba_conv_w, mamba_conv_b, mamba_dt_bias, mamba_a_log, mamba_d, mamba_norm_w, w_out, norm2_w, peer_w_query, peer_sub_keys, peer_u, peer_v, final_norm_w):
    out_dtype = x.dtype
    bsz, t, d = x.shape
    h = x.astype(F32)
    depth = ada_w.shape[0]
    for i in range(depth):
        mod = _ada_mod(c.astype(F32), ada_w[i], ada_b[i])
        shift1, scale1, gate1, shift2, scale2, gate2 = jnp.split(mod, 6, axis=-1)
        w_packed, mu_packed = _pack_in_proj(w_in[i], rwkv_mu[i])
        proj = _in_proj(h, norm1_w[i], shift1, scale1, w_packed, mu_packed, _tile(t, 512))
        o_rw = _rwkv(proj, rwkv_w0[i], rwkv_a0[i], rwkv_k_k[i], rwkv_k_a[i], rwkv_r_k[i].reshape(-1),
                     rwkv_ln_w[i], rwkv_ln_b[i], rwkv_w_up[i], rwkv_a_up[i], rwkv_g_up[i],
                     _tile(t, 512))
        o_ss = _ssd(proj, mamba_conv_w[i], mamba_conv_b[i], mamba_dt_bias[i], mamba_a_log[i],
                    mamba_d[i], mamba_norm_w[i])
        wq = peer_w_query[i].reshape(d, PEER_HEADS, 2, PEER_HALF).transpose(0, 2, 1, 3)
        wq = wq.reshape(d, 2 * PEER_HEADS * PEER_HALF).astype(BF16)
        h1, n2, q = _out_proj(o_rw, o_ss, h, gate1, norm2_w[i], shift2, scale2, w_out[i], wq,
                              _tile(t, 256))
        ntok = bsz * t
        th, e1, s2, e2 = _peer_route(q.reshape(ntok, -1), peer_sub_keys[i], _tile(ntok, 256))
        peer_t = _peer_dense(n2.reshape(ntok, d), peer_u[i].astype(BF16), peer_v[i].T.astype(BF16),
                             th, e1, s2, e2, _tile(ntok, 512), 1024)
        if i + 1 < depth:
            h = h1 + gate2[:, None, :] * peer_t.T.reshape(bsz, t, d)
        else:
            return _final(h1, peer_t, gate2, final_norm_w, out_dtype, _tile(t, 512))
```

```python
import functools

import jax
import jax.numpy as jnp
from jax import lax
from jax.experimental import pallas as pl
from jax.experimental.pallas import tpu as pltpu

F32 = jnp.float32
BF16 = jnp.bfloat16

D_MODEL = 2048
RW = 1024
RW_HEAD = 64
SSM_W = 1024
SSM_HEAD = 64
SSM_HEADS = 16
SSM_STATE = 128
SSM_CONV = 4
PEER_HEADS = 8
PEER_KEYS = 128
PEER_HALF = 128
PEER_TOPK = 16
RMS_EPS = 1e-6
GN_EPS = 64e-5
SSM_EPS = 1e-5

P_R, P_K, P_V, P_Z, P_XS, P_BC, P_LORA = 0, 1024, 2048, 3072, 4096, 5120, 5632
P_TOT = 6144
LORA_W = 512
LORA_DT = 384

RWKV_CHUNK = 64
SSD_CHUNK = 128
LANES = 128

VMEM_LIMIT = 56 * 1024 * 1024


def _cparams(n_axes):
    return pltpu.CompilerParams(
        dimension_semantics=("arbitrary",) * n_axes, vmem_limit_bytes=VMEM_LIMIT)


def _split3(x):
    h1 = x.astype(BF16)
    r1 = x - h1.astype(F32)
    h2 = r1.astype(BF16)
    r2 = r1 - h2.astype(F32)
    return h1, h2, r2.astype(BF16)


def _mm(a, b, dims):
    return lax.dot_general(a.astype(BF16), b.astype(BF16), (dims, ((), ())),
                           preferred_element_type=F32)


def _dot(a, b):
    return _mm(a, b, ((1,), (0,)))


def _dot_nt(a, b):
    return _mm(a, b, ((1,), (1,)))


def _dot_tn(a, b):
    return _mm(a, b, ((0,), (0,)))


def _dot_xe(x, e_bf16):
    h1, h2, h3 = _split3(x)
    return _dot(h1, e_bf16) + _dot(h2, e_bf16) + _dot(h3, e_bf16)


def _dot_ex(e_bf16, x):
    h1, h2, h3 = _split3(x)
    return _dot(e_bf16, h1) + _dot(e_bf16, h2) + _dot(e_bf16, h3)


def _dot_hi(a, b):
    ah = a.astype(BF16)
    al = (a - ah.astype(F32)).astype(BF16)
    bh = b.astype(BF16)
    bl = (b - bh.astype(F32)).astype(BF16)
    return _dot(ah, bh) + _dot(al, bh) + _dot(ah, bl)


def _silu(x):
    return x * jax.nn.sigmoid(x)


def _ada_kernel(cb_ref, w_ref, b_ref, o_ref):
    w = w_ref[...]
    reps = w.shape[1] // LANES
    rows = []
    for b in range(cb_ref.shape[0]):
        cb = _silu(cb_ref[b])
        rows.append(jnp.sum(w * pltpu.repeat(cb, reps, axis=1), axis=0, keepdims=True))
    o_ref[...] = jnp.concatenate(rows, axis=0) + b_ref[...]


def _ada_mod(c, ada_w, ada_b):
    bsz, d = c.shape
    n = ada_w.shape[1]
    tn = 1024
    cb = jnp.broadcast_to(c[:, :, None], (bsz, d, LANES))
    return pl.pallas_call(
        _ada_kernel,
        grid=(n // tn,),
        in_specs=[pl.BlockSpec((bsz, d, LANES), lambda j: (0, 0, 0)),
                  pl.BlockSpec((d, tn), lambda j: (0, j)),
                  pl.BlockSpec((1, tn), lambda j: (0, j))],
        out_specs=pl.BlockSpec((bsz, tn), lambda j: (0, j)),
        out_shape=jax.ShapeDtypeStruct((bsz, n), F32),
        compiler_params=_cparams(1),
        name="ada_mod",
    )(cb, ada_w, ada_b.reshape(1, n))


def _rms_mod(x, w, shift, scale):
    ms = jnp.mean(x * x, axis=-1, keepdims=True)
    return (x * lax.rsqrt(ms + RMS_EPS) * w) * (1.0 + scale) + shift


def _inproj_kernel(x_ref, nw_ref, sh_ref, sc_ref, w_ref, mu_ref, o_ref, n_scr, carry_scr):
    i = pl.program_id(1)
    j = pl.program_id(2)

    @pl.when(j == 0)
    def _():
        n_scr[...] = _rms_mod(x_ref[0], nw_ref[...], sh_ref[0], sc_ref[0]).astype(BF16)

    n = n_scr[...]
    tm = n.shape[0]
    pw = 256
    first = i == 0
    for c in range(w_ref.shape[1] // pw):
        cs = slice(c * pw, (c + 1) * pw)
        p = _dot(n, w_ref[:, cs])
        prev_last = jnp.where(first, 0.0, carry_scr[j, 7:8, cs])
        row = lax.broadcasted_iota(jnp.int32, p.shape, 0)
        shifted = jnp.where(row == 0, prev_last, pltpu.roll(p, 1, 0))
        carry_scr[j, :, cs] = p[tm - 8:tm, :]
        o_ref[0, :, cs] = p + (shifted - p) * mu_ref[:, cs]


def _in_proj(x, norm_w, shift, scale, w_packed, mu_packed, tm):
    bsz, t, d = x.shape
    tn = 1536
    nb = P_TOT // tn
    return pl.pallas_call(
        _inproj_kernel,
        grid=(bsz, t // tm, nb),
        in_specs=[pl.BlockSpec((1, tm, d), lambda b, i, j: (b, i, 0)),
                  pl.BlockSpec((1, d), lambda b, i, j: (0, 0)),
                  pl.BlockSpec((1, 1, d), lambda b, i, j: (b, 0, 0)),
                  pl.BlockSpec((1, 1, d), lambda b, i, j: (b, 0, 0)),
                  pl.BlockSpec((d, tn), lambda b, i, j: (0, j)),
                  pl.BlockSpec((1, tn), lambda b, i, j: (0, j))],
        out_specs=pl.BlockSpec((1, tm, tn), lambda b, i, j: (b, i, j)),
        out_shape=jax.ShapeDtypeStruct((bsz, t, P_TOT), F32),
        scratch_shapes=[pltpu.VMEM((tm, d), BF16), pltpu.VMEM((nb, 8, tn), F32)],
        compiler_params=_cparams(3),
        name="in_proj",
    )(x, norm_w.reshape(1, d), shift.reshape(bsz, 1, d), scale.reshape(bsz, 1, d),
      w_packed, mu_packed)


RWKV_PAIRS = 8


def _rwkv_kernel(r_ref, k_ref, v_ref, lora_ref, w0_ref, a0_ref, kk_ref, ka_ref, rk_ref,
                 lnw_ref, lnb_ref, wup_ref, aup_ref, gup_ref, o_ref,
                 s_scr, kkn_scr, k2_scr, beta_scr, logw_scr, cum_scr, o_scr):
    C = RWKV_CHUNK
    tc = r_ref.shape[1]
    n_chunks = tc // C
    n_pairs = r_ref.shape[2] // LANES

    @pl.when(pl.program_id(2) == 0)
    def _():
        s_scr[...] = jnp.zeros_like(s_scr)

    lane = lax.broadcasted_iota(jnp.int32, (1, LANES), 1)
    first = lane < RW_HEAD
    m0 = first.astype(F32)
    m1 = 1.0 - m0
    row = lax.broadcasted_iota(jnp.int32, (LANES, LANES), 0)
    col = lax.broadcasted_iota(jnp.int32, (LANES, LANES), 1)
    strict = row > col
    incl = row >= col
    blk16 = (row // 16) == (col // 16)
    eye = jnp.where(row == col, 1.0, 0.0)
    r64 = lax.broadcasted_iota(jnp.int32, (C, C), 0)
    c64 = lax.broadcasted_iota(jnp.int32, (C, C), 1)
    tri64 = jnp.where(r64 >= c64, 1.0, 0.0).astype(BF16)

    def per_head(x):
        cols = []
        for p in range(n_pairs):
            xp = x[:, p * LANES:(p + 1) * LANES]
            s0 = jnp.sum(xp * m0, axis=-1, keepdims=True)
            s1 = jnp.sum(xp * m1, axis=-1, keepdims=True)
            cols.append(jnp.where(first, s0, s1))
        return jnp.concatenate(cols, axis=1)

    lora = lora_ref[0]
    wa = lora[:, 0:128]
    w_lin = w0_ref[...] + _dot_hi(jnp.tanh(wa), wup_ref[...])
    logw = -jnp.exp(-jax.nn.softplus(-w_lin) - 0.5)
    a = jax.nn.sigmoid(a0_ref[...] + _dot_hi(wa, aup_ref[...]))
    k = k_ref[0]
    kk = k * kk_ref[...]
    kkn = kk / jnp.maximum(jnp.sqrt(per_head(kk * kk)), 1e-12)
    kkn_scr[...] = kkn
    k2_scr[...] = k * (1.0 + (a - 1.0) * ka_ref[...])
    beta_scr[...] = kkn * a
    logw_scr[...] = logw
    for c in range(n_chunks):
        cum_scr[c * C:(c + 1) * C, :] = _dot_ex(tri64, logw[c * C:(c + 1) * C, :])

    def stack(x):
        return jnp.concatenate([x * m0, x * m1], axis=0)

    def chunk(ci, carry):
        sl = pl.ds(pl.multiple_of(ci * C, C), C)
        P = range(n_pairs)
        lns = [slice(p * LANES, (p + 1) * LANES) for p in P]
        cum = [cum_scr[sl, ln] for ln in lns]
        cum_last = [c[C - 1:C, :] for c in cum]
        ig = [jnp.exp(-c) for c in cum]
        d_end = [jnp.exp(cl - c) for cl, c in zip(cum_last, cum)]
        k2 = [k2_scr[sl, ln] for ln in lns]
        beta = [beta_scr[sl, ln] for ln in lns]
        v_s = [stack(v_ref[0, sl, ln]) for ln in lns]
        rt_s = [stack(r_ref[0, sl, ln] * jnp.exp(c)) for ln, c in zip(lns, cum)]
        kap_s = [stack(kkn_scr[sl, ln] * jnp.exp(c - logw_scr[sl, ln])) for ln, c in zip(lns, cum)]
        bt_s = [stack(b * g) for b, g in zip(beta, ig)]
        kt_s = [stack(k * g) for k, g in zip(k2, ig)]
        upd = [jnp.concatenate([stack(k * d), -stack(b * d)], axis=0)
               for k, b, d in zip(k2, beta, d_end)]
        lhs = [jnp.concatenate([a, b], axis=0) for a, b in zip(kap_s, rt_s)]
        gb = [_dot_nt(l, b) for l, b in zip(lhs, bt_s)]
        gk = [_dot_nt(l, k) for l, k in zip(lhs, kt_s)]
        lb = [jnp.where(strict, g[:2 * C], 0.0) for g in gb]
        lk = [jnp.where(strict, g[:2 * C], 0.0) for g in gk]
        arb = [jnp.where(incl, g[2 * C:], 0.0) for g in gb]
        ark = [jnp.where(incl, g[2 * C:], 0.0) for g in gk]

        ld = [jnp.where(blk16, x, 0.0) for x in lb]
        off = [x - d for x, d in zip(lb, ld)]
        m = [-d for d in ld]
        m2 = [_dot(a, a) for a in m]
        m3 = [_dot(a, b) for a, b in zip(m, m2)]
        m4 = [_dot(a, a) for a in m2]
        m8 = [_dot(a, a) for a in m4]
        m12 = [_dot(a, b) for a, b in zip(m4, m8)]
        dinv = [_dot(eye + a + b + c, eye + d + e + f)
                for a, b, c, d, e, f in zip(m, m2, m3, m4, m8, m12)]
        n1 = [_dot(d, o) for d, o in zip(dinv, off)]
        n2 = [_dot(a, a) for a in n1]
        x1 = [d + _dot(a, d) for d, a in zip(dinv, n2)]
        tinv = [x - _dot(a, x) for x, a in zip(x1, n1)]

        s = [s_scr[p] for p in P]
        ks = [_dot_nt(l, st) for l, st in zip(lhs, s)]
        lkv = [_dot(a, b) for a, b in zip(lk, v_s)]
        y = [_dot(t, k[:2 * C] + x) for t, k, x in zip(tinv, ks, lkv)]
        sn = [_dot_tn(jnp.concatenate([vs, yy], axis=0), u) for vs, yy, u in zip(v_s, y, upd)]
        for p in P:
            s_scr[p] = s[p] * jnp.exp(cum_last[p]) + sn[p]
        arkv = [_dot(a, b) for a, b in zip(ark, v_s)]
        arby = [_dot(a, b) for a, b in zip(arb, y)]
        for p in P:
            o_s = ks[p][2 * C:] + arkv[p] - arby[p]
            o_scr[sl, lns[p]] = o_s[:C] + o_s[C:]
        return carry

    lax.fori_loop(0, n_chunks, chunk, 0)

    o = o_scr[...]
    mean = per_head(o) * (1.0 / RW_HEAD)
    dlt = o - mean
    var = per_head(dlt * dlt) * (1.0 / RW_HEAD)
    on = dlt * lax.rsqrt(var + GN_EPS) * lnw_ref[...] + lnb_ref[...]
    bonus = per_head(r_ref[0] * k2_scr[...] * rk_ref[...]) * v_ref[0]
    g = _dot_hi(jax.nn.sigmoid(lora[:, 128:384]), gup_ref[...])
    o_ref[0] = ((on + bonus) * g).astype(o_ref.dtype)


def _rwkv(proj, w0, a0, k_k, k_a, r_k, ln_w, ln_b, w_up, a_up, g_up, tc):
    bsz, t, _ = proj.shape
    wl = RWKV_PAIRS * LANES
    row = lambda p: p.reshape(1, RW)
    wup_p = jnp.concatenate([w_up, jnp.zeros_like(a_up)], axis=0)
    aup_p = jnp.concatenate([jnp.zeros_like(w_up), a_up], axis=0)
    gup_p = jnp.concatenate([g_up, jnp.zeros((256 - g_up.shape[0], RW), F32)], axis=0)
    pspec = lambda off: pl.BlockSpec((1, tc, wl), lambda b, h, i: (b, i, off // wl + h))
    vec = pl.BlockSpec((1, wl), lambda b, h, i: (0, h))
    tile = pltpu.VMEM((tc, wl), F32)
    return pl.pallas_call(
        _rwkv_kernel,
        grid=(bsz, RW // wl, t // tc),
        in_specs=[pspec(P_R), pspec(P_K), pspec(P_V),
                  pl.BlockSpec((1, tc, LORA_W), lambda b, h, i: (b, i, P_LORA // LORA_W)),
                  vec, vec, vec, vec, vec, vec, vec,
                  pl.BlockSpec((128, wl), lambda b, h, i: (0, h)),
                  pl.BlockSpec((128, wl), lambda b, h, i: (0, h)),
                  pl.BlockSpec((256, wl), lambda b, h, i: (0, h))],
        out_specs=pl.BlockSpec((1, tc, wl), lambda b, h, i: (b, i, h)),
        out_shape=jax.ShapeDtypeStruct((bsz, t, RW), BF16),
        scratch_shapes=[pltpu.VMEM((RWKV_PAIRS, LANES, LANES), F32),
                        tile, tile, tile, tile, tile, tile],
        compiler_params=_cparams(3),
        name="rwkv",
    )(proj, proj, proj, proj, row(w0), row(a0), row(k_k), row(k_a), row(r_k), row(ln_w),
      row(ln_b), wup_p, aup_p, gup_p)


def _ssd_kernel(z_ref, x_ref, bc_ref, lora_ref, cwx_ref, cbx_ref, cwb_ref, cbb_ref, dtb_ref,
                alog_ref, dsk_ref, nw_ref, ex_ref, o_ref, extx, extb, st_scr):
    L = SSD_CHUNK

    @pl.when(pl.program_id(1) == 0)
    def _():
        extx[0:8, :] = jnp.zeros((8, extx.shape[1]), F32)
        extb[0:8, :] = jnp.zeros((8, extb.shape[1]), F32)
        st_scr[...] = jnp.zeros_like(st_scr)

    def conv(ext, cur, w_ref, b_ref):
        ext[8:8 + L, :] = cur
        acc = b_ref[...] + w_ref[0:1, :] * ext[5:5 + L, :]
        for j in range(1, SSM_CONV):
            acc = acc + w_ref[j:j + 1, :] * ext[5 + j:5 + j + L, :]
        ext[0:8, :] = ext[L:L + 8, :]
        return _silu(acc)

    xs = conv(extx, x_ref[0], cwx_ref, cbx_ref)
    bc = conv(extb, bc_ref[0], cwb_ref, cbb_ref)

    row = lax.broadcasted_iota(jnp.int32, (L, L), 0)
    col = lax.broadcasted_iota(jnp.int32, (L, L), 1)
    incl = row >= col
    tri = jnp.where(incl, 1.0, 0.0).astype(BF16)
    lane = lax.broadcasted_iota(jnp.int32, (1, LANES), 1)
    m_par = [(lane < SSM_HEAD).astype(F32), (lane >= SSM_HEAD).astype(F32)]

    ex = ex_ref[...]
    dts = jax.nn.softplus(lora_ref[0][:, LORA_DT:LORA_DT + LANES] + dtb_ref[...])
    a_slab = dts * (-jnp.exp(alog_ref[...]))
    acs_slab = _dot_ex(tri, a_slab)
    acs_t = acs_slab.T
    dtx = _dot_xe(dts, ex)
    acs_x = _dot_xe(acs_slab, ex)
    last = acs_x[L - 1:L, :]
    xdt = xs * dtx
    xdte = xdt * jnp.exp(last - acs_x)
    eacs = jnp.exp(acs_x)
    elast = jnp.exp(last)

    ys = []
    for j in range(SSM_HEADS // 2):
        g = j // (SSM_HEADS // 4)
        bm = bc[:, g * LANES:(g + 1) * LANES]
        cm = bc[:, (2 + g) * LANES:(3 + g) * LANES]
        pr = slice(j * LANES, (j + 1) * LANES)
        cb = _dot_nt(cm, bm)
        st = st_scr[j]
        y = eacs[:, pr] * _dot(cm, st)
        xp = xdt[:, pr]
        for par in range(2):
            h = 2 * j + par
            diff = acs_slab[:, h:h + 1] - acs_t[h:h + 1, :]
            mh = jnp.exp(jnp.where(incl, diff, -jnp.inf)) * cb
            y = y + _dot(mh, xp * m_par[par])
        st_scr[j] = st * elast[:, pr] + _dot_tn(bm, xdte[:, pr])
        ys.append(y)
    y = jnp.concatenate(ys, axis=1) + xs * dsk_ref[...]
    y = y * _silu(z_ref[0])
    half = SSM_W // 2
    outs = []
    for g in range(2):
        seg = y[:, g * half:(g + 1) * half]
        outs.append(seg * lax.rsqrt(jnp.mean(seg * seg, axis=-1, keepdims=True) + SSM_EPS))
    o_ref[0] = (jnp.concatenate(outs, axis=1) * nw_ref[...]).astype(o_ref.dtype)


def _ssd(proj, conv_w, conv_b, dt_bias, a_log, d_skip, norm_w):
    bsz, t, _ = proj.shape
    L = SSD_CHUNK
    pad_slab = lambda p: jnp.concatenate([p, jnp.zeros((LANES - SSM_HEADS,), F32)]).reshape(1, LANES)
    hh = jnp.arange(LANES)[:, None]
    ll = jnp.arange(SSM_W)[None, :]
    ex = (hh == ll // SSM_HEAD).astype(BF16)
    cwx, cwb = conv_w[:, :SSM_W], conv_w[:, SSM_W:]
    cbx, cbb = conv_b[:SSM_W].reshape(1, -1), conv_b[SSM_W:].reshape(1, -1)
    full = lambda a: pl.BlockSpec(a.shape, lambda b, i: (0,) * a.ndim)
    args = [cwx, cbx, cwb, cbb, pad_slab(dt_bias), pad_slab(a_log),
            jnp.repeat(d_skip, SSM_HEAD).reshape(1, SSM_W), norm_w.reshape(1, SSM_W), ex]
    return pl.pallas_call(
        _ssd_kernel,
        grid=(bsz, t // L),
        in_specs=[pl.BlockSpec((1, L, SSM_W), lambda b, i: (b, i, P_Z // SSM_W)),
                  pl.BlockSpec((1, L, SSM_W), lambda b, i: (b, i, P_XS // SSM_W)),
                  pl.BlockSpec((1, L, 512), lambda b, i: (b, i, P_BC // 512)),
                  pl.BlockSpec((1, L, LORA_W), lambda b, i: (b, i, P_LORA // LORA_W))]
                 + [full(a) for a in args],
        out_specs=pl.BlockSpec((1, L, SSM_W), lambda b, i: (b, i, 0)),
        out_shape=jax.ShapeDtypeStruct((bsz, t, SSM_W), BF16),
        scratch_shapes=[pltpu.VMEM((L + 8, SSM_W), F32), pltpu.VMEM((L + 8, 512), F32),
                        pltpu.VMEM((SSM_HEADS // 2, SSM_STATE, LANES), F32)],
        compiler_params=_cparams(2),
        name="ssd",
    )(proj, proj, proj, proj, *args)


def _outproj_kernel(orw_ref, oss_ref, x_ref, g1_ref, nw_ref, sh_ref, sc_ref, wo1_ref, wo2_ref,
                    wq_ref, h1_ref, n2_ref, q_ref):
    mix = _dot(orw_ref[0], wo1_ref[...]) + _dot(oss_ref[0], wo2_ref[...])
    h1 = x_ref[0] + g1_ref[0] * mix
    h1_ref[0] = h1
    n2 = _rms_mod(h1, nw_ref[...], sh_ref[0], sc_ref[0]).astype(BF16)
    n2_ref[0] = n2
    q_ref[0] = _dot(n2, wq_ref[...])


def _out_proj(o_rw, o_ss, x, gate1, norm_w, shift, scale, w_out, w_query_p, tm):
    bsz, t, d = x.shape
    wo1 = w_out[:RW].astype(BF16)
    wo2 = w_out[RW:].astype(BF16)
    tok = lambda w: pl.BlockSpec((1, tm, w), lambda b, i: (b, i, 0))
    per_b = pl.BlockSpec((1, 1, d), lambda b, i: (b, 0, 0))
    const = lambda a: pl.BlockSpec(a.shape, lambda b, i: (0,) * a.ndim, pipeline_mode=pl.Buffered(1))
    nw = norm_w.reshape(1, d)
    return pl.pallas_call(
        _outproj_kernel,
        grid=(bsz, t // tm),
        in_specs=[tok(RW), tok(SSM_W), tok(d), per_b, const(nw), per_b, per_b,
                  const(wo1), const(wo2), const(w_query_p)],
        out_specs=[tok(d), tok(d), tok(d)],
        out_shape=[jax.ShapeDtypeStruct((bsz, t, d), F32), jax.ShapeDtypeStruct((bsz, t, d), BF16),
                   jax.ShapeDtypeStruct((bsz, t, d), F32)],
        compiler_params=_cparams(2),
        name="out_proj",
    )(o_rw, o_ss, x, gate1.reshape(bsz, 1, d), nw, shift.reshape(bsz, 1, d),
      scale.reshape(bsz, 1, d), wo1, wo2, w_query_p)


_CANDS = [(i, j) for i in range(PEER_TOPK) for j in range(PEER_TOPK) if (i + 1) * (j + 1) <= PEER_TOPK]


def _route_kernel(q_ref, k1_ref, k2_ref, cnt_ref, e1_ref, rk_ref, e2_ref):
    nh, hd, K = PEER_HEADS, PEER_HALF, PEER_TOPK
    q = q_ref[...]

    def top_vals(s, want_rank):
        vals, prev = [], None
        rank = jnp.zeros_like(s) if want_rank else None
        for _ in range(K):
            if prev is None:
                cand = s
            else:
                below = s < prev
                cand = jnp.where(below, s, -jnp.inf)
                if want_rank:
                    rank = jnp.where(below, rank + 1.0, rank)
            prev = jnp.max(cand, axis=0, keepdims=True)
            vals.append(prev)
        if want_rank:
            rank = jnp.where(s < prev, rank + 1.0, rank)
        return vals, rank

    s1 = [_dot_nt(k1_ref[h], q[:, h * hd:(h + 1) * hd]) for h in range(nh)]
    s2 = [_dot_nt(k2_ref[h], q[:, (nh + h) * hd:(nh + h + 1) * hd]) for h in range(nh)]
    a_h = [top_vals(s, False)[0] for s in s1]
    b_hr = [top_vals(s, True) for s in s2]
    b_h = [x[0] for x in b_hr]
    a = [jnp.concatenate([a_h[h][i] for h in range(nh)], axis=0) for i in range(K)]
    b = [jnp.concatenate([b_h[h][i] for h in range(nh)], axis=0) for i in range(K)]

    sums = {c: a[c[0]] + b[c[1]] for c in _CANDS}
    tau = None
    for _ in range(K):
        best = None
        for c in _CANDS:
            v = sums[c] if tau is None else jnp.where(sums[c] < tau, sums[c], -jnp.inf)
            best = v if best is None else jnp.maximum(best, v)
        tau = best
    ea = [jnp.exp(ai - a[0]) for ai in a]
    eb = [jnp.exp(bj - b[0]) for bj in b]
    zsum = jnp.zeros_like(tau)
    cnt = [jnp.zeros_like(tau) for _ in range(K)]
    for (i, j) in _CANDS:
        sel = sums[(i, j)] >= tau
        zsum = zsum + jnp.where(sel, ea[i] * eb[j], 0.0)
        cnt[i] = cnt[i] + jnp.where(sel, 1.0, 0.0)
    zinv = 1.0 / zsum
    for h in range(nh):
        row = slice(h, h + 1)
        cfull = jnp.zeros_like(s1[h])
        for i in range(K):
            cfull = jnp.where(s1[h] == a_h[h][i], cnt[i][row, :], cfull)
        cnt_ref[h] = cfull
        e1_ref[h] = jnp.exp(s1[h] - a_h[h][0])
        rk_ref[h] = b_hr[h][1].astype(BF16)
        e2_ref[h] = (jnp.exp(s2[h] - b_h[h][0]) * zinv[row, :]).astype(BF16)


def _peer_route(q, sub_keys, tm):
    ntok = q.shape[0]
    nk, nh = PEER_KEYS, PEER_HEADS
    k1 = sub_keys[:, 0].astype(BF16)
    k2 = sub_keys[:, 1].astype(BF16)
    blk = pl.BlockSpec((nh, nk, tm), lambda i: (0, 0, i))
    f32_out = jax.ShapeDtypeStruct((nh, nk, ntok), F32)
    bf16_out = jax.ShapeDtypeStruct((nh, nk, ntok), BF16)
    return pl.pallas_call(
        _route_kernel,
        grid=(ntok // tm,),
        in_specs=[pl.BlockSpec((tm, q.shape[1]), lambda i: (i, 0)),
                  pl.BlockSpec(k1.shape, lambda i: (0, 0, 0)),
                  pl.BlockSpec(k2.shape, lambda i: (0, 0, 0))],
        out_specs=[blk, blk, blk, blk],
        out_shape=[f32_out, f32_out, bf16_out, bf16_out],
        compiler_params=_cparams(1),
        name="peer_route",
    )(q, k1, k2)


def _dense_kernel(n2_ref, u_ref, vt_ref, cnt_ref, e1_ref, rk_ref, e2_ref, o_ref, w_scr):
    j = pl.program_id(1)
    nk = PEER_KEYS
    ec = u_ref.shape[0]
    tm = n2_ref.shape[0]
    n_piece = 4
    pc = ec // n_piece
    zero = jnp.zeros((), BF16)
    n2 = n2_ref[...]

    def weights(p, act):
        for s in range(pc // nk):
            sl = p * (pc // nk) + s
            gsum = jnp.zeros((nk, tm), BF16)
            for h in range(PEER_HEADS):
                cb = jnp.broadcast_to(cnt_ref[h, sl:sl + 1, :], (nk, tm)).astype(BF16)
                eb = jnp.broadcast_to(e1_ref[h, sl:sl + 1, :], (nk, tm)).astype(BF16)
                gsum = gsum + jnp.where(rk_ref[h] < cb, e2_ref[h], zero) * eb
            a = act[s * nk:(s + 1) * nk, :]
            gelu = 0.5 * a * (1.0 + lax.erf(a * (2.0 ** -0.5)))
            w_scr[sl * nk:(sl + 1) * nk, :] = gsum * gelu.astype(BF16)

    acts = [_dot_nt(u_ref[p * pc:(p + 1) * pc, :], n2) for p in range(n_piece)]
    for p in range(n_piece):
        weights(p, acts[p])
        part = _dot(vt_ref[:, p * pc:(p + 1) * pc], w_scr[p * pc:(p + 1) * pc, :])
        prev = jnp.where(j == 0, 0.0, o_ref[...]) if p == 0 else o_ref[...]
        o_ref[...] = prev + part


def _peer_dense(n2, u_bf, vt_bf, cnt, e1, rk, e2, tm, ec):
    ntok, d = n2.shape
    n_exp = u_bf.shape[0]
    ns = ec // PEER_KEYS
    rows = pl.BlockSpec((PEER_HEADS, ns, tm), lambda i, j: (0, j, i))
    slab = pl.BlockSpec((PEER_HEADS, PEER_KEYS, tm), lambda i, j: (0, 0, i))
    return pl.pallas_call(
        _dense_kernel,
        grid=(ntok // tm, n_exp // ec),
        in_specs=[pl.BlockSpec((tm, d), lambda i, j: (i, 0)),
                  pl.BlockSpec((ec, d), lambda i, j: (j, 0)),
                  pl.BlockSpec((d, ec), lambda i, j: (0, j)),
                  rows, rows, slab, slab],
        out_specs=pl.BlockSpec((d, tm), lambda i, j: (0, i)),
        out_shape=jax.ShapeDtypeStruct((d, ntok), F32),
        scratch_shapes=[pltpu.VMEM((ec, tm), BF16)],
        compiler_params=_cparams(2),
        name="peer_dense",
    )(n2, u_bf, vt_bf, cnt, e1, rk, e2)


def _final_kernel(h1_ref, pt_ref, g2_ref, nw_ref, o_ref):
    h = h1_ref[0] + g2_ref[0] * pt_ref[...].T
    ms = jnp.mean(h * h, axis=-1, keepdims=True)
    o_ref[0] = (h * lax.rsqrt(ms + RMS_EPS) * nw_ref[...]).astype(o_ref.dtype)


def _final(h1, peer_t, gate2, norm_w, out_dtype, tm):
    bsz, t, d = h1.shape
    nt = t // tm
    return pl.pallas_call(
        _final_kernel,
        grid=(bsz, nt),
        in_specs=[pl.BlockSpec((1, tm, d), lambda b, i: (b, i, 0)),
                  pl.BlockSpec((d, tm), lambda b, i: (0, b * nt + i)),
                  pl.BlockSpec((1, 1, d), lambda b, i: (b, 0, 0)),
                  pl.BlockSpec((1, d), lambda b, i: (0, 0))],
        out_specs=pl.BlockSpec((1, tm, d), lambda b, i: (b, i, 0)),
        out_shape=jax.ShapeDtypeStruct((bsz, t, d), out_dtype),
        compiler_params=_cparams(2),
        name="final_norm",
    )(h1, peer_t, gate2.reshape(bsz, 1, d), norm_w.reshape(1, d))


def _pack_in_proj(w_in, mu):
    d = w_in.shape[0]
    rp = 3 * RW + 64 + 64 + 160
    zc = lambda n: jnp.zeros((d, n), w_in.dtype)
    z0 = rp
    dt0 = rp + SSM_W + SSM_W + 512
    w = jnp.concatenate([
        w_in[:, :3 * RW],
        w_in[:, z0:z0 + SSM_W],
        w_in[:, z0 + SSM_W:z0 + 2 * SSM_W],
        w_in[:, z0 + 2 * SSM_W:z0 + 2 * SSM_W + 512],
        w_in[:, 3 * RW:rp], zc(LORA_DT - 288),
        w_in[:, dt0:dt0 + SSM_HEADS], zc(LORA_W - LORA_DT - SSM_HEADS)], axis=1)
    m = jnp.concatenate([mu[:3 * RW], jnp.zeros((P_LORA - 3 * RW,), mu.dtype), mu[3 * RW:rp],
                         jnp.zeros((LORA_W - 288,), mu.dtype)])
    return w.astype(BF16), m.reshape(1, P_TOT)


def _tile(t, pref):
    return pref if t % pref == 0 else t


def kernel(x, c, ada_w, ada_b, norm1_w, w_in, rwkv_mu, rwkv_w0, rwkv_w_up, rwkv_a0, rwkv_a_up, rwkv_g_up, rwkv_k_k, rwkv_k_a, rwkv_r_k, rwkv_ln_w, rwkv_ln_b, mamba_conv_w, mamba_conv_b, mamba_dt_bias, mamba_a_log, mamba_d, mamba_norm_w, w_out, norm2_w, peer_w_query, peer_sub_keys, peer_u, peer_v, final_norm_w):
    out_dtype = x.dtype
    bsz, t, d = x.shape
    h = x.astype(F32)
    depth = ada_w.shape[0]
    for i in range(depth):
        mod = _ada_mod(c.astype(F32), ada_w[i], ada_b[i])
        shift1, scale1, gate1, shift2, scale2, gate2 = jnp.split(mod, 6, axis=-1)
        w_packed, mu_packed = _pack_in_proj(w_in[i], rwkv_mu[i])
        proj = _in_proj(h, norm1_w[i], shift1, scale1, w_packed, mu_packed, _tile(t, 512))
        o_rw = _rwkv(proj, rwkv_w0[i], rwkv_a0[i], rwkv_k_k[i], rwkv_k_a[i], rwkv_r_k[i].reshape(-1),
                     rwkv_ln_w[i], rwkv_ln_b[i], rwkv_w_up[i], rwkv_a_up[i], rwkv_g_up[i],
                     _tile(t, 512))
        o_ss = _ssd(proj, mamba_conv_w[i], mamba_conv_b[i], mamba_dt_bias[i], mamba_a_log[i],
                    mamba_d[i], mamba_norm_w[i])
        wq = peer_w_query[i].reshape(d, PEER_HEADS, 2, PEER_HALF).transpose(0, 2, 1, 3)
        wq = wq.reshape(d, 2 * PEER_HEADS * PEER_HALF).astype(BF16)
        h1, n2, q = _out_proj(o_rw, o_ss, h, gate1, norm2_w[i], shift2, scale2, w_out[i], wq,
                              _tile(t, 256))
        ntok = bsz * t
        th, e1, s2, e2 = _peer_route(q.reshape(ntok, -1), peer_sub_keys[i], _tile(ntok, 256))
        peer_t = _peer_dense(n2.reshape(ntok, d), peer_u[i].astype(BF16), peer_v[i].T.astype(BF16),
                             th, e1, s2, e2, _tile(ntok, 512), 1024)
        if i + 1 < depth:
            h = h1 + gate2[:, None, :] * peer_t.T.reshape(bsz, t, d)
        else:
            return _final(h1, peer_t, gate2, final_norm_w, out_dtype, _tile(t, 512))
```

```python
import functools

import jax
import jax.numpy as jnp
from jax import lax
from jax.experimental import pallas as pl
from jax.experimental.pallas import tpu as pltpu

F32 = jnp.float32
BF16 = jnp.bfloat16

D_MODEL = 2048
RW = 1024
RW_HEAD = 64
SSM_W = 1024
SSM_HEAD = 64
SSM_HEADS = 16
SSM_STATE = 128
SSM_CONV = 4
PEER_HEADS = 8
PEER_KEYS = 128
PEER_HALF = 128
PEER_TOPK = 16
RMS_EPS = 1e-6
GN_EPS = 64e-5
SSM_EPS = 1e-5

P_R, P_K, P_V, P_Z, P_XS, P_BC, P_LORA = 0, 1024, 2048, 3072, 4096, 5120, 5632
P_TOT = 6144
LORA_W = 512
LORA_DT = 384

RWKV_CHUNK = 64
SSD_CHUNK = 128
LANES = 128

VMEM_LIMIT = 56 * 1024 * 1024


def _cparams(n_axes):
    return pltpu.CompilerParams(
        dimension_semantics=("arbitrary",) * n_axes, vmem_limit_bytes=VMEM_LIMIT)


def _split3(x):
    h1 = x.astype(BF16)
    r1 = x - h1.astype(F32)
    h2 = r1.astype(BF16)
    r2 = r1 - h2.astype(F32)
    return h1, h2, r2.astype(BF16)


def _mm(a, b, dims):
    return lax.dot_general(a.astype(BF16), b.astype(BF16), (dims, ((), ())),
                           preferred_element_type=F32)


def _dot(a, b):
    return _mm(a, b, ((1,), (0,)))


def _dot_nt(a, b):
    return _mm(a, b, ((1,), (1,)))


def _dot_tn(a, b):
    return _mm(a, b, ((0,), (0,)))


def _dot_xe(x, e_bf16):
    h1, h2, h3 = _split3(x)
    return _dot(h1, e_bf16) + _dot(h2, e_bf16) + _dot(h3, e_bf16)


def _dot_ex(e_bf16, x):
    h1, h2, h3 = _split3(x)
    return _dot(e_bf16, h1) + _dot(e_bf16, h2) + _dot(e_bf16, h3)


def _dot_hi(a, b):
    ah = a.astype(BF16)
    al = (a - ah.astype(F32)).astype(BF16)
    bh = b.astype(BF16)
    bl = (b - bh.astype(F32)).astype(BF16)
    return _dot(ah, bh) + _dot(al, bh) + _dot(ah, bl)


def _silu(x):
    return x * jax.nn.sigmoid(x)


def _ada_kernel(cb_ref, w_ref, b_ref, o_ref):
    w = w_ref[...]
    reps = w.shape[1] // LANES
    rows = []
    for b in range(cb_ref.shape[0]):
        cb = _silu(cb_ref[b])
        rows.append(jnp.sum(w * pltpu.repeat(cb, reps, axis=1), axis=0, keepdims=True))
    o_ref[...] = jnp.concatenate(rows, axis=0) + b_ref[...]


def _ada_mod(c, ada_w, ada_b):
    bsz, d = c.shape
    n = ada_w.shape[1]
    tn = 1024
    cb = jnp.broadcast_to(c[:, :, None], (bsz, d, LANES))
    return pl.pallas_call(
        _ada_kernel,
        grid=(n // tn,),
        in_specs=[pl.BlockSpec((bsz, d, LANES), lambda j: (0, 0, 0)),
                  pl.BlockSpec((d, tn), lambda j: (0, j)),
                  pl.BlockSpec((1, tn), lambda j: (0, j))],
        out_specs=pl.BlockSpec((bsz, tn), lambda j: (0, j)),
        out_shape=jax.ShapeDtypeStruct((bsz, n), F32),
        compiler_params=_cparams(1),
        name="ada_mod",
    )(cb, ada_w, ada_b.reshape(1, n))


def _rms_mod(x, w, shift, scale):
    ms = jnp.mean(x * x, axis=-1, keepdims=True)
    return (x * lax.rsqrt(ms + RMS_EPS) * w) * (1.0 + scale) + shift


def _inproj_kernel(x_ref, nw_ref, sh_ref, sc_ref, w_ref, mu_ref, o_ref, n_scr, carry_scr):
    i = pl.program_id(1)
    j = pl.program_id(2)

    @pl.when(j == 0)
    def _():
        n_scr[...] = _rms_mod(x_ref[0], nw_ref[...], sh_ref[0], sc_ref[0]).astype(BF16)

    n = n_scr[...]
    tm = n.shape[0]
    pw = 256
    first = i == 0
    for c in range(w_ref.shape[1] // pw):
        cs = slice(c * pw, (c + 1) * pw)
        p = _dot(n, w_ref[:, cs])
        prev_last = jnp.where(first, 0.0, carry_scr[j, 7:8, cs])
        row = lax.broadcasted_iota(jnp.int32, p.shape, 0)
        shifted = jnp.where(row == 0, prev_last, pltpu.roll(p, 1, 0))
        carry_scr[j, :, cs] = p[tm - 8:tm, :]
        o_ref[0, :, cs] = p + (shifted - p) * mu_ref[:, cs]


def _in_proj(x, norm_w, shift, scale, w_packed, mu_packed, tm):
    bsz, t, d = x.shape
    tn = 1536
    nb = P_TOT // tn
    return pl.pallas_call(
        _inproj_kernel,
        grid=(bsz, t // tm, nb),
        in_specs=[pl.BlockSpec((1, tm, d), lambda b, i, j: (b, i, 0)),
                  pl.BlockSpec((1, d), lambda b, i, j: (0, 0)),
                  pl.BlockSpec((1, 1, d), lambda b, i, j: (b, 0, 0)),
                  pl.BlockSpec((1, 1, d), lambda b, i, j: (b, 0, 0)),
                  pl.BlockSpec((d, tn), lambda b, i, j: (0, j)),
                  pl.BlockSpec((1, tn), lambda b, i, j: (0, j))],
        out_specs=pl.BlockSpec((1, tm, tn), lambda b, i, j: (b, i, j)),
        out_shape=jax.ShapeDtypeStruct((bsz, t, P_TOT), F32),
        scratch_shapes=[pltpu.VMEM((tm, d), BF16), pltpu.VMEM((nb, 8, tn), F32)],
        compiler_params=_cparams(3),
        name="in_proj",
    )(x, norm_w.reshape(1, d), shift.reshape(bsz, 1, d), scale.reshape(bsz, 1, d),
      w_packed, mu_packed)


RWKV_PAIRS = 8
RWKV_A_CHUNKS = 2


def _rwkv_kernel(r_ref, k_ref, v_ref, lora_ref, w0_ref, a0_ref, kk_ref, ka_ref, rk_ref,
                 lnw_ref, lnb_ref, wup_ref, aup_ref, gup_ref, o_ref,
                 s_scr, kkn_scr, k2_scr, beta_scr, logw_scr, cum_scr, o_scr,
                 tinv_scr, arb_scr, vs_scr, lhs_scr, upd_scr, lkv_scr, arkv_scr):
    C = RWKV_CHUNK
    tc = r_ref.shape[1]
    n_chunks = tc // C
    n_pairs = r_ref.shape[2] // LANES

    @pl.when(pl.program_id(2) == 0)
    def _():
        s_scr[...] = jnp.zeros_like(s_scr)

    lane = lax.broadcasted_iota(jnp.int32, (1, LANES), 1)
    first = lane < RW_HEAD
    m0 = first.astype(F32)
    m1 = 1.0 - m0
    row = lax.broadcasted_iota(jnp.int32, (LANES, LANES), 0)
    col = lax.broadcasted_iota(jnp.int32, (LANES, LANES), 1)
    strict = row > col
    incl = row >= col
    blk16 = (row // 16) == (col // 16)
    eye = jnp.where(row == col, 1.0, 0.0)
    r64 = lax.broadcasted_iota(jnp.int32, (C, C), 0)
    c64 = lax.broadcasted_iota(jnp.int32, (C, C), 1)
    tri64 = jnp.where(r64 >= c64, 1.0, 0.0).astype(BF16)

    def per_head(x):
        cols = []
        for p in range(n_pairs):
            xp = x[:, p * LANES:(p + 1) * LANES]
            s0 = jnp.sum(xp * m0, axis=-1, keepdims=True)
            s1 = jnp.sum(xp * m1, axis=-1, keepdims=True)
            cols.append(jnp.where(first, s0, s1))
        return jnp.concatenate(cols, axis=1)

    lora = lora_ref[0]
    wa = lora[:, 0:128]
    w_lin = w0_ref[...] + _dot_hi(jnp.tanh(wa), wup_ref[...])
    logw = -jnp.exp(-jax.nn.softplus(-w_lin) - 0.5)
    a = jax.nn.sigmoid(a0_ref[...] + _dot_hi(wa, aup_ref[...]))
    k = k_ref[0]
    kk = k * kk_ref[...]
    kkn = kk / jnp.maximum(jnp.sqrt(per_head(kk * kk)), 1e-12)
    kkn_scr[...] = kkn
    k2_scr[...] = k * (1.0 + (a - 1.0) * ka_ref[...])
    beta_scr[...] = kkn * a
    logw_scr[...] = logw
    for c in range(n_chunks):
        cum_scr[c * C:(c + 1) * C, :] = _dot_ex(tri64, logw[c * C:(c + 1) * C, :])

    def stack(x):
        return jnp.concatenate([x * m0, x * m1], axis=0)

    lns = [slice(p * LANES, (p + 1) * LANES) for p in range(n_pairs)]

    def indep(it, carry):
        chains = [(it * RWKV_A_CHUNKS + u, p) for u in range(RWKV_A_CHUNKS) for p in range(n_pairs)]
        sls = [pl.ds(pl.multiple_of(c * C, C), C) for c, _ in chains]
        lnc = [lns[p] for _, p in chains]
        cum = [cum_scr[sl, ln] for sl, ln in zip(sls, lnc)]
        ig = [jnp.exp(-c) for c in cum]
        d_end = [jnp.exp(c[C - 1:C, :] - c) for c in cum]
        k2 = [k2_scr[sl, ln] for sl, ln in zip(sls, lnc)]
        beta = [beta_scr[sl, ln] for sl, ln in zip(sls, lnc)]
        v_s = [stack(v_ref[0, sl, ln]).astype(BF16) for sl, ln in zip(sls, lnc)]
        rt_s = [stack(r_ref[0, sl, ln] * jnp.exp(c)) for sl, ln, c in zip(sls, lnc, cum)]
        kap_s = [stack(kkn_scr[sl, ln] * jnp.exp(c - logw_scr[sl, ln]))
                 for sl, ln, c in zip(sls, lnc, cum)]
        bt_s = [stack(b * g) for b, g in zip(beta, ig)]
        kt_s = [stack(k * g) for k, g in zip(k2, ig)]
        lhs = [jnp.concatenate([a, b], axis=0).astype(BF16) for a, b in zip(kap_s, rt_s)]
        for (c, p), k, b, d, l, vs in zip(chains, k2, beta, d_end, lhs, v_s):
            upd_scr[c, p] = jnp.concatenate([stack(k * d), -stack(b * d)], axis=0).astype(BF16)
            lhs_scr[c, p] = l
            vs_scr[c, p] = vs
        gb = [_dot_nt(l, b) for l, b in zip(lhs, bt_s)]
        gk = [_dot_nt(l, k) for l, k in zip(lhs, kt_s)]
        lb = [jnp.where(strict, g[:2 * C], 0.0) for g in gb]
        lk = [jnp.where(strict, g[:2 * C], 0.0) for g in gk]
        ark = [jnp.where(incl, g[2 * C:], 0.0) for g in gk]
        for (c, p), g in zip(chains, gb):
            arb_scr[c, p] = jnp.where(incl, g[2 * C:], 0.0).astype(BF16)

        ld = [jnp.where(blk16, x, 0.0) for x in lb]
        off = [x - d for x, d in zip(lb, ld)]
        m = [-d for d in ld]
        m2 = [_dot(a, a) for a in m]
        m3 = [_dot(a, b) for a, b in zip(m, m2)]
        m4 = [_dot(a, a) for a in m2]
        lkv = [_dot(a, b) for a, b in zip(lk, v_s)]
        m8 = [_dot(a, a) for a in m4]
        arkv = [_dot(a, b) for a, b in zip(ark, v_s)]
        m12 = [_dot(a, b) for a, b in zip(m4, m8)]
        dinv = [_dot(eye + a + b + c, eye + d + e + f)
                for a, b, c, d, e, f in zip(m, m2, m3, m4, m8, m12)]
        n1 = [_dot(d, o) for d, o in zip(dinv, off)]
        n2 = [_dot(a, a) for a in n1]
        x1 = [d + _dot(a, d) for d, a in zip(dinv, n2)]
        for (c, p), x, a, lv, av in zip(chains, x1, n1, lkv, arkv):
            tinv_scr[c, p] = (x - _dot(a, x)).astype(BF16)
            lkv_scr[c, p] = lv
            arkv_scr[c, p] = av
        return carry

    lax.fori_loop(0, n_chunks // RWKV_A_CHUNKS, indep, 0)

    def recur(ci, carry):
        sl = pl.ds(pl.multiple_of(ci * C, C), C)
        P = range(n_pairs)
        s = [s_scr[p] for p in P]
        ks = [_dot_nt(lhs_scr[ci, p], s[p]) for p in P]
        y = [_dot(tinv_scr[ci, p], ks[p][:2 * C] + lkv_scr[ci, p]) for p in P]
        sn = [_dot_tn(jnp.concatenate([vs_scr[ci, p], y[p].astype(BF16)], axis=0), upd_scr[ci, p])
              for p in P]
        for p in P:
            s_scr[p] = s[p] * jnp.exp(cum_scr[sl, lns[p]][C - 1:C, :]) + sn[p]
        arby = [_dot(arb_scr[ci, p], y[p]) for p in P]
        for p in P:
            o_s = ks[p][2 * C:] + arkv_scr[ci, p] - arby[p]
            o_scr[sl, lns[p]] = o_s[:C] + o_s[C:]
        return carry

    lax.fori_loop(0, n_chunks, recur, 0)

    o = o_scr[...]
    mean = per_head(o) * (1.0 / RW_HEAD)
    dlt = o - mean
    var = per_head(dlt * dlt) * (1.0 / RW_HEAD)
    on = dlt * lax.rsqrt(var + GN_EPS) * lnw_ref[...] + lnb_ref[...]
    bonus = per_head(r_ref[0] * k2_scr[...] * rk_ref[...]) * v_ref[0]
    g = _dot_hi(jax.nn.sigmoid(lora[:, 128:384]), gup_ref[...])
    o_ref[0] = ((on + bonus) * g).astype(o_ref.dtype)


def _rwkv(proj, w0, a0, k_k, k_a, r_k, ln_w, ln_b, w_up, a_up, g_up, tc):
    bsz, t, _ = proj.shape
    wl = RWKV_PAIRS * LANES
    row = lambda p: p.reshape(1, RW)
    wup_p = jnp.concatenate([w_up, jnp.zeros_like(a_up)], axis=0)
    aup_p = jnp.concatenate([jnp.zeros_like(w_up), a_up], axis=0)
    gup_p = jnp.concatenate([g_up, jnp.zeros((256 - g_up.shape[0], RW), F32)], axis=0)
    pspec = lambda off: pl.BlockSpec((1, tc, wl), lambda b, h, i: (b, i, off // wl + h))
    vec = pl.BlockSpec((1, wl), lambda b, h, i: (0, h))
    tile = pltpu.VMEM((tc, wl), F32)
    per_chain = lambda rows, dt: pltpu.VMEM((tc // RWKV_CHUNK, RWKV_PAIRS, rows, LANES), dt)
    return pl.pallas_call(
        _rwkv_kernel,
        grid=(bsz, RW // wl, t // tc),
        in_specs=[pspec(P_R), pspec(P_K), pspec(P_V),
                  pl.BlockSpec((1, tc, LORA_W), lambda b, h, i: (b, i, P_LORA // LORA_W)),
                  vec, vec, vec, vec, vec, vec, vec,
                  pl.BlockSpec((128, wl), lambda b, h, i: (0, h)),
                  pl.BlockSpec((128, wl), lambda b, h, i: (0, h)),
                  pl.BlockSpec((256, wl), lambda b, h, i: (0, h))],
        out_specs=pl.BlockSpec((1, tc, wl), lambda b, h, i: (b, i, h)),
        out_shape=jax.ShapeDtypeStruct((bsz, t, RW), BF16),
        scratch_shapes=[pltpu.VMEM((RWKV_PAIRS, LANES, LANES), F32),
                        tile, tile, tile, tile, tile, tile,
                        per_chain(LANES, BF16), per_chain(LANES, BF16), per_chain(LANES, BF16),
                        per_chain(2 * LANES, BF16), per_chain(2 * LANES, BF16),
                        per_chain(LANES, F32), per_chain(LANES, F32)],
        compiler_params=_cparams(3),
        name="rwkv",
    )(proj, proj, proj, proj, row(w0), row(a0), row(k_k), row(k_a), row(r_k), row(ln_w),
      row(ln_b), wup_p, aup_p, gup_p)


def _ssd_kernel(z_ref, x_ref, bc_ref, lora_ref, cwx_ref, cbx_ref, cwb_ref, cbb_ref, dtb_ref,
                alog_ref, dsk_ref, nw_ref, ex_ref, o_ref, extx, extb, st_scr):
    L = SSD_CHUNK

    @pl.when(pl.program_id(1) == 0)
    def _():
        extx[0:8, :] = jnp.zeros((8, extx.shape[1]), F32)
        extb[0:8, :] = jnp.zeros((8, extb.shape[1]), F32)
        st_scr[...] = jnp.zeros_like(st_scr)

    def conv(ext, cur, w_ref, b_ref):
        ext[8:8 + L, :] = cur
        acc = b_ref[...] + w_ref[0:1, :] * ext[5:5 + L, :]
        for j in range(1, SSM_CONV):
            acc = acc + w_ref[j:j + 1, :] * ext[5 + j:5 + j + L, :]
        ext[0:8, :] = ext[L:L + 8, :]
        return _silu(acc)

    xs = conv(extx, x_ref[0], cwx_ref, cbx_ref)
    bc = conv(extb, bc_ref[0], cwb_ref, cbb_ref)

    row = lax.broadcasted_iota(jnp.int32, (L, L), 0)
    col = lax.broadcasted_iota(jnp.int32, (L, L), 1)
    incl = row >= col
    tri = jnp.where(incl, 1.0, 0.0).astype(BF16)
    lane = lax.broadcasted_iota(jnp.int32, (1, LANES), 1)
    m_par = [(lane < SSM_HEAD).astype(F32), (lane >= SSM_HEAD).astype(F32)]

    ex = ex_ref[...]
    dts = jax.nn.softplus(lora_ref[0][:, LORA_DT:LORA_DT + LANES] + dtb_ref[...])
    a_slab = dts * (-jnp.exp(alog_ref[...]))
    acs_slab = _dot_ex(tri, a_slab)
    acs_t = acs_slab.T
    dtx = _dot_xe(dts, ex)
    acs_x = _dot_xe(acs_slab, ex)
    last = acs_x[L - 1:L, :]
    xdt = xs * dtx
    xdte = xdt * jnp.exp(last - acs_x)
    eacs = jnp.exp(acs_x)
    elast = jnp.exp(last)

    ys = []
    for j in range(SSM_HEADS // 2):
        g = j // (SSM_HEADS // 4)
        bm = bc[:, g * LANES:(g + 1) * LANES]
        cm = bc[:, (2 + g) * LANES:(3 + g) * LANES]
        pr = slice(j * LANES, (j + 1) * LANES)
        cb = _dot_nt(cm, bm)
        st = st_scr[j]
        y = eacs[:, pr] * _dot(cm, st)
        xp = xdt[:, pr]
        for par in range(2):
            h = 2 * j + par
            diff = acs_slab[:, h:h + 1] - acs_t[h:h + 1, :]
            mh = jnp.exp(jnp.where(incl, diff, -jnp.inf)) * cb
            y = y + _dot(mh, xp * m_par[par])
        st_scr[j] = st * elast[:, pr] + _dot_tn(bm, xdte[:, pr])
        ys.append(y)
    y = jnp.concatenate(ys, axis=1) + xs * dsk_ref[...]
    y = y * _silu(z_ref[0])
    half = SSM_W // 2
    outs = []
    for g in range(2):
        seg = y[:, g * half:(g + 1) * half]
        outs.append(seg * lax.rsqrt(jnp.mean(seg * seg, axis=-1, keepdims=True) + SSM_EPS))
    o_ref[0] = (jnp.concatenate(outs, axis=1) * nw_ref[...]).astype(o_ref.dtype)


def _ssd(proj, conv_w, conv_b, dt_bias, a_log, d_skip, norm_w):
    bsz, t, _ = proj.shape
    L = SSD_CHUNK
    pad_slab = lambda p: jnp.concatenate([p, jnp.zeros((LANES - SSM_HEADS,), F32)]).reshape(1, LANES)
    hh = jnp.arange(LANES)[:, None]
    ll = jnp.arange(SSM_W)[None, :]
    ex = (hh == ll // SSM_HEAD).astype(BF16)
    cwx, cwb = conv_w[:, :SSM_W], conv_w[:, SSM_W:]
    cbx, cbb = conv_b[:SSM_W].reshape(1, -1), conv_b[SSM_W:].reshape(1, -1)
    full = lambda a: pl.BlockSpec(a.shape, lambda b, i: (0,) * a.ndim)
    args = [cwx, cbx, cwb, cbb, pad_slab(dt_bias), pad_slab(a_log),
            jnp.repeat(d_skip, SSM_HEAD).reshape(1, SSM_W), norm_w.reshape(1, SSM_W), ex]
    return pl.pallas_call(
        _ssd_kernel,
        grid=(bsz, t // L),
        in_specs=[pl.BlockSpec((1, L, SSM_W), lambda b, i: (b, i, P_Z // SSM_W)),
                  pl.BlockSpec((1, L, SSM_W), lambda b, i: (b, i, P_XS // SSM_W)),
                  pl.BlockSpec((1, L, 512), lambda b, i: (b, i, P_BC // 512)),
                  pl.BlockSpec((1, L, LORA_W), lambda b, i: (b, i, P_LORA // LORA_W))]
                 + [full(a) for a in args],
        out_specs=pl.BlockSpec((1, L, SSM_W), lambda b, i: (b, i, 0)),
        out_shape=jax.ShapeDtypeStruct((bsz, t, SSM_W), BF16),
        scratch_shapes=[pltpu.VMEM((L + 8, SSM_W), F32), pltpu.VMEM((L + 8, 512), F32),
                        pltpu.VMEM((SSM_HEADS // 2, SSM_STATE, LANES), F32)],
        compiler_params=_cparams(2),
        name="ssd",
    )(proj, proj, proj, proj, *args)


def _outproj_kernel(orw_ref, oss_ref, x_ref, g1_ref, nw_ref, sh_ref, sc_ref, wo1_ref, wo2_ref,
                    wq_ref, h1_ref, n2_ref, q_ref):
    mix = _dot(orw_ref[0], wo1_ref[...]) + _dot(oss_ref[0], wo2_ref[...])
    h1 = x_ref[0] + g1_ref[0] * mix
    h1_ref[0] = h1
    n2 = _rms_mod(h1, nw_ref[...], sh_ref[0], sc_ref[0]).astype(BF16)
    n2_ref[0] = n2
    q_ref[0] = _dot(n2, wq_ref[...])


def _out_proj(o_rw, o_ss, x, gate1, norm_w, shift, scale, w_out, w_query_p, tm):
    bsz, t, d = x.shape
    wo1 = w_out[:RW].astype(BF16)
    wo2 = w_out[RW:].astype(BF16)
    tok = lambda w: pl.BlockSpec((1, tm, w), lambda b, i: (b, i, 0))
    per_b = pl.BlockSpec((1, 1, d), lambda b, i: (b, 0, 0))
    const = lambda a: pl.BlockSpec(a.shape, lambda b, i: (0,) * a.ndim, pipeline_mode=pl.Buffered(1))
    nw = norm_w.reshape(1, d)
    return pl.pallas_call(
        _outproj_kernel,
        grid=(bsz, t // tm),
        in_specs=[tok(RW), tok(SSM_W), tok(d), per_b, const(nw), per_b, per_b,
                  const(wo1), const(wo2), const(w_query_p)],
        out_specs=[tok(d), tok(d), tok(d)],
        out_shape=[jax.ShapeDtypeStruct((bsz, t, d), F32), jax.ShapeDtypeStruct((bsz, t, d), BF16),
                   jax.ShapeDtypeStruct((bsz, t, d), F32)],
        compiler_params=_cparams(2),
        name="out_proj",
    )(o_rw, o_ss, x, gate1.reshape(bsz, 1, d), nw, shift.reshape(bsz, 1, d),
      scale.reshape(bsz, 1, d), wo1, wo2, w_query_p)


_CANDS = [(i, j) for i in range(PEER_TOPK) for j in range(PEER_TOPK) if (i + 1) * (j + 1) <= PEER_TOPK]


def _route_kernel(q_ref, k1_ref, k2_ref, cnt_ref, e1_ref, rk_ref, e2_ref):
    nh, hd, K = PEER_HEADS, PEER_HALF, PEER_TOPK
    q = q_ref[...]

    def top_vals(s, want_rank):
        vals, prev = [], None
        rank = jnp.zeros_like(s) if want_rank else None
        for _ in range(K):
            if prev is None:
                cand = s
            else:
                below = s < prev
                cand = jnp.where(below, s, -jnp.inf)
                if want_rank:
                    rank = jnp.where(below, rank + 1.0, rank)
            prev = jnp.max(cand, axis=0, keepdims=True)
            vals.append(prev)
        if want_rank:
            rank = jnp.where(s < prev, rank + 1.0, rank)
        return vals, rank

    s1 = [_dot_nt(k1_ref[h], q[:, h * hd:(h + 1) * hd]) for h in range(nh)]
    s2 = [_dot_nt(k2_ref[h], q[:, (nh + h) * hd:(nh + h + 1) * hd]) for h in range(nh)]
    a_h = [top_vals(s, False)[0] for s in s1]
    b_hr = [top_vals(s, True) for s in s2]
    b_h = [x[0] for x in b_hr]
    a = [jnp.concatenate([a_h[h][i] for h in range(nh)], axis=0) for i in range(K)]
    b = [jnp.concatenate([b_h[h][i] for h in range(nh)], axis=0) for i in range(K)]

    sums = {c: a[c[0]] + b[c[1]] for c in _CANDS}
    tau = None
    for _ in range(K):
        best = None
        for c in _CANDS:
            v = sums[c] if tau is None else jnp.where(sums[c] < tau, sums[c], -jnp.inf)
            best = v if best is None else jnp.maximum(best, v)
        tau = best
    ea = [jnp.exp(ai - a[0]) for ai in a]
    eb = [jnp.exp(bj - b[0]) for bj in b]
    zsum = jnp.zeros_like(tau)
    cnt = [jnp.zeros_like(tau) for _ in range(K)]
    for (i, j) in _CANDS:
        sel = sums[(i, j)] >= tau
        zsum = zsum + jnp.where(sel, ea[i] * eb[j], 0.0)
        cnt[i] = cnt[i] + jnp.where(sel, 1.0, 0.0)
    zinv = 1.0 / zsum
    for h in range(nh):
        row = slice(h, h + 1)
        cfull = jnp.zeros_like(s1[h])
        for i in range(K):
            cfull = jnp.where(s1[h] == a_h[h][i], cnt[i][row, :], cfull)
        cnt_ref[h] = cfull
        e1_ref[h] = jnp.exp(s1[h] - a_h[h][0])
        rk_ref[h] = b_hr[h][1].astype(BF16)
        e2_ref[h] = (jnp.exp(s2[h] - b_h[h][0]) * zinv[row, :]).astype(BF16)


def _peer_route(q, sub_keys, tm):
    ntok = q.shape[0]
    nk, nh = PEER_KEYS, PEER_HEADS
    k1 = sub_keys[:, 0].astype(BF16)
    k2 = sub_keys[:, 1].astype(BF16)
    blk = pl.BlockSpec((nh, nk, tm), lambda i: (0, 0, i))
    f32_out = jax.ShapeDtypeStruct((nh, nk, ntok), F32)
    bf16_out = jax.ShapeDtypeStruct((nh, nk, ntok), BF16)
    return pl.pallas_call(
        _route_kernel,
        grid=(ntok // tm,),
        in_specs=[pl.BlockSpec((tm, q.shape[1]), lambda i: (i, 0)),
                  pl.BlockSpec(k1.shape, lambda i: (0, 0, 0)),
                  pl.BlockSpec(k2.shape, lambda i: (0, 0, 0))],
        out_specs=[blk, blk, blk, blk],
        out_shape=[f32_out, f32_out, bf16_out, bf16_out],
        compiler_params=_cparams(1),
        name="peer_route",
    )(q, k1, k2)


def _dense_kernel(n2_ref, u_ref, vt_ref, cnt_ref, e1_ref, rk_ref, e2_ref, h1_ref, g2_ref, nw_ref,
                  o_ref, w_scr, acc_scr):
    j = pl.program_id(1)
    nk = PEER_KEYS
    ec = u_ref.shape[0]
    tm = n2_ref.shape[0]
    n_piece = 4
    pc = ec // n_piece
    zero = jnp.zeros((), BF16)
    n2 = n2_ref[...]

    def weights(p, act):
        for s in range(pc // nk):
            sl = p * (pc // nk) + s
            gsum = jnp.zeros((nk, tm), BF16)
            for h in range(PEER_HEADS):
                cb = jnp.broadcast_to(cnt_ref[h, sl:sl + 1, :], (nk, tm)).astype(BF16)
                eb = jnp.broadcast_to(e1_ref[h, sl:sl + 1, :], (nk, tm)).astype(BF16)
                gsum = gsum + jnp.where(rk_ref[h] < cb, e2_ref[h], zero) * eb
            a = act[s * nk:(s + 1) * nk, :]
            gelu = 0.5 * a * (1.0 + lax.erf(a * (2.0 ** -0.5)))
            w_scr[sl * nk:(sl + 1) * nk, :] = gsum * gelu.astype(BF16)

    acts = [_dot_nt(u_ref[p * pc:(p + 1) * pc, :], n2) for p in range(n_piece)]
    for p in range(n_piece):
        weights(p, acts[p])
        part = _dot(vt_ref[:, p * pc:(p + 1) * pc], w_scr[p * pc:(p + 1) * pc, :])
        prev = jnp.where(j == 0, 0.0, acc_scr[...]) if p == 0 else acc_scr[...]
        acc_scr[...] = prev + part

    @pl.when(j == pl.num_programs(1) - 1)
    def _():
        h = h1_ref[...] + g2_ref[0] * acc_scr[...].T
        ms = jnp.mean(h * h, axis=-1, keepdims=True)
        o_ref[...] = (h * lax.rsqrt(ms + RMS_EPS) * nw_ref[...]).astype(o_ref.dtype)


def _peer_dense_final(n2, u_bf, vt_bf, cnt, e1, rk, e2, h1, gate2, norm_w, out_dtype, tm, ec):
    ntok, d = n2.shape
    bsz = gate2.shape[0]
    tiles_per_batch = ntok // bsz // tm
    n_exp = u_bf.shape[0]
    ns = ec // PEER_KEYS
    rows = pl.BlockSpec((PEER_HEADS, ns, tm), lambda i, j: (0, j, i))
    slab = pl.BlockSpec((PEER_HEADS, PEER_KEYS, tm), lambda i, j: (0, 0, i))
    return pl.pallas_call(
        _dense_kernel,
        grid=(ntok // tm, n_exp // ec),
        in_specs=[pl.BlockSpec((tm, d), lambda i, j: (i, 0)),
                  pl.BlockSpec((ec, d), lambda i, j: (j, 0)),
                  pl.BlockSpec((d, ec), lambda i, j: (0, j)),
                  rows, rows, slab, slab,
                  pl.BlockSpec((tm, d), lambda i, j: (i, 0), pipeline_mode=pl.Buffered(1)),
                  pl.BlockSpec((1, 1, d), lambda i, j: (i // tiles_per_batch, 0, 0)),
                  pl.BlockSpec((1, d), lambda i, j: (0, 0))],
        out_specs=pl.BlockSpec((tm, d), lambda i, j: (i, 0)),
        out_shape=jax.ShapeDtypeStruct((ntok, d), out_dtype),
        scratch_shapes=[pltpu.VMEM((ec, tm), BF16), pltpu.VMEM((d, tm), F32)],
        compiler_params=_cparams(2),
        name="peer_dense",
    )(n2, u_bf, vt_bf, cnt, e1, rk, e2, h1, gate2.reshape(bsz, 1, d), norm_w.reshape(1, d))


def _pack_in_proj(w_in, mu):
    d = w_in.shape[0]
    rp = 3 * RW + 64 + 64 + 160
    zc = lambda n: jnp.zeros((d, n), w_in.dtype)
    z0 = rp
    dt0 = rp + SSM_W + SSM_W + 512
    w = jnp.concatenate([
        w_in[:, :3 * RW],
        w_in[:, z0:z0 + SSM_W],
        w_in[:, z0 + SSM_W:z0 + 2 * SSM_W],
        w_in[:, z0 + 2 * SSM_W:z0 + 2 * SSM_W + 512],
        w_in[:, 3 * RW:rp], zc(LORA_DT - 288),
        w_in[:, dt0:dt0 + SSM_HEADS], zc(LORA_W - LORA_DT - SSM_HEADS)], axis=1)
    m = jnp.concatenate([mu[:3 * RW], jnp.zeros((P_LORA - 3 * RW,), mu.dtype), mu[3 * RW:rp],
                         jnp.zeros((LORA_W - 288,), mu.dtype)])
    return w.astype(BF16), m.reshape(1, P_TOT)


def _tile(t, pref):
    return pref if t % pref == 0 else t


def kernel(x, c, ada_w, ada_b, norm1_w, w_in, rwkv_mu, rwkv_w0, rwkv_w_up, rwkv_a0, rwkv_a_up, rwkv_g_up, rwkv_k_k, rwkv_k_a, rwkv_r_k, rwkv_ln_w, rwkv_ln_b, mamba_conv_w, mamba_conv_b, mamba_dt_bias, mamba_a_log, mamba_d, mamba_norm_w, w_out, norm2_w, peer_w_query, peer_sub_keys, peer_u, peer_v, final_norm_w):
    out_dtype = x.dtype
    bsz, t, d = x.shape
    ntok = bsz * t
    assert ada_w.shape[0] == 1, "single-layer trunk (the closing RMSNorm is fused into the PEER kernel)"
    h = x.astype(F32)
    mod = _ada_mod(c.astype(F32), ada_w[0], ada_b[0])
    shift1, scale1, gate1, shift2, scale2, gate2 = jnp.split(mod, 6, axis=-1)
    w_packed, mu_packed = _pack_in_proj(w_in[0], rwkv_mu[0])
    proj = _in_proj(h, norm1_w[0], shift1, scale1, w_packed, mu_packed, _tile(t, 512))
    o_rw = _rwkv(proj, rwkv_w0[0], rwkv_a0[0], rwkv_k_k[0], rwkv_k_a[0], rwkv_r_k[0].reshape(-1),
                 rwkv_ln_w[0], rwkv_ln_b[0], rwkv_w_up[0], rwkv_a_up[0], rwkv_g_up[0],
                 _tile(t, 256))
    o_ss = _ssd(proj, mamba_conv_w[0], mamba_conv_b[0], mamba_dt_bias[0], mamba_a_log[0],
                mamba_d[0], mamba_norm_w[0])
    wq = peer_w_query[0].reshape(d, PEER_HEADS, 2, PEER_HALF).transpose(0, 2, 1, 3)
    wq = wq.reshape(d, 2 * PEER_HEADS * PEER_HALF).astype(BF16)
    h1, n2, q = _out_proj(o_rw, o_ss, h, gate1, norm2_w[0], shift2, scale2, w_out[0], wq,
                          _tile(t, 256))
    cnt, e1, rk, e2 = _peer_route(q.reshape(ntok, -1), peer_sub_keys[0], _tile(ntok, 256))
    out = _peer_dense_final(n2.reshape(ntok, d), peer_u[0].astype(BF16), peer_v[0].T.astype(BF16),
                            cnt, e1, rk, e2, h1.reshape(ntok, d), gate2, final_norm_w, out_dtype,
                            _tile(t, 512), 1024)
    return out.reshape(bsz, t, d)
```

```python
import functools

import jax
import jax.numpy as jnp
from jax import lax
from jax.experimental import pallas as pl
from jax.experimental.pallas import tpu as pltpu

F32 = jnp.float32
BF16 = jnp.bfloat16

D_MODEL = 2048
RW = 1024
RW_HEAD = 64
SSM_W = 1024
SSM_HEAD = 64
SSM_HEADS = 16
SSM_STATE = 128
SSM_CONV = 4
PEER_HEADS = 8
PEER_KEYS = 128
PEER_HALF = 128
PEER_TOPK = 16
RMS_EPS = 1e-6
GN_EPS = 64e-5
SSM_EPS = 1e-5

P_R, P_K, P_V, P_Z, P_XS, P_BC, P_LORA = 0, 1024, 2048, 3072, 4096, 5120, 5632
P_TOT = 6144
LORA_W = 512
LORA_DT = 384

RWKV_CHUNK = 64
SSD_CHUNK = 128
LANES = 128

VMEM_LIMIT = 56 * 1024 * 1024


def _cparams(n_axes):
    return pltpu.CompilerParams(
        dimension_semantics=("arbitrary",) * n_axes, vmem_limit_bytes=VMEM_LIMIT)


def _split3(x):
    h1 = x.astype(BF16)
    r1 = x - h1.astype(F32)
    h2 = r1.astype(BF16)
    r2 = r1 - h2.astype(F32)
    return h1, h2, r2.astype(BF16)


def _mm(a, b, dims):
    return lax.dot_general(a.astype(BF16), b.astype(BF16), (dims, ((), ())),
                           preferred_element_type=F32)


def _dot(a, b):
    return _mm(a, b, ((1,), (0,)))


def _dot_nt(a, b):
    return _mm(a, b, ((1,), (1,)))


def _dot_tn(a, b):
    return _mm(a, b, ((0,), (0,)))


def _dot_xe(x, e_bf16):
    h1, h2, h3 = _split3(x)
    return _dot(h1, e_bf16) + _dot(h2, e_bf16) + _dot(h3, e_bf16)


def _dot_ex(e_bf16, x):
    h1, h2, h3 = _split3(x)
    return _dot(e_bf16, h1) + _dot(e_bf16, h2) + _dot(e_bf16, h3)


def _dot_hi(a, b):
    ah = a.astype(BF16)
    al = (a - ah.astype(F32)).astype(BF16)
    bh = b.astype(BF16)
    bl = (b - bh.astype(F32)).astype(BF16)
    return _dot(ah, bh) + _dot(al, bh) + _dot(ah, bl)


def _silu(x):
    return x * jax.nn.sigmoid(x)


def _ada_kernel(cb_ref, w_ref, b_ref, o_ref):
    w = w_ref[...]
    reps = w.shape[1] // LANES
    rows = []
    for b in range(cb_ref.shape[0]):
        cb = _silu(cb_ref[b])
        rows.append(jnp.sum(w * pltpu.repeat(cb, reps, axis=1), axis=0, keepdims=True))
    o_ref[...] = jnp.concatenate(rows, axis=0) + b_ref[...]


def _ada_mod(c, ada_w, ada_b):
    bsz, d = c.shape
    n = ada_w.shape[1]
    tn = 1024
    cb = jnp.broadcast_to(c[:, :, None], (bsz, d, LANES))
    return pl.pallas_call(
        _ada_kernel,
        grid=(n // tn,),
        in_specs=[pl.BlockSpec((bsz, d, LANES), lambda j: (0, 0, 0)),
                  pl.BlockSpec((d, tn), lambda j: (0, j)),
                  pl.BlockSpec((1, tn), lambda j: (0, j))],
        out_specs=pl.BlockSpec((bsz, tn), lambda j: (0, j)),
        out_shape=jax.ShapeDtypeStruct((bsz, n), F32),
        compiler_params=_cparams(1),
        name="ada_mod",
    )(cb, ada_w, ada_b.reshape(1, n))


def _rms_mod(x, w, shift, scale):
    ms = jnp.mean(x * x, axis=-1, keepdims=True)
    return (x * lax.rsqrt(ms + RMS_EPS) * w) * (1.0 + scale) + shift


def _inproj_kernel(x_ref, nw_ref, sh_ref, sc_ref, w_ref, mu_ref, o_ref, n_scr, carry_scr):
    i = pl.program_id(1)
    j = pl.program_id(2)

    @pl.when(j == 0)
    def _():
        n_scr[...] = _rms_mod(x_ref[0], nw_ref[...], sh_ref[0], sc_ref[0]).astype(BF16)

    n = n_scr[...]
    tm = n.shape[0]
    pw = 256
    first = i == 0
    for c in range(w_ref.shape[1] // pw):
        cs = slice(c * pw, (c + 1) * pw)
        p = _dot(n, w_ref[:, cs])
        prev_last = jnp.where(first, 0.0, carry_scr[j, 7:8, cs])
        row = lax.broadcasted_iota(jnp.int32, p.shape, 0)
        shifted = jnp.where(row == 0, prev_last, pltpu.roll(p, 1, 0))
        carry_scr[j, :, cs] = p[tm - 8:tm, :]
        o_ref[0, :, cs] = p + (shifted - p) * mu_ref[:, cs]


def _in_proj(x, norm_w, shift, scale, w_packed, mu_packed, tm, tn):
    bsz, t, d = x.shape
    nb = P_TOT // tn
    return pl.pallas_call(
        _inproj_kernel,
        grid=(bsz, t // tm, nb),
        in_specs=[pl.BlockSpec((1, tm, d), lambda b, i, j: (b, i, 0)),
                  pl.BlockSpec((1, d), lambda b, i, j: (0, 0)),
                  pl.BlockSpec((1, 1, d), lambda b, i, j: (b, 0, 0)),
                  pl.BlockSpec((1, 1, d), lambda b, i, j: (b, 0, 0)),
                  pl.BlockSpec((d, tn), lambda b, i, j: (0, j)),
                  pl.BlockSpec((1, tn), lambda b, i, j: (0, j))],
        out_specs=pl.BlockSpec((1, tm, tn), lambda b, i, j: (b, i, j)),
        out_shape=jax.ShapeDtypeStruct((bsz, t, P_TOT), F32),
        scratch_shapes=[pltpu.VMEM((tm, d), BF16), pltpu.VMEM((nb, 8, tn), F32)],
        compiler_params=_cparams(3),
        name="in_proj",
    )(x, norm_w.reshape(1, d), shift.reshape(bsz, 1, d), scale.reshape(bsz, 1, d),
      w_packed, mu_packed)


RWKV_PAIRS = 8
RWKV_A_CHUNKS = 2


def _rwkv_kernel(r_ref, k_ref, v_ref, lora_ref, w0_ref, a0_ref, kk_ref, ka_ref, rk_ref,
                 lnw_ref, lnb_ref, wup_ref, aup_ref, gup_ref, o_ref,
                 s_scr, kkn_scr, k2_scr, beta_scr, logw_scr, cum_scr, o_scr,
                 tinv_scr, arb_scr, vs_scr, lhs_scr, upd_scr, lkv_scr, arkv_scr):
    C = RWKV_CHUNK
    tc = r_ref.shape[1]
    n_chunks = tc // C
    n_pairs = r_ref.shape[2] // LANES

    @pl.when(pl.program_id(2) == 0)
    def _():
        s_scr[...] = jnp.zeros_like(s_scr)

    lane = lax.broadcasted_iota(jnp.int32, (1, LANES), 1)
    first = lane < RW_HEAD
    m0 = first.astype(F32)
    m1 = 1.0 - m0
    row = lax.broadcasted_iota(jnp.int32, (LANES, LANES), 0)
    col = lax.broadcasted_iota(jnp.int32, (LANES, LANES), 1)
    strict = row > col
    incl = row >= col
    blk16 = (row // 16) == (col // 16)
    eye = jnp.where(row == col, 1.0, 0.0)
    r64 = lax.broadcasted_iota(jnp.int32, (C, C), 0)
    c64 = lax.broadcasted_iota(jnp.int32, (C, C), 1)
    tri64 = jnp.where(r64 >= c64, 1.0, 0.0).astype(BF16)

    def per_head(x):
        cols = []
        for p in range(n_pairs):
            xp = x[:, p * LANES:(p + 1) * LANES]
            s0 = jnp.sum(xp * m0, axis=-1, keepdims=True)
            s1 = jnp.sum(xp * m1, axis=-1, keepdims=True)
            cols.append(jnp.where(first, s0, s1))
        return jnp.concatenate(cols, axis=1)

    lora = lora_ref[0]
    wa = lora[:, 0:128]
    w_lin = w0_ref[...] + _dot_hi(jnp.tanh(wa), wup_ref[...])
    logw = -jnp.exp(-jax.nn.softplus(-w_lin) - 0.5)
    a = jax.nn.sigmoid(a0_ref[...] + _dot_hi(wa, aup_ref[...]))
    k = k_ref[0]
    kk = k * kk_ref[...]
    kkn = kk / jnp.maximum(jnp.sqrt(per_head(kk * kk)), 1e-12)
    kkn_scr[...] = kkn
    k2_scr[...] = k * (1.0 + (a - 1.0) * ka_ref[...])
    beta_scr[...] = kkn * a
    logw_scr[...] = logw
    for c in range(n_chunks):
        cum_scr[c * C:(c + 1) * C, :] = _dot_ex(tri64, logw[c * C:(c + 1) * C, :])

    def stack(x):
        return jnp.concatenate([x * m0, x * m1], axis=0)

    lns = [slice(p * LANES, (p + 1) * LANES) for p in range(n_pairs)]

    def indep(it, carry):
        chains = [(it * RWKV_A_CHUNKS + u, p) for u in range(RWKV_A_CHUNKS) for p in range(n_pairs)]
        sls = [pl.ds(pl.multiple_of(c * C, C), C) for c, _ in chains]
        lnc = [lns[p] for _, p in chains]
        cum = [cum_scr[sl, ln] for sl, ln in zip(sls, lnc)]
        ig = [jnp.exp(-c) for c in cum]
        d_end = [jnp.exp(c[C - 1:C, :] - c) for c in cum]
        k2 = [k2_scr[sl, ln] for sl, ln in zip(sls, lnc)]
        beta = [beta_scr[sl, ln] for sl, ln in zip(sls, lnc)]
        v_s = [stack(v_ref[0, sl, ln]).astype(BF16) for sl, ln in zip(sls, lnc)]
        rt_s = [stack(r_ref[0, sl, ln] * jnp.exp(c)) for sl, ln, c in zip(sls, lnc, cum)]
        kap_s = [stack(kkn_scr[sl, ln] * jnp.exp(c - logw_scr[sl, ln]))
                 for sl, ln, c in zip(sls, lnc, cum)]
        bt_s = [stack(b * g) for b, g in zip(beta, ig)]
        kt_s = [stack(k * g) for k, g in zip(k2, ig)]
        lhs = [jnp.concatenate([a, b], axis=0).astype(BF16) for a, b in zip(kap_s, rt_s)]
        for (c, p), k, b, d, l, vs in zip(chains, k2, beta, d_end, lhs, v_s):
            upd_scr[c, p] = jnp.concatenate([stack(k * d), -stack(b * d)], axis=0).astype(BF16)
            lhs_scr[c, p] = l
            vs_scr[c, p] = vs
        gb = [_dot_nt(l, b) for l, b in zip(lhs, bt_s)]
        gk = [_dot_nt(l, k) for l, k in zip(lhs, kt_s)]
        lb = [jnp.where(strict, g[:2 * C], 0.0) for g in gb]
        lk = [jnp.where(strict, g[:2 * C], 0.0) for g in gk]
        ark = [jnp.where(incl, g[2 * C:], 0.0) for g in gk]
        for (c, p), g in zip(chains, gb):
            arb_scr[c, p] = jnp.where(incl, g[2 * C:], 0.0).astype(BF16)

        ld = [jnp.where(blk16, x, 0.0) for x in lb]
        off = [x - d for x, d in zip(lb, ld)]
        m = [-d for d in ld]
        m2 = [_dot(a, a) for a in m]
        m3 = [_dot(a, b) for a, b in zip(m, m2)]
        m4 = [_dot(a, a) for a in m2]
        lkv = [_dot(a, b) for a, b in zip(lk, v_s)]
        m8 = [_dot(a, a) for a in m4]
        arkv = [_dot(a, b) for a, b in zip(ark, v_s)]
        m12 = [_dot(a, b) for a, b in zip(m4, m8)]
        dinv = [_dot(eye + a + b + c, eye + d + e + f)
                for a, b, c, d, e, f in zip(m, m2, m3, m4, m8, m12)]
        n1 = [_dot(d, o) for d, o in zip(dinv, off)]
        n2 = [_dot(a, a) for a in n1]
        x1 = [d + _dot(a, d) for d, a in zip(dinv, n2)]
        for (c, p), x, a, lv, av in zip(chains, x1, n1, lkv, arkv):
            tinv_scr[c, p] = (x - _dot(a, x)).astype(BF16)
            lkv_scr[c, p] = lv
            arkv_scr[c, p] = av
        return carry

    lax.fori_loop(0, n_chunks // RWKV_A_CHUNKS, indep, 0)

    def recur(ci, carry):
        sl = pl.ds(pl.multiple_of(ci * C, C), C)
        P = range(n_pairs)
        s = [s_scr[p] for p in P]
        ks = [_dot_nt(lhs_scr[ci, p], s[p]) for p in P]
        y = [_dot(tinv_scr[ci, p], ks[p][:2 * C] + lkv_scr[ci, p]) for p in P]
        sn = [_dot_tn(jnp.concatenate([vs_scr[ci, p], y[p].astype(BF16)], axis=0), upd_scr[ci, p])
              for p in P]
        for p in P:
            s_scr[p] = s[p] * jnp.exp(cum_scr[sl, lns[p]][C - 1:C, :]) + sn[p]
        arby = [_dot(arb_scr[ci, p], y[p]) for p in P]
        for p in P:
            o_s = ks[p][2 * C:] + arkv_scr[ci, p] - arby[p]
            o_scr[sl, lns[p]] = o_s[:C] + o_s[C:]
        return carry

    lax.fori_loop(0, n_chunks, recur, 0)

    o = o_scr[...]
    mean = per_head(o) * (1.0 / RW_HEAD)
    dlt = o - mean
    var = per_head(dlt * dlt) * (1.0 / RW_HEAD)
    on = dlt * lax.rsqrt(var + GN_EPS) * lnw_ref[...] + lnb_ref[...]
    bonus = per_head(r_ref[0] * k2_scr[...] * rk_ref[...]) * v_ref[0]
    g = _dot_hi(jax.nn.sigmoid(lora[:, 128:384]), gup_ref[...])
    o_ref[0] = ((on + bonus) * g).astype(o_ref.dtype)


def _rwkv(proj, w0, a0, k_k, k_a, r_k, ln_w, ln_b, w_up, a_up, g_up, tc):
    bsz, t, _ = proj.shape
    wl = RWKV_PAIRS * LANES
    row = lambda p: p.reshape(1, RW)
    wup_p = jnp.concatenate([w_up, jnp.zeros_like(a_up)], axis=0)
    aup_p = jnp.concatenate([jnp.zeros_like(w_up), a_up], axis=0)
    gup_p = jnp.concatenate([g_up, jnp.zeros((256 - g_up.shape[0], RW), F32)], axis=0)
    pspec = lambda off: pl.BlockSpec((1, tc, wl), lambda b, h, i: (b, i, off // wl + h))
    vec = pl.BlockSpec((1, wl), lambda b, h, i: (0, h))
    tile = pltpu.VMEM((tc, wl), F32)
    per_chain = lambda rows, dt: pltpu.VMEM((tc // RWKV_CHUNK, RWKV_PAIRS, rows, LANES), dt)
    return pl.pallas_call(
        _rwkv_kernel,
        grid=(bsz, RW // wl, t // tc),
        in_specs=[pspec(P_R), pspec(P_K), pspec(P_V),
                  pl.BlockSpec((1, tc, LORA_W), lambda b, h, i: (b, i, P_LORA // LORA_W)),
                  vec, vec, vec, vec, vec, vec, vec,
                  pl.BlockSpec((128, wl), lambda b, h, i: (0, h)),
                  pl.BlockSpec((128, wl), lambda b, h, i: (0, h)),
                  pl.BlockSpec((256, wl), lambda b, h, i: (0, h))],
        out_specs=pl.BlockSpec((1, tc, wl), lambda b, h, i: (b, i, h)),
        out_shape=jax.ShapeDtypeStruct((bsz, t, RW), BF16),
        scratch_shapes=[pltpu.VMEM((RWKV_PAIRS, LANES, LANES), F32),
                        tile, tile, tile, tile, tile, tile,
                        per_chain(LANES, BF16), per_chain(LANES, BF16), per_chain(LANES, BF16),
                        per_chain(2 * LANES, BF16), per_chain(2 * LANES, BF16),
                        per_chain(LANES, F32), per_chain(LANES, F32)],
        compiler_params=_cparams(3),
        name="rwkv",
    )(proj, proj, proj, proj, row(w0), row(a0), row(k_k), row(k_a), row(r_k), row(ln_w),
      row(ln_b), wup_p, aup_p, gup_p)


def _ssd_kernel(z_ref, x_ref, bc_ref, lora_ref, cwx_ref, cbx_ref, cwb_ref, cbb_ref, dtb_ref,
                alog_ref, dsk_ref, nw_ref, ex_ref, o_ref, extx, extb, st_scr):
    L = SSD_CHUNK

    @pl.when(pl.program_id(1) == 0)
    def _():
        extx[0:8, :] = jnp.zeros((8, extx.shape[1]), F32)
        extb[0:8, :] = jnp.zeros((8, extb.shape[1]), F32)
        st_scr[...] = jnp.zeros_like(st_scr)

    def conv(ext, cur, w_ref, b_ref):
        ext[8:8 + L, :] = cur
        acc = b_ref[...] + w_ref[0:1, :] * ext[5:5 + L, :]
        for j in range(1, SSM_CONV):
            acc = acc + w_ref[j:j + 1, :] * ext[5 + j:5 + j + L, :]
        ext[0:8, :] = ext[L:L + 8, :]
        return _silu(acc)

    xs = conv(extx, x_ref[0], cwx_ref, cbx_ref)
    bc = conv(extb, bc_ref[0], cwb_ref, cbb_ref)

    row = lax.broadcasted_iota(jnp.int32, (L, L), 0)
    col = lax.broadcasted_iota(jnp.int32, (L, L), 1)
    incl = row >= col
    tri = jnp.where(incl, 1.0, 0.0).astype(BF16)
    lane = lax.broadcasted_iota(jnp.int32, (1, LANES), 1)
    m_par = [(lane < SSM_HEAD).astype(F32), (lane >= SSM_HEAD).astype(F32)]

    ex = ex_ref[...]
    dts = jax.nn.softplus(lora_ref[0][:, LORA_DT:LORA_DT + LANES] + dtb_ref[...])
    a_slab = dts * (-jnp.exp(alog_ref[...]))
    acs_slab = _dot_ex(tri, a_slab)
    acs_t = acs_slab.T
    dtx = _dot_xe(dts, ex)
    acs_x = _dot_xe(acs_slab, ex)
    last = acs_x[L - 1:L, :]
    xdt = xs * dtx
    xdte = xdt * jnp.exp(last - acs_x)
    eacs = jnp.exp(acs_x)
    elast = jnp.exp(last)

    ys = []
    for j in range(SSM_HEADS // 2):
        g = j // (SSM_HEADS // 4)
        bm = bc[:, g * LANES:(g + 1) * LANES]
        cm = bc[:, (2 + g) * LANES:(3 + g) * LANES]
        pr = slice(j * LANES, (j + 1) * LANES)
        cb = _dot_nt(cm, bm)
        st = st_scr[j]
        y = eacs[:, pr] * _dot(cm, st)
        xp = xdt[:, pr]
        for par in range(2):
            h = 2 * j + par
            diff = acs_slab[:, h:h + 1] - acs_t[h:h + 1, :]
            mh = jnp.exp(jnp.where(incl, diff, -jnp.inf)) * cb
            y = y + _dot(mh, xp * m_par[par])
        st_scr[j] = st * elast[:, pr] + _dot_tn(bm, xdte[:, pr])
        ys.append(y)
    y = jnp.concatenate(ys, axis=1) + xs * dsk_ref[...]
    y = y * _silu(z_ref[0])
    half = SSM_W // 2
    outs = []
    for g in range(2):
        seg = y[:, g * half:(g + 1) * half]
        outs.append(seg * lax.rsqrt(jnp.mean(seg * seg, axis=-1, keepdims=True) + SSM_EPS))
    o_ref[0] = (jnp.concatenate(outs, axis=1) * nw_ref[...]).astype(o_ref.dtype)


def _ssd(proj, conv_w, conv_b, dt_bias, a_log, d_skip, norm_w):
    bsz, t, _ = proj.shape
    L = SSD_CHUNK
    pad_slab = lambda p: jnp.concatenate([p, jnp.zeros((LANES - SSM_HEADS,), F32)]).reshape(1, LANES)
    hh = jnp.arange(LANES)[:, None]
    ll = jnp.arange(SSM_W)[None, :]
    ex = (hh == ll // SSM_HEAD).astype(BF16)
    cwx, cwb = conv_w[:, :SSM_W], conv_w[:, SSM_W:]
    cbx, cbb = conv_b[:SSM_W].reshape(1, -1), conv_b[SSM_W:].reshape(1, -1)
    full = lambda a: pl.BlockSpec(a.shape, lambda b, i: (0,) * a.ndim)
    args = [cwx, cbx, cwb, cbb, pad_slab(dt_bias), pad_slab(a_log),
            jnp.repeat(d_skip, SSM_HEAD).reshape(1, SSM_W), norm_w.reshape(1, SSM_W), ex]
    return pl.pallas_call(
        _ssd_kernel,
        grid=(bsz, t // L),
        in_specs=[pl.BlockSpec((1, L, SSM_W), lambda b, i: (b, i, P_Z // SSM_W)),
                  pl.BlockSpec((1, L, SSM_W), lambda b, i: (b, i, P_XS // SSM_W)),
                  pl.BlockSpec((1, L, 512), lambda b, i: (b, i, P_BC // 512)),
                  pl.BlockSpec((1, L, LORA_W), lambda b, i: (b, i, P_LORA // LORA_W))]
                 + [full(a) for a in args],
        out_specs=pl.BlockSpec((1, L, SSM_W), lambda b, i: (b, i, 0)),
        out_shape=jax.ShapeDtypeStruct((bsz, t, SSM_W), BF16),
        scratch_shapes=[pltpu.VMEM((L + 8, SSM_W), F32), pltpu.VMEM((L + 8, 512), F32),
                        pltpu.VMEM((SSM_HEADS // 2, SSM_STATE, LANES), F32)],
        compiler_params=_cparams(2),
        name="ssd",
    )(proj, proj, proj, proj, *args)


def _outproj_kernel(orw_ref, oss_ref, x_ref, g1_ref, nw_ref, sh_ref, sc_ref, wo1_ref, wo2_ref,
                    wq_ref, h1_ref, n2_ref, q_ref):
    mix = _dot(orw_ref[0], wo1_ref[...]) + _dot(oss_ref[0], wo2_ref[...])
    h1 = x_ref[0] + g1_ref[0] * mix
    h1_ref[0] = h1
    n2 = _rms_mod(h1, nw_ref[...], sh_ref[0], sc_ref[0]).astype(BF16)
    n2_ref[0] = n2
    q_ref[0] = _dot(n2, wq_ref[...])


def _out_proj(o_rw, o_ss, x, gate1, norm_w, shift, scale, w_out, w_query_p, tm):
    bsz, t, d = x.shape
    wo1 = w_out[:RW].astype(BF16)
    wo2 = w_out[RW:].astype(BF16)
    tok = lambda w: pl.BlockSpec((1, tm, w), lambda b, i: (b, i, 0))
    per_b = pl.BlockSpec((1, 1, d), lambda b, i: (b, 0, 0))
    const = lambda a: pl.BlockSpec(a.shape, lambda b, i: (0,) * a.ndim, pipeline_mode=pl.Buffered(1))
    nw = norm_w.reshape(1, d)
    return pl.pallas_call(
        _outproj_kernel,
        grid=(bsz, t // tm),
        in_specs=[tok(RW), tok(SSM_W), tok(d), per_b, const(nw), per_b, per_b,
                  const(wo1), const(wo2), const(w_query_p)],
        out_specs=[tok(d), tok(d), tok(d)],
        out_shape=[jax.ShapeDtypeStruct((bsz, t, d), F32), jax.ShapeDtypeStruct((bsz, t, d), BF16),
                   jax.ShapeDtypeStruct((bsz, t, d), F32)],
        compiler_params=_cparams(2),
        name="out_proj",
    )(o_rw, o_ss, x, gate1.reshape(bsz, 1, d), nw, shift.reshape(bsz, 1, d),
      scale.reshape(bsz, 1, d), wo1, wo2, w_query_p)


_CANDS = [(i, j) for i in range(PEER_TOPK) for j in range(PEER_TOPK) if (i + 1) * (j + 1) <= PEER_TOPK]


def _route_kernel(q_ref, k1_ref, k2_ref, cnt_ref, e1_ref, rk_ref, e2_ref):
    nh, hd, K = PEER_HEADS, PEER_HALF, PEER_TOPK
    q = q_ref[...]

    def top_vals(s, want_rank):
        vals, prev = [], None
        rank = jnp.zeros_like(s) if want_rank else None
        for _ in range(K):
            if prev is None:
                cand = s
            else:
                below = s < prev
                cand = jnp.where(below, s, -jnp.inf)
                if want_rank:
                    rank = jnp.where(below, rank + 1.0, rank)
            prev = jnp.max(cand, axis=0, keepdims=True)
            vals.append(prev)
        if want_rank:
            rank = jnp.where(s < prev, rank + 1.0, rank)
        return vals, rank

    s1 = [_dot_nt(k1_ref[h], q[:, h * hd:(h + 1) * hd]) for h in range(nh)]
    s2 = [_dot_nt(k2_ref[h], q[:, (nh + h) * hd:(nh + h + 1) * hd]) for h in range(nh)]
    a_h = [top_vals(s, False)[0] for s in s1]
    b_hr = [top_vals(s, True) for s in s2]
    b_h = [x[0] for x in b_hr]
    a = [jnp.concatenate([a_h[h][i] for h in range(nh)], axis=0) for i in range(K)]
    b = [jnp.concatenate([b_h[h][i] for h in range(nh)], axis=0) for i in range(K)]

    sums = {c: a[c[0]] + b[c[1]] for c in _CANDS}
    tau = None
    for _ in range(K):
        best = None
        for c in _CANDS:
            v = sums[c] if tau is None else jnp.where(sums[c] < tau, sums[c], -jnp.inf)
            best = v if best is None else jnp.maximum(best, v)
        tau = best
    ea = [jnp.exp(ai - a[0]) for ai in a]
    eb = [jnp.exp(bj - b[0]) for bj in b]
    zsum = jnp.zeros_like(tau)
    cnt = [jnp.zeros_like(tau) for _ in range(K)]
    for (i, j) in _CANDS:
        sel = sums[(i, j)] >= tau
        zsum = zsum + jnp.where(sel, ea[i] * eb[j], 0.0)
        cnt[i] = cnt[i] + jnp.where(sel, 1.0, 0.0)
    zinv = 1.0 / zsum
    for h in range(nh):
        row = slice(h, h + 1)
        cfull = jnp.zeros_like(s1[h])
        for i in range(K):
            cfull = jnp.where(s1[h] == a_h[h][i], cnt[i][row, :], cfull)
        cnt_ref[h] = cfull
        e1_ref[h] = jnp.exp(s1[h] - a_h[h][0])
        rk_ref[h] = b_hr[h][1].astype(BF16)
        e2_ref[h] = (jnp.exp(s2[h] - b_h[h][0]) * zinv[row, :]).astype(BF16)


def _peer_route(q, sub_keys, tm):
    ntok = q.shape[0]
    nk, nh = PEER_KEYS, PEER_HEADS
    k1 = sub_keys[:, 0].astype(BF16)
    k2 = sub_keys[:, 1].astype(BF16)
    blk = pl.BlockSpec((nh, nk, tm), lambda i: (0, 0, i))
    f32_out = jax.ShapeDtypeStruct((nh, nk, ntok), F32)
    bf16_out = jax.ShapeDtypeStruct((nh, nk, ntok), BF16)
    return pl.pallas_call(
        _route_kernel,
        grid=(ntok // tm,),
        in_specs=[pl.BlockSpec((tm, q.shape[1]), lambda i: (i, 0)),
                  pl.BlockSpec(k1.shape, lambda i: (0, 0, 0)),
                  pl.BlockSpec(k2.shape, lambda i: (0, 0, 0))],
        out_specs=[blk, blk, blk, blk],
        out_shape=[f32_out, f32_out, bf16_out, bf16_out],
        compiler_params=_cparams(1),
        name="peer_route",
    )(q, k1, k2)


def _dense_kernel(n2_ref, u_ref, vt_ref, cnt_ref, e1_ref, rk_ref, e2_ref, o_ref, w_scr):
    j = pl.program_id(1)
    nk = PEER_KEYS
    ec = u_ref.shape[0]
    tm = n2_ref.shape[0]
    n_piece = 4
    pc = ec // n_piece
    zero = jnp.zeros((), BF16)
    n2 = n2_ref[...]

    def weights(p, act):
        for s in range(pc // nk):
            sl = p * (pc // nk) + s
            gsum = jnp.zeros((nk, tm), BF16)
            for h in range(PEER_HEADS):
                cb = jnp.broadcast_to(cnt_ref[h, sl:sl + 1, :], (nk, tm)).astype(BF16)
                eb = jnp.broadcast_to(e1_ref[h, sl:sl + 1, :], (nk, tm)).astype(BF16)
                gsum = gsum + jnp.where(rk_ref[h] < cb, e2_ref[h], zero) * eb
            a = act[s * nk:(s + 1) * nk, :]
            gelu = 0.5 * a * (1.0 + lax.erf(a * (2.0 ** -0.5)))
            w_scr[sl * nk:(sl + 1) * nk, :] = gsum * gelu.astype(BF16)

    acts = [_dot_nt(u_ref[p * pc:(p + 1) * pc, :], n2) for p in range(n_piece)]
    for p in range(n_piece):
        weights(p, acts[p])
        part = _dot(vt_ref[:, p * pc:(p + 1) * pc], w_scr[p * pc:(p + 1) * pc, :])
        prev = jnp.where(j == 0, 0.0, o_ref[...]) if p == 0 else o_ref[...]
        o_ref[...] = prev + part


def _peer_dense(n2, u_bf, vt_bf, cnt, e1, rk, e2, tm, ec):
    ntok, d = n2.shape
    n_exp = u_bf.shape[0]
    ns = ec // PEER_KEYS
    rows = pl.BlockSpec((PEER_HEADS, ns, tm), lambda i, j: (0, j, i))
    slab = pl.BlockSpec((PEER_HEADS, PEER_KEYS, tm), lambda i, j: (0, 0, i))
    return pl.pallas_call(
        _dense_kernel,
        grid=(ntok // tm, n_exp // ec),
        in_specs=[pl.BlockSpec((tm, d), lambda i, j: (i, 0)),
                  pl.BlockSpec((ec, d), lambda i, j: (j, 0)),
                  pl.BlockSpec((d, ec), lambda i, j: (0, j)),
                  rows, rows, slab, slab],
        out_specs=pl.BlockSpec((d, tm), lambda i, j: (0, i)),
        out_shape=jax.ShapeDtypeStruct((d, ntok), F32),
        scratch_shapes=[pltpu.VMEM((ec, tm), BF16)],
        compiler_params=_cparams(2),
        name="peer_dense",
    )(n2, u_bf, vt_bf, cnt, e1, rk, e2)


def _final_kernel(h1_ref, pt_ref, g2_ref, nw_ref, o_ref):
    h = h1_ref[0] + g2_ref[0] * pt_ref[...].T
    ms = jnp.mean(h * h, axis=-1, keepdims=True)
    o_ref[0] = (h * lax.rsqrt(ms + RMS_EPS) * nw_ref[...]).astype(o_ref.dtype)


def _final(h1, peer_t, gate2, norm_w, out_dtype, tm):
    bsz, t, d = h1.shape
    nt = t // tm
    return pl.pallas_call(
        _final_kernel,
        grid=(bsz, nt),
        in_specs=[pl.BlockSpec((1, tm, d), lambda b, i: (b, i, 0)),
                  pl.BlockSpec((d, tm), lambda b, i: (0, b * nt + i)),
                  pl.BlockSpec((1, 1, d), lambda b, i: (b, 0, 0)),
                  pl.BlockSpec((1, d), lambda b, i: (0, 0))],
        out_specs=pl.BlockSpec((1, tm, d), lambda b, i: (b, i, 0)),
        out_shape=jax.ShapeDtypeStruct((bsz, t, d), out_dtype),
        compiler_params=_cparams(2),
        name="final_norm",
    )(h1, peer_t, gate2.reshape(bsz, 1, d), norm_w.reshape(1, d))


def _pack_in_proj(w_in, mu):
    d = w_in.shape[0]
    rp = 3 * RW + 64 + 64 + 160
    zc = lambda n: jnp.zeros((d, n), w_in.dtype)
    z0 = rp
    dt0 = rp + SSM_W + SSM_W + 512
    w = jnp.concatenate([
        w_in[:, :3 * RW],
        w_in[:, z0:z0 + SSM_W],
        w_in[:, z0 + SSM_W:z0 + 2 * SSM_W],
        w_in[:, z0 + 2 * SSM_W:z0 + 2 * SSM_W + 512],
        w_in[:, 3 * RW:rp], zc(LORA_DT - 288),
        w_in[:, dt0:dt0 + SSM_HEADS], zc(LORA_W - LORA_DT - SSM_HEADS)], axis=1)
    m = jnp.concatenate([mu[:3 * RW], jnp.zeros((P_LORA - 3 * RW,), mu.dtype), mu[3 * RW:rp],
                         jnp.zeros((LORA_W - 288,), mu.dtype)])
    return w.astype(BF16), m.reshape(1, P_TOT)


def _tile(t, pref):
    return pref if t % pref == 0 else t


def kernel(x, c, ada_w, ada_b, norm1_w, w_in, rwkv_mu, rwkv_w0, rwkv_w_up, rwkv_a0, rwkv_a_up, rwkv_g_up, rwkv_k_k, rwkv_k_a, rwkv_r_k, rwkv_ln_w, rwkv_ln_b, mamba_conv_w, mamba_conv_b, mamba_dt_bias, mamba_a_log, mamba_d, mamba_norm_w, w_out, norm2_w, peer_w_query, peer_sub_keys, peer_u, peer_v, final_norm_w):
    out_dtype = x.dtype
    bsz, t, d = x.shape
    ntok = bsz * t
    assert ada_w.shape[0] == 1, "single-layer trunk"
    h = x.astype(F32)
    mod = _ada_mod(c.astype(F32), ada_w[0], ada_b[0])
    shift1, scale1, gate1, shift2, scale2, gate2 = jnp.split(mod, 6, axis=-1)
    w_packed, mu_packed = _pack_in_proj(w_in[0], rwkv_mu[0])
    proj = _in_proj(h, norm1_w[0], shift1, scale1, w_packed, mu_packed, _tile(t, 1024), 768)
    o_rw = _rwkv(proj, rwkv_w0[0], rwkv_a0[0], rwkv_k_k[0], rwkv_k_a[0], rwkv_r_k[0].reshape(-1),
                 rwkv_ln_w[0], rwkv_ln_b[0], rwkv_w_up[0], rwkv_a_up[0], rwkv_g_up[0],
                 _tile(t, 256))
    o_ss = _ssd(proj, mamba_conv_w[0], mamba_conv_b[0], mamba_dt_bias[0], mamba_a_log[0],
                mamba_d[0], mamba_norm_w[0])
    wq = peer_w_query[0].reshape(d, PEER_HEADS, 2, PEER_HALF).transpose(0, 2, 1, 3)
    wq = wq.reshape(d, 2 * PEER_HEADS * PEER_HALF).astype(BF16)
    h1, n2, q = _out_proj(o_rw, o_ss, h, gate1, norm2_w[0], shift2, scale2, w_out[0], wq,
                          _tile(t, 256))
    cnt, e1, rk, e2 = _peer_route(q.reshape(ntok, -1), peer_sub_keys[0], _tile(ntok, 256))
    peer_t = _peer_dense(n2.reshape(ntok, d), peer_u[0].astype(BF16), peer_v[0].T.astype(BF16),
                         cnt, e1, rk, e2, _tile(ntok, 512), 1024)
    return _final(h1, peer_t, gate2, final_norm_w, out_dtype, _tile(t, 512))
```

```python
import functools

import jax
import jax.numpy as jnp
from jax import lax
from jax.experimental import pallas as pl
from jax.experimental.pallas import tpu as pltpu

F32 = jnp.float32
BF16 = jnp.bfloat16

D_MODEL = 2048
RW = 1024
RW_HEAD = 64
SSM_W = 1024
SSM_HEAD = 64
SSM_HEADS = 16
SSM_STATE = 128
SSM_CONV = 4
PEER_HEADS = 8
PEER_KEYS = 128
PEER_HALF = 128
PEER_TOPK = 16
RMS_EPS = 1e-6
GN_EPS = 64e-5
SSM_EPS = 1e-5

P_R, P_K, P_V, P_Z, P_XS, P_BC, P_LORA = 0, 1024, 2048, 3072, 4096, 5120, 5632
P_TOT = 6144
LORA_W = 512
LORA_DT = 384

RWKV_CHUNK = 64
SSD_CHUNK = 128
LANES = 128

VMEM_LIMIT = 56 * 1024 * 1024


def _cparams(n_axes):
    return pltpu.CompilerParams(
        dimension_semantics=("arbitrary",) * n_axes, vmem_limit_bytes=VMEM_LIMIT)


def _split3(x):
    h1 = x.astype(BF16)
    r1 = x - h1.astype(F32)
    h2 = r1.astype(BF16)
    r2 = r1 - h2.astype(F32)
    return h1, h2, r2.astype(BF16)


def _mm(a, b, dims):
    return lax.dot_general(a.astype(BF16), b.astype(BF16), (dims, ((), ())),
                           preferred_element_type=F32)


def _dot(a, b):
    return _mm(a, b, ((1,), (0,)))


def _dot_nt(a, b):
    return _mm(a, b, ((1,), (1,)))


def _dot_tn(a, b):
    return _mm(a, b, ((0,), (0,)))


def _dot_xe(x, e_bf16):
    h1, h2, h3 = _split3(x)
    return _dot(h1, e_bf16) + _dot(h2, e_bf16) + _dot(h3, e_bf16)


def _dot_ex(e_bf16, x):
    h1, h2, h3 = _split3(x)
    return _dot(e_bf16, h1) + _dot(e_bf16, h2) + _dot(e_bf16, h3)


def _dot_hi(a, b):
    ah = a.astype(BF16)
    al = (a - ah.astype(F32)).astype(BF16)
    bh = b.astype(BF16)
    bl = (b - bh.astype(F32)).astype(BF16)
    return _dot(ah, bh) + _dot(al, bh) + _dot(ah, bl)


def _silu(x):
    return x * jax.nn.sigmoid(x)


def _ada_kernel(cb_ref, w_ref, b_ref, o_ref):
    w = w_ref[...]
    reps = w.shape[1] // LANES
    rows = []
    for b in range(cb_ref.shape[0]):
        cb = _silu(cb_ref[b])
        rows.append(jnp.sum(w * jnp.concatenate([cb] * reps, axis=1), axis=0, keepdims=True))
    o_ref[...] = jnp.concatenate(rows, axis=0) + b_ref[...]


def _ada_mod(c, ada_w, ada_b):
    bsz, d = c.shape
    n = ada_w.shape[1]
    tn = 1024
    cb = jnp.broadcast_to(c[:, :, None], (bsz, d, LANES))
    return pl.pallas_call(
        _ada_kernel,
        grid=(n // tn,),
        in_specs=[pl.BlockSpec((bsz, d, LANES), lambda j: (0, 0, 0)),
                  pl.BlockSpec((d, tn), lambda j: (0, j)),
                  pl.BlockSpec((1, tn), lambda j: (0, j))],
        out_specs=pl.BlockSpec((bsz, tn), lambda j: (0, j)),
        out_shape=jax.ShapeDtypeStruct((bsz, n), F32),
        compiler_params=_cparams(1),
        name="ada_mod",
    )(cb, ada_w, ada_b.reshape(1, n))


def _rms_mod(x, w, shift, scale):
    ms = jnp.mean(x * x, axis=-1, keepdims=True)
    return (x * lax.rsqrt(ms + RMS_EPS) * w) * (1.0 + scale) + shift


def _inproj_kernel(x_ref, nw_ref, sh_ref, sc_ref, w_ref, mu_ref, o_ref, n_scr, carry_scr):
    i = pl.program_id(1)
    j = pl.program_id(2)

    @pl.when(j == 0)
    def _():
        n_scr[...] = _rms_mod(x_ref[0], nw_ref[...], sh_ref[0], sc_ref[0]).astype(BF16)

    n = n_scr[...]
    tm = n.shape[0]
    pw = 256
    first = i == 0
    for c in range(w_ref.shape[1] // pw):
        cs = slice(c * pw, (c + 1) * pw)
        p = _dot(n, w_ref[:, cs])
        prev_last = jnp.where(first, 0.0, carry_scr[j, 7:8, cs])
        row = lax.broadcasted_iota(jnp.int32, p.shape, 0)
        shifted = jnp.where(row == 0, prev_last, pltpu.roll(p, 1, 0))
        carry_scr[j, :, cs] = p[tm - 8:tm, :]
        o_ref[0, :, cs] = p + (shifted - p) * mu_ref[:, cs]


def _in_proj(x, norm_w, shift, scale, w_packed, mu_packed, tm, tn):
    bsz, t, d = x.shape
    nb = P_TOT // tn
    return pl.pallas_call(
        _inproj_kernel,
        grid=(bsz, t // tm, nb),
        in_specs=[pl.BlockSpec((1, tm, d), lambda b, i, j: (b, i, 0)),
                  pl.BlockSpec((1, d), lambda b, i, j: (0, 0)),
                  pl.BlockSpec((1, 1, d), lambda b, i, j: (b, 0, 0)),
                  pl.BlockSpec((1, 1, d), lambda b, i, j: (b, 0, 0)),
                  pl.BlockSpec((d, tn), lambda b, i, j: (0, j)),
                  pl.BlockSpec((1, tn), lambda b, i, j: (0, j))],
        out_specs=pl.BlockSpec((1, tm, tn), lambda b, i, j: (b, i, j)),
        out_shape=jax.ShapeDtypeStruct((bsz, t, P_TOT), F32),
        scratch_shapes=[pltpu.VMEM((tm, d), BF16), pltpu.VMEM((nb, 8, tn), F32)],
        compiler_params=_cparams(3),
        name="in_proj",
    )(x, norm_w.reshape(1, d), shift.reshape(bsz, 1, d), scale.reshape(bsz, 1, d),
      w_packed, mu_packed)


RWKV_PAIRS = 8
RWKV_A_CHUNKS = 2


def _rwkv_kernel(r_ref, k_ref, v_ref, lora_ref, w0_ref, a0_ref, kk_ref, ka_ref, rk_ref,
                 lnw_ref, lnb_ref, wup_ref, aup_ref, gup_ref, o_ref,
                 s_scr, kkn_scr, k2_scr, beta_scr, logw_scr, cum_scr, o_scr,
                 tinv_scr, arb_scr, vs_scr, lhs_scr, upd_scr, lkv_scr, arkv_scr):
    C = RWKV_CHUNK
    tc = r_ref.shape[1]
    n_chunks = tc // C
    n_pairs = r_ref.shape[2] // LANES

    @pl.when(pl.program_id(2) == 0)
    def _():
        s_scr[...] = jnp.zeros_like(s_scr)

    lane = lax.broadcasted_iota(jnp.int32, (1, LANES), 1)
    first = lane < RW_HEAD
    m0 = first.astype(F32)
    m1 = 1.0 - m0
    row = lax.broadcasted_iota(jnp.int32, (LANES, LANES), 0)
    col = lax.broadcasted_iota(jnp.int32, (LANES, LANES), 1)
    strict = row > col
    incl = row >= col
    blk16 = (row // 16) == (col // 16)
    eye = jnp.where(row == col, 1.0, 0.0)
    r64 = lax.broadcasted_iota(jnp.int32, (C, C), 0)
    c64 = lax.broadcasted_iota(jnp.int32, (C, C), 1)
    tri64 = jnp.where(r64 >= c64, 1.0, 0.0).astype(BF16)

    def per_head(x):
        cols = []
        for p in range(n_pairs):
            xp = x[:, p * LANES:(p + 1) * LANES]
            s0 = jnp.sum(xp * m0, axis=-1, keepdims=True)
            s1 = jnp.sum(xp * m1, axis=-1, keepdims=True)
            cols.append(jnp.where(first, s0, s1))
        return jnp.concatenate(cols, axis=1)

    lora = lora_ref[0]
    wa = lora[:, 0:128]
    w_lin = w0_ref[...] + _dot_hi(jnp.tanh(wa), wup_ref[...])
    logw = -jnp.exp(-jax.nn.softplus(-w_lin) - 0.5)
    a = jax.nn.sigmoid(a0_ref[...] + _dot_hi(wa, aup_ref[...]))
    k = k_ref[0]
    kk = k * kk_ref[...]
    kkn = kk / jnp.maximum(jnp.sqrt(per_head(kk * kk)), 1e-12)
    kkn_scr[...] = kkn
    k2_scr[...] = k * (1.0 + (a - 1.0) * ka_ref[...])
    beta_scr[...] = kkn * a
    logw_scr[...] = logw
    for c in range(n_chunks):
        cum_scr[c * C:(c + 1) * C, :] = _dot_ex(tri64, logw[c * C:(c + 1) * C, :])

    def stack(x):
        return jnp.concatenate([x * m0, x * m1], axis=0)

    lns = [slice(p * LANES, (p + 1) * LANES) for p in range(n_pairs)]

    def indep(it, carry):
        chains = [(it * RWKV_A_CHUNKS + u, p) for u in range(RWKV_A_CHUNKS) for p in range(n_pairs)]
        sls = [pl.ds(pl.multiple_of(c * C, C), C) for c, _ in chains]
        lnc = [lns[p] for _, p in chains]
        cum = [cum_scr[sl, ln] for sl, ln in zip(sls, lnc)]
        ig = [jnp.exp(-c) for c in cum]
        d_end = [jnp.exp(c[C - 1:C, :] - c) for c in cum]
        k2 = [k2_scr[sl, ln] for sl, ln in zip(sls, lnc)]
        beta = [beta_scr[sl, ln] for sl, ln in zip(sls, lnc)]
        v_s = [stack(v_ref[0, sl, ln]).astype(BF16) for sl, ln in zip(sls, lnc)]
        rt_s = [stack(r_ref[0, sl, ln] * jnp.exp(c)) for sl, ln, c in zip(sls, lnc, cum)]
        kap_s = [stack(kkn_scr[sl, ln] * jnp.exp(c - logw_scr[sl, ln]))
                 for sl, ln, c in zip(sls, lnc, cum)]
        bt_s = [stack(b * g) for b, g in zip(beta, ig)]
        kt_s = [stack(k * g) for k, g in zip(k2, ig)]
        lhs = [jnp.concatenate([a, b], axis=0).astype(BF16) for a, b in zip(kap_s, rt_s)]
        for (c, p), k, b, d, l, vs in zip(chains, k2, beta, d_end, lhs, v_s):
            upd_scr[c, p] = jnp.concatenate([stack(k * d), -stack(b * d)], axis=0).astype(BF16)
            lhs_scr[c, p] = l
            vs_scr[c, p] = vs
        gb = [_dot_nt(l, b) for l, b in zip(lhs, bt_s)]
        gk = [_dot_nt(l, k) for l, k in zip(lhs, kt_s)]
        lb = [jnp.where(strict, g[:2 * C], 0.0) for g in gb]
        lk = [jnp.where(strict, g[:2 * C], 0.0) for g in gk]
        ark = [jnp.where(incl, g[2 * C:], 0.0) for g in gk]
        for (c, p), g in zip(chains, gb):
            arb_scr[c, p] = jnp.where(incl, g[2 * C:], 0.0).astype(BF16)

        ld = [jnp.where(blk16, x, 0.0) for x in lb]
        off = [x - d for x, d in zip(lb, ld)]
        m = [-d for d in ld]
        m2 = [_dot(a, a) for a in m]
        m3 = [_dot(a, b) for a, b in zip(m, m2)]
        m4 = [_dot(a, a) for a in m2]
        lkv = [_dot(a, b) for a, b in zip(lk, v_s)]
        m8 = [_dot(a, a) for a in m4]
        arkv = [_dot(a, b) for a, b in zip(ark, v_s)]
        m12 = [_dot(a, b) for a, b in zip(m4, m8)]
        dinv = [_dot(eye + a + b + c, eye + d + e + f)
                for a, b, c, d, e, f in zip(m, m2, m3, m4, m8, m12)]
        n1 = [_dot(d, o) for d, o in zip(dinv, off)]
        n2 = [_dot(a, a) for a in n1]
        x1 = [d + _dot(a, d) for d, a in zip(dinv, n2)]
        for (c, p), x, a, lv, av in zip(chains, x1, n1, lkv, arkv):
            tinv_scr[c, p] = (x - _dot(a, x)).astype(BF16)
            lkv_scr[c, p] = lv
            arkv_scr[c, p] = av
        return carry

    lax.fori_loop(0, n_chunks // RWKV_A_CHUNKS, indep, 0)

    def recur(ci, carry):
        sl = pl.ds(pl.multiple_of(ci * C, C), C)
        P = range(n_pairs)
        s = [s_scr[p] for p in P]
        ks = [_dot_nt(lhs_scr[ci, p], s[p]) for p in P]
        y = [_dot(tinv_scr[ci, p], ks[p][:2 * C] + lkv_scr[ci, p]) for p in P]
        sn = [_dot_tn(jnp.concatenate([vs_scr[ci, p], y[p].astype(BF16)], axis=0), upd_scr[ci, p])
              for p in P]
        for p in P:
            s_scr[p] = s[p] * jnp.exp(cum_scr[sl, lns[p]][C - 1:C, :]) + sn[p]
        arby = [_dot(arb_scr[ci, p], y[p]) for p in P]
        for p in P:
            o_s = ks[p][2 * C:] + arkv_scr[ci, p] - arby[p]
            o_scr[sl, lns[p]] = o_s[:C] + o_s[C:]
        return carry

    lax.fori_loop(0, n_chunks, recur, 0)

    o = o_scr[...]
    mean = per_head(o) * (1.0 / RW_HEAD)
    dlt = o - mean
    var = per_head(dlt * dlt) * (1.0 / RW_HEAD)
    on = dlt * lax.rsqrt(var + GN_EPS) * lnw_ref[...] + lnb_ref[...]
    bonus = per_head(r_ref[0] * k2_scr[...] * rk_ref[...]) * v_ref[0]
    g = _dot_hi(jax.nn.sigmoid(lora[:, 128:384]), gup_ref[...])
    o_ref[0] = ((on + bonus) * g).astype(o_ref.dtype)


def _rwkv(proj, w0, a0, k_k, k_a, r_k, ln_w, ln_b, w_up, a_up, g_up, tc):
    bsz, t, _ = proj.shape
    wl = RWKV_PAIRS * LANES
    row = lambda p: p.reshape(1, RW)
    wup_p = jnp.concatenate([w_up, jnp.zeros_like(a_up)], axis=0)
    aup_p = jnp.concatenate([jnp.zeros_like(w_up), a_up], axis=0)
    gup_p = jnp.concatenate([g_up, jnp.zeros((256 - g_up.shape[0], RW), F32)], axis=0)
    pspec = lambda off: pl.BlockSpec((1, tc, wl), lambda b, h, i: (b, i, off // wl + h))
    vec = pl.BlockSpec((1, wl), lambda b, h, i: (0, h))
    tile = pltpu.VMEM((tc, wl), F32)
    per_chain = lambda rows, dt: pltpu.VMEM((tc // RWKV_CHUNK, RWKV_PAIRS, rows, LANES), dt)
    return pl.pallas_call(
        _rwkv_kernel,
        grid=(bsz, RW // wl, t // tc),
        in_specs=[pspec(P_R), pspec(P_K), pspec(P_V),
                  pl.BlockSpec((1, tc, LORA_W), lambda b, h, i: (b, i, P_LORA // LORA_W)),
                  vec, vec, vec, vec, vec, vec, vec,
                  pl.BlockSpec((128, wl), lambda b, h, i: (0, h)),
                  pl.BlockSpec((128, wl), lambda b, h, i: (0, h)),
                  pl.BlockSpec((256, wl), lambda b, h, i: (0, h))],
        out_specs=pl.BlockSpec((1, tc, wl), lambda b, h, i: (b, i, h)),
        out_shape=jax.ShapeDtypeStruct((bsz, t, RW), BF16),
        scratch_shapes=[pltpu.VMEM((RWKV_PAIRS, LANES, LANES), F32),
                        tile, tile, tile, tile, tile, tile,
                        per_chain(LANES, BF16), per_chain(LANES, BF16), per_chain(LANES, BF16),
                        per_chain(2 * LANES, BF16), per_chain(2 * LANES, BF16),
                        per_chain(LANES, F32), per_chain(LANES, F32)],
        compiler_params=_cparams(3),
        name="rwkv",
    )(proj, proj, proj, proj, row(w0), row(a0), row(k_k), row(k_a), row(r_k), row(ln_w),
      row(ln_b), wup_p, aup_p, gup_p)


def _ssd_kernel(z_ref, x_ref, bc_ref, lora_ref, cwx_ref, cbx_ref, cwb_ref, cbb_ref, dtb_ref,
                alog_ref, dsk_ref, nw_ref, ex_ref, o_ref, extx, extb, st_scr):
    L = SSD_CHUNK

    @pl.when(pl.program_id(1) == 0)
    def _():
        extx[0:8, :] = jnp.zeros((8, extx.shape[1]), F32)
        extb[0:8, :] = jnp.zeros((8, extb.shape[1]), F32)
        st_scr[...] = jnp.zeros_like(st_scr)

    def conv(ext, cur, w_ref, b_ref):
        ext[8:8 + L, :] = cur
        acc = b_ref[...] + w_ref[0:1, :] * ext[5:5 + L, :]
        for j in range(1, SSM_CONV):
            acc = acc + w_ref[j:j + 1, :] * ext[5 + j:5 + j + L, :]
        ext[0:8, :] = ext[L:L + 8, :]
        return _silu(acc)

    xs = conv(extx, x_ref[0], cwx_ref, cbx_ref)
    bc = conv(extb, bc_ref[0], cwb_ref, cbb_ref)

    row = lax.broadcasted_iota(jnp.int32, (L, L), 0)
    col = lax.broadcasted_iota(jnp.int32, (L, L), 1)
    incl = row >= col
    tri = jnp.where(incl, 1.0, 0.0).astype(BF16)
    lane = lax.broadcasted_iota(jnp.int32, (1, LANES), 1)
    m_par = [(lane < SSM_HEAD).astype(F32), (lane >= SSM_HEAD).astype(F32)]

    ex = ex_ref[...]
    dts = jax.nn.softplus(lora_ref[0][:, LORA_DT:LORA_DT + LANES] + dtb_ref[...])
    a_slab = dts * (-jnp.exp(alog_ref[...]))
    acs_slab = _dot_ex(tri, a_slab)
    acs_t = acs_slab.T
    dtx = _dot_xe(dts, ex)
    acs_x = _dot_xe(acs_slab, ex)
    last = acs_x[L - 1:L, :]
    xdt = xs * dtx
    xdte = xdt * jnp.exp(last - acs_x)
    eacs = jnp.exp(acs_x)
    elast = jnp.exp(last)

    ys = []
    for j in range(SSM_HEADS // 2):
        g = j // (SSM_HEADS // 4)
        bm = bc[:, g * LANES:(g + 1) * LANES]
        cm = bc[:, (2 + g) * LANES:(3 + g) * LANES]
        pr = slice(j * LANES, (j + 1) * LANES)
        cb = _dot_nt(cm, bm)
        st = st_scr[j]
        y = eacs[:, pr] * _dot(cm, st)
        xp = xdt[:, pr]
        for par in range(2):
            h = 2 * j + par
            diff = acs_slab[:, h:h + 1] - acs_t[h:h + 1, :]
            mh = jnp.exp(jnp.where(incl, diff, -jnp.inf)) * cb
            y = y + _dot(mh, xp * m_par[par])
        st_scr[j] = st * elast[:, pr] + _dot_tn(bm, xdte[:, pr])
        ys.append(y)
    y = jnp.concatenate(ys, axis=1) + xs * dsk_ref[...]
    y = y * _silu(z_ref[0])
    half = SSM_W // 2
    outs = []
    for g in range(2):
        seg = y[:, g * half:(g + 1) * half]
        outs.append(seg * lax.rsqrt(jnp.mean(seg * seg, axis=-1, keepdims=True) + SSM_EPS))
    o_ref[0] = (jnp.concatenate(outs, axis=1) * nw_ref[...]).astype(o_ref.dtype)


def _ssd(proj, conv_w, conv_b, dt_bias, a_log, d_skip, norm_w):
    bsz, t, _ = proj.shape
    L = SSD_CHUNK
    pad_slab = lambda p: jnp.concatenate([p, jnp.zeros((LANES - SSM_HEADS,), F32)]).reshape(1, LANES)
    hh = jnp.arange(LANES)[:, None]
    ll = jnp.arange(SSM_W)[None, :]
    ex = (hh == ll // SSM_HEAD).astype(BF16)
    cwx, cwb = conv_w[:, :SSM_W], conv_w[:, SSM_W:]
    cbx, cbb = conv_b[:SSM_W].reshape(1, -1), conv_b[SSM_W:].reshape(1, -1)
    full = lambda a: pl.BlockSpec(a.shape, lambda b, i: (0,) * a.ndim)
    args = [cwx, cbx, cwb, cbb, pad_slab(dt_bias), pad_slab(a_log),
            jnp.repeat(d_skip, SSM_HEAD).reshape(1, SSM_W), norm_w.reshape(1, SSM_W), ex]
    return pl.pallas_call(
        _ssd_kernel,
        grid=(bsz, t // L),
        in_specs=[pl.BlockSpec((1, L, SSM_W), lambda b, i: (b, i, P_Z // SSM_W)),
                  pl.BlockSpec((1, L, SSM_W), lambda b, i: (b, i, P_XS // SSM_W)),
                  pl.BlockSpec((1, L, 512), lambda b, i: (b, i, P_BC // 512)),
                  pl.BlockSpec((1, L, LORA_W), lambda b, i: (b, i, P_LORA // LORA_W))]
                 + [full(a) for a in args],
        out_specs=pl.BlockSpec((1, L, SSM_W), lambda b, i: (b, i, 0)),
        out_shape=jax.ShapeDtypeStruct((bsz, t, SSM_W), BF16),
        scratch_shapes=[pltpu.VMEM((L + 8, SSM_W), F32), pltpu.VMEM((L + 8, 512), F32),
                        pltpu.VMEM((SSM_HEADS // 2, SSM_STATE, LANES), F32)],
        compiler_params=_cparams(2),
        name="ssd",
    )(proj, proj, proj, proj, *args)


def _outproj_kernel(orw_ref, oss_ref, x_ref, g1_ref, nw_ref, sh_ref, sc_ref, wo1_ref, wo2_ref,
                    wq_ref, h1_ref, n2_ref, q_ref):
    mix = _dot(orw_ref[0], wo1_ref[...]) + _dot(oss_ref[0], wo2_ref[...])
    h1 = x_ref[0] + g1_ref[0] * mix
    h1_ref[0] = h1
    n2 = _rms_mod(h1, nw_ref[...], sh_ref[0], sc_ref[0]).astype(BF16)
    n2_ref[0] = n2
    q_ref[0] = _dot(n2, wq_ref[...])


def _out_proj(o_rw, o_ss, x, gate1, norm_w, shift, scale, w_out, w_query_p, tm):
    bsz, t, d = x.shape
    wo1 = w_out[:RW].astype(BF16)
    wo2 = w_out[RW:].astype(BF16)
    tok = lambda w: pl.BlockSpec((1, tm, w), lambda b, i: (b, i, 0))
    per_b = pl.BlockSpec((1, 1, d), lambda b, i: (b, 0, 0))
    const = lambda a: pl.BlockSpec(a.shape, lambda b, i: (0,) * a.ndim, pipeline_mode=pl.Buffered(1))
    nw = norm_w.reshape(1, d)
    return pl.pallas_call(
        _outproj_kernel,
        grid=(bsz, t // tm),
        in_specs=[tok(RW), tok(SSM_W), tok(d), per_b, const(nw), per_b, per_b,
                  const(wo1), const(wo2), const(w_query_p)],
        out_specs=[tok(d), tok(d), tok(d)],
        out_shape=[jax.ShapeDtypeStruct((bsz, t, d), F32), jax.ShapeDtypeStruct((bsz, t, d), BF16),
                   jax.ShapeDtypeStruct((bsz, t, d), F32)],
        compiler_params=_cparams(2),
        name="out_proj",
    )(o_rw, o_ss, x, gate1.reshape(bsz, 1, d), nw, shift.reshape(bsz, 1, d),
      scale.reshape(bsz, 1, d), wo1, wo2, w_query_p)


_CANDS = [(i, j) for i in range(PEER_TOPK) for j in range(PEER_TOPK) if (i + 1) * (j + 1) <= PEER_TOPK]


def _route_kernel(q_ref, k1_ref, k2_ref, cnt_ref, e1_ref, rk_ref, e2_ref):
    nh, hd, K = PEER_HEADS, PEER_HALF, PEER_TOPK
    q = q_ref[...]

    def top_vals(s, want_rank):
        vals, prev = [], None
        rank = jnp.zeros_like(s) if want_rank else None
        for _ in range(K):
            if prev is None:
                cand = s
            else:
                below = s < prev
                cand = jnp.where(below, s, -jnp.inf)
                if want_rank:
                    rank = jnp.where(below, rank + 1.0, rank)
            prev = jnp.max(cand, axis=0, keepdims=True)
            vals.append(prev)
        if want_rank:
            rank = jnp.where(s < prev, rank + 1.0, rank)
        return vals, rank

    s1 = [_dot_nt(k1_ref[h], q[:, h * hd:(h + 1) * hd]) for h in range(nh)]
    s2 = [_dot_nt(k2_ref[h], q[:, (nh + h) * hd:(nh + h + 1) * hd]) for h in range(nh)]
    a_h = [top_vals(s, False)[0] for s in s1]
    b_hr = [top_vals(s, True) for s in s2]
    b_h = [x[0] for x in b_hr]
    a = [jnp.concatenate([a_h[h][i] for h in range(nh)], axis=0) for i in range(K)]
    b = [jnp.concatenate([b_h[h][i] for h in range(nh)], axis=0) for i in range(K)]

    sums = {c: a[c[0]] + b[c[1]] for c in _CANDS}
    tau = None
    for _ in range(K):
        best = None
        for c in _CANDS:
            v = sums[c] if tau is None else jnp.where(sums[c] < tau, sums[c], -jnp.inf)
            best = v if best is None else jnp.maximum(best, v)
        tau = best
    ea = [jnp.exp(ai - a[0]) for ai in a]
    eb = [jnp.exp(bj - b[0]) for bj in b]
    zsum = jnp.zeros_like(tau)
    cnt = [jnp.zeros_like(tau) for _ in range(K)]
    for (i, j) in _CANDS:
        sel = sums[(i, j)] >= tau
        zsum = zsum + jnp.where(sel, ea[i] * eb[j], 0.0)
        cnt[i] = cnt[i] + jnp.where(sel, 1.0, 0.0)
    zinv = 1.0 / zsum
    for h in range(nh):
        row = slice(h, h + 1)
        cfull = jnp.zeros_like(s1[h])
        for i in range(K):
            cfull = jnp.where(s1[h] == a_h[h][i], cnt[i][row, :], cfull)
        cnt_ref[h] = cfull
        e1_ref[h] = jnp.exp(s1[h] - a_h[h][0])
        rk_ref[h] = b_hr[h][1].astype(BF16)
        e2_ref[h] = (jnp.exp(s2[h] - b_h[h][0]) * zinv[row, :]).astype(BF16)


def _peer_route(q, sub_keys, tm):
    ntok = q.shape[0]
    nk, nh = PEER_KEYS, PEER_HEADS
    k1 = sub_keys[:, 0].astype(BF16)
    k2 = sub_keys[:, 1].astype(BF16)
    blk = pl.BlockSpec((nh, nk, tm), lambda i: (0, 0, i))
    f32_out = jax.ShapeDtypeStruct((nh, nk, ntok), F32)
    bf16_out = jax.ShapeDtypeStruct((nh, nk, ntok), BF16)
    return pl.pallas_call(
        _route_kernel,
        grid=(ntok // tm,),
        in_specs=[pl.BlockSpec((tm, q.shape[1]), lambda i: (i, 0)),
                  pl.BlockSpec(k1.shape, lambda i: (0, 0, 0)),
                  pl.BlockSpec(k2.shape, lambda i: (0, 0, 0))],
        out_specs=[blk, blk, blk, blk],
        out_shape=[f32_out, f32_out, bf16_out, bf16_out],
        compiler_params=_cparams(1),
        name="peer_route",
    )(q, k1, k2)


def _dense_kernel(n2_ref, u_ref, vt_ref, cnt_ref, e1_ref, rk_ref, e2_ref, o_ref, w_scr):
    j = pl.program_id(1)
    nk = PEER_KEYS
    ec = u_ref.shape[0]
    tm = n2_ref.shape[0]
    n_piece = 4
    pc = ec // n_piece
    zero = jnp.zeros((), BF16)
    n2 = n2_ref[...]

    def weights(p, act):
        for s in range(pc // nk):
            sl = p * (pc // nk) + s
            gsum = jnp.zeros((nk, tm), BF16)
            for h in range(PEER_HEADS):
                cb = jnp.broadcast_to(cnt_ref[h, sl:sl + 1, :], (nk, tm)).astype(BF16)
                eb = jnp.broadcast_to(e1_ref[h, sl:sl + 1, :], (nk, tm)).astype(BF16)
                gsum = gsum + jnp.where(rk_ref[h] < cb, e2_ref[h], zero) * eb
            a = act[s * nk:(s + 1) * nk, :]
            gelu = 0.5 * a * (1.0 + lax.erf(a * (2.0 ** -0.5)))
            w_scr[sl * nk:(sl + 1) * nk, :] = gsum * gelu.astype(BF16)

    acts = [_dot_nt(u_ref[p * pc:(p + 1) * pc, :], n2) for p in range(n_piece)]
    for p in range(n_piece):
        weights(p, acts[p])
        part = _dot(vt_ref[:, p * pc:(p + 1) * pc], w_scr[p * pc:(p + 1) * pc, :])
        prev = jnp.where(j == 0, 0.0, o_ref[...]) if p == 0 else o_ref[...]
        o_ref[...] = prev + part


def _peer_dense(n2, u_bf, vt_bf, cnt, e1, rk, e2, tm, ec):
    ntok, d = n2.shape
    n_exp = u_bf.shape[0]
    ns = ec // PEER_KEYS
    rows = pl.BlockSpec((PEER_HEADS, ns, tm), lambda i, j: (0, j, i))
    slab = pl.BlockSpec((PEER_HEADS, PEER_KEYS, tm), lambda i, j: (0, 0, i))
    return pl.pallas_call(
        _dense_kernel,
        grid=(ntok // tm, n_exp // ec),
        in_specs=[pl.BlockSpec((tm, d), lambda i, j: (i, 0)),
                  pl.BlockSpec((ec, d), lambda i, j: (j, 0)),
                  pl.BlockSpec((d, ec), lambda i, j: (0, j)),
                  rows, rows, slab, slab],
        out_specs=pl.BlockSpec((d, tm), lambda i, j: (0, i)),
        out_shape=jax.ShapeDtypeStruct((d, ntok), F32),
        scratch_shapes=[pltpu.VMEM((ec, tm), BF16)],
        compiler_params=_cparams(2),
        name="peer_dense",
    )(n2, u_bf, vt_bf, cnt, e1, rk, e2)


def _final_kernel(h1_ref, pt_ref, g2_ref, nw_ref, o_ref):
    h = h1_ref[0] + g2_ref[0] * pt_ref[...].T
    ms = jnp.mean(h * h, axis=-1, keepdims=True)
    o_ref[0] = (h * lax.rsqrt(ms + RMS_EPS) * nw_ref[...]).astype(o_ref.dtype)


def _final(h1, peer_t, gate2, norm_w, out_dtype, tm):
    bsz, t, d = h1.shape
    nt = t // tm
    return pl.pallas_call(
        _final_kernel,
        grid=(bsz, nt),
        in_specs=[pl.BlockSpec((1, tm, d), lambda b, i: (b, i, 0)),
                  pl.BlockSpec((d, tm), lambda b, i: (0, b * nt + i)),
                  pl.BlockSpec((1, 1, d), lambda b, i: (b, 0, 0)),
                  pl.BlockSpec((1, d), lambda b, i: (0, 0))],
        out_specs=pl.BlockSpec((1, tm, d), lambda b, i: (b, i, 0)),
        out_shape=jax.ShapeDtypeStruct((bsz, t, d), out_dtype),
        compiler_params=_cparams(2),
        name="final_norm",
    )(h1, peer_t, gate2.reshape(bsz, 1, d), norm_w.reshape(1, d))


def _pack_in_proj(w_in, mu):
    d = w_in.shape[0]
    rp = 3 * RW + 64 + 64 + 160
    zc = lambda n: jnp.zeros((d, n), w_in.dtype)
    z0 = rp
    dt0 = rp + SSM_W + SSM_W + 512
    w = jnp.concatenate([
        w_in[:, :3 * RW],
        w_in[:, z0:z0 + SSM_W],
        w_in[:, z0 + SSM_W:z0 + 2 * SSM_W],
        w_in[:, z0 + 2 * SSM_W:z0 + 2 * SSM_W + 512],
        w_in[:, 3 * RW:rp], zc(LORA_DT - 288),
        w_in[:, dt0:dt0 + SSM_HEADS], zc(LORA_W - LORA_DT - SSM_HEADS)], axis=1)
    m = jnp.concatenate([mu[:3 * RW], jnp.zeros((P_LORA - 3 * RW,), mu.dtype), mu[3 * RW:rp],
                         jnp.zeros((LORA_W - 288,), mu.dtype)])
    return w.astype(BF16), m.reshape(1, P_TOT)


def _tile(t, pref):
    return pref if t % pref == 0 else t


def kernel(x, c, ada_w, ada_b, norm1_w, w_in, rwkv_mu, rwkv_w0, rwkv_w_up, rwkv_a0, rwkv_a_up, rwkv_g_up, rwkv_k_k, rwkv_k_a, rwkv_r_k, rwkv_ln_w, rwkv_ln_b, mamba_conv_w, mamba_conv_b, mamba_dt_bias, mamba_a_log, mamba_d, mamba_norm_w, w_out, norm2_w, peer_w_query, peer_sub_keys, peer_u, peer_v, final_norm_w):
    out_dtype = x.dtype
    bsz, t, d = x.shape
    ntok = bsz * t
    assert ada_w.shape[0] == 1, "single-layer trunk"
    h = x.astype(F32)
    mod = _ada_mod(c.astype(F32), ada_w[0], ada_b[0])
    shift1, scale1, gate1, shift2, scale2, gate2 = jnp.split(mod, 6, axis=-1)
    w_packed, mu_packed = _pack_in_proj(w_in[0], rwkv_mu[0])
    proj = _in_proj(h, norm1_w[0], shift1, scale1, w_packed, mu_packed, _tile(t, 1024), 768)
    o_rw = _rwkv(proj, rwkv_w0[0], rwkv_a0[0], rwkv_k_k[0], rwkv_k_a[0], rwkv_r_k[0].reshape(-1),
                 rwkv_ln_w[0], rwkv_ln_b[0], rwkv_w_up[0], rwkv_a_up[0], rwkv_g_up[0],
                 _tile(t, 256))
    o_ss = _ssd(proj, mamba_conv_w[0], mamba_conv_b[0], mamba_dt_bias[0], mamba_a_log[0],
                mamba_d[0], mamba_norm_w[0])
    wq = peer_w_query[0].reshape(d, PEER_HEADS, 2, PEER_HALF).transpose(0, 2, 1, 3)
    wq = wq.reshape(d, 2 * PEER_HEADS * PEER_HALF).astype(BF16)
    h1, n2, q = _out_proj(o_rw, o_ss, h, gate1, norm2_w[0], shift2, scale2, w_out[0], wq,
                          _tile(t, 256))
    cnt, e1, rk, e2 = _peer_route(q.reshape(ntok, -1), peer_sub_keys[0], _tile(ntok, 256))
    peer_t = _peer_dense(n2.reshape(ntok, d), peer_u[0].astype(BF16), peer_v[0].T.astype(BF16),
                         cnt, e1, rk, e2, _tile(ntok, 512), 1024)
    return _final(h1, peer_t, gate2, final_norm_w, out_dtype, _tile(t, 512))
```

```python
import math

import jax
import jax.numpy as jnp
from jax import lax
from jax.experimental import pallas as pl
from jax.experimental.pallas import tpu as pltpu

F32 = jnp.float32
BF16 = jnp.bfloat16

D_MODEL = 2048
RW = 1024
RW_HEAD = 64
SSM_W = 1024
SSM_HEAD = 64
SSM_HEADS = 16
SSM_STATE = 128
SSM_CONV = 4
PEER_HEADS = 8
PEER_KEYS = 128
PEER_HALF = 128
PEER_TOPK = 16
RMS_EPS = 1e-6
GN_EPS = 64e-5
SSM_EPS = 1e-5

P_R, P_K, P_V, P_Z, P_XS, P_BC, P_LORA = 0, 1024, 2048, 3072, 4096, 5120, 5632
P_TOT = 6144
LORA_W = 512
LORA_DT = 384

RWKV_CHUNK = 64
SSD_CHUNK = 128
LANES = 128

VMEM_LIMIT = 56 * 1024 * 1024


def _cparams(n_axes):
    return pltpu.CompilerParams(
        dimension_semantics=("arbitrary",) * n_axes, vmem_limit_bytes=VMEM_LIMIT)


def _split3(x):
    h1 = x.astype(BF16)
    r1 = x - h1.astype(F32)
    h2 = r1.astype(BF16)
    r2 = r1 - h2.astype(F32)
    return h1, h2, r2.astype(BF16)


def _mm(a, b, dims):
    return lax.dot_general(a.astype(BF16), b.astype(BF16), (dims, ((), ())),
                           preferred_element_type=F32)


def _dot(a, b):
    return _mm(a, b, ((1,), (0,)))


def _dot_nt(a, b):
    return _mm(a, b, ((1,), (1,)))


def _dot_tn(a, b):
    return _mm(a, b, ((0,), (0,)))


def _dot_xe(x, e_bf16):
    h1, h2, h3 = _split3(x)
    return _dot(h1, e_bf16) + _dot(h2, e_bf16) + _dot(h3, e_bf16)


def _dot_ex(e_bf16, x):
    h1, h2, h3 = _split3(x)
    return _dot(e_bf16, h1) + _dot(e_bf16, h2) + _dot(e_bf16, h3)


def _dot_ex2(e_bf16, x):
    hi = x.astype(BF16)
    lo = (x - hi.astype(F32)).astype(BF16)
    return _dot(e_bf16, hi) + _dot(e_bf16, lo)


def _dot_hi(a, b):
    ah = a.astype(BF16)
    al = (a - ah.astype(F32)).astype(BF16)
    bh = b.astype(BF16)
    bl = (b - bh.astype(F32)).astype(BF16)
    return _dot(ah, bh) + _dot(al, bh) + _dot(ah, bl)


def _silu(x):
    return x * jax.nn.sigmoid(x)


def _ada_kernel(cb_ref, w_ref, b_ref, o_ref):
    w = w_ref[...]
    reps = w.shape[1] // LANES
    rows = []
    for b in range(cb_ref.shape[0]):
        cb = _silu(cb_ref[b])
        rows.append(jnp.sum(w * jnp.concatenate([cb] * reps, axis=1), axis=0, keepdims=True))
    o_ref[...] = jnp.concatenate(rows, axis=0) + b_ref[...]


def _ada_mod(c, ada_w, ada_b):
    bsz, d = c.shape
    n = ada_w.shape[1]
    tn = 1024
    cb = jnp.broadcast_to(c[:, :, None], (bsz, d, LANES))
    return pl.pallas_call(
        _ada_kernel,
        grid=(n // tn,),
        in_specs=[pl.BlockSpec((bsz, d, LANES), lambda j: (0, 0, 0)),
                  pl.BlockSpec((d, tn), lambda j: (0, j)),
                  pl.BlockSpec((1, tn), lambda j: (0, j))],
        out_specs=pl.BlockSpec((bsz, tn), lambda j: (0, j)),
        out_shape=jax.ShapeDtypeStruct((bsz, n), F32),
        compiler_params=_cparams(1),
        name="ada_mod",
    )(cb, ada_w, ada_b.reshape(1, n))


def _rms_mod(x, w, shift, scale):
    ms = jnp.mean(x * x, axis=-1, keepdims=True)
    return (x * lax.rsqrt(ms + RMS_EPS) * w) * (1.0 + scale) + shift


def _inproj_kernel(x_ref, nw_ref, sh_ref, sc_ref, w_ref, mu_ref, o_ref, n_scr, carry_scr):
    i = pl.program_id(1)
    j = pl.program_id(2)

    @pl.when(j == 0)
    def _():
        n_scr[...] = _rms_mod(x_ref[0], nw_ref[...], sh_ref[0], sc_ref[0]).astype(BF16)

    n = n_scr[...]
    tm = n.shape[0]
    pw = 256
    first = i == 0
    for c in range(w_ref.shape[1] // pw):
        cs = slice(c * pw, (c + 1) * pw)
        p = _dot(n, w_ref[:, cs])
        prev_last = jnp.where(first, 0.0, carry_scr[j, 7:8, cs])
        row = lax.broadcasted_iota(jnp.int32, p.shape, 0)
        shifted = jnp.where(row == 0, prev_last, pltpu.roll(p, 1, 0))
        carry_scr[j, :, cs] = p[tm - 8:tm, :]
        o_ref[0, :, cs] = p + (shifted - p) * mu_ref[:, cs]


def _in_proj(x, norm_w, shift, scale, w_packed, mu_packed, tm, tn):
    bsz, t, d = x.shape
    nb = P_TOT // tn
    return pl.pallas_call(
        _inproj_kernel,
        grid=(bsz, t // tm, nb),
        in_specs=[pl.BlockSpec((1, tm, d), lambda b, i, j: (b, i, 0)),
                  pl.BlockSpec((1, d), lambda b, i, j: (0, 0)),
                  pl.BlockSpec((1, 1, d), lambda b, i, j: (b, 0, 0)),
                  pl.BlockSpec((1, 1, d), lambda b, i, j: (b, 0, 0)),
                  pl.BlockSpec((d, tn), lambda b, i, j: (0, j)),
                  pl.BlockSpec((1, tn), lambda b, i, j: (0, j))],
        out_specs=pl.BlockSpec((1, tm, tn), lambda b, i, j: (b, i, j)),
        out_shape=jax.ShapeDtypeStruct((bsz, t, P_TOT), F32),
        scratch_shapes=[pltpu.VMEM((tm, d), BF16), pltpu.VMEM((nb, 8, tn), F32)],
        compiler_params=_cparams(3),
        name="in_proj",
    )(x, norm_w.reshape(1, d), shift.reshape(bsz, 1, d), scale.reshape(bsz, 1, d),
      w_packed, mu_packed)


RWKV_PAIRS = 8
RWKV_A_CHUNKS = 2


def _rwkv_kernel(r_ref, k_ref, v_ref, lora_ref, w0_ref, a0_ref, kk_ref, ka_ref, rk_ref,
                 lnw_ref, lnb_ref, wup_ref, aup_ref, gup_ref, o_ref,
                 s_scr, kkn_scr, k2_scr, beta_scr, logw_scr, cum_scr, o_scr,
                 tinv_scr, arb_scr, vs_scr, lhs_scr, upd_scr, lkv_scr, arkv_scr):
    C = RWKV_CHUNK
    tc = r_ref.shape[1]
    n_chunks = tc // C
    n_pairs = r_ref.shape[2] // LANES

    @pl.when(pl.program_id(2) == 0)
    def _():
        s_scr[...] = jnp.zeros_like(s_scr)

    lane = lax.broadcasted_iota(jnp.int32, (1, LANES), 1)
    first = lane < RW_HEAD
    m0 = first.astype(F32)
    m1 = 1.0 - m0
    row = lax.broadcasted_iota(jnp.int32, (LANES, LANES), 0)
    col = lax.broadcasted_iota(jnp.int32, (LANES, LANES), 1)
    strict = row > col
    incl = row >= col
    blk16 = (row // 16) == (col // 16)
    eye = jnp.where(row == col, 1.0, 0.0)
    r64 = lax.broadcasted_iota(jnp.int32, (C, C), 0)
    c64 = lax.broadcasted_iota(jnp.int32, (C, C), 1)
    tri64 = jnp.where(r64 >= c64, 1.0, 0.0).astype(BF16)

    def per_head(x):
        cols = []
        for p in range(n_pairs):
            xp = x[:, p * LANES:(p + 1) * LANES]
            s0 = jnp.sum(xp * m0, axis=-1, keepdims=True)
            s1 = jnp.sum(xp * m1, axis=-1, keepdims=True)
            cols.append(jnp.where(first, s0, s1))
        return jnp.concatenate(cols, axis=1)

    lora = lora_ref[0]
    wa = lora[:, 0:128]
    w_lin = w0_ref[...] + _dot_hi(jnp.tanh(wa), wup_ref[...])
    logw = -math.exp(-0.5) / (1.0 + jnp.exp(-w_lin))
    a = jax.nn.sigmoid(a0_ref[...] + _dot_hi(wa, aup_ref[...]))
    k = k_ref[0]
    kk = k * kk_ref[...]
    kkn = kk / jnp.maximum(jnp.sqrt(per_head(kk * kk)), 1e-12)
    kkn_scr[...] = kkn
    k2_scr[...] = k * (1.0 + (a - 1.0) * ka_ref[...])
    beta_scr[...] = kkn * a
    logw_scr[...] = logw
    for c in range(n_chunks):
        cum_scr[c * C:(c + 1) * C, :] = _dot_ex2(tri64, logw[c * C:(c + 1) * C, :])

    def stack(x):
        return jnp.concatenate([x * m0, x * m1], axis=0)

    lns = [slice(p * LANES, (p + 1) * LANES) for p in range(n_pairs)]

    def indep(it, carry):
        chains = [(it * RWKV_A_CHUNKS + u, p) for u in range(RWKV_A_CHUNKS) for p in range(n_pairs)]
        sls = [pl.ds(pl.multiple_of(c * C, C), C) for c, _ in chains]
        lnc = [lns[p] for _, p in chains]
        cum = [cum_scr[sl, ln] for sl, ln in zip(sls, lnc)]
        ig = [jnp.exp(-c) for c in cum]
        d_end = [jnp.exp(c[C - 1:C, :] - c) for c in cum]
        k2 = [k2_scr[sl, ln] for sl, ln in zip(sls, lnc)]
        beta = [beta_scr[sl, ln] for sl, ln in zip(sls, lnc)]
        v_s = [stack(v_ref[0, sl, ln]).astype(BF16) for sl, ln in zip(sls, lnc)]
        rt_s = [stack(r_ref[0, sl, ln] * jnp.exp(c)) for sl, ln, c in zip(sls, lnc, cum)]
        kap_s = [stack(kkn_scr[sl, ln] * jnp.exp(c - logw_scr[sl, ln]))
                 for sl, ln, c in zip(sls, lnc, cum)]
        bt_s = [stack(b * g) for b, g in zip(beta, ig)]
        kt_s = [stack(k * g) for k, g in zip(k2, ig)]
        lhs = [jnp.concatenate([a, b], axis=0).astype(BF16) for a, b in zip(kap_s, rt_s)]
        for (c, p), k, b, d, l, vs in zip(chains, k2, beta, d_end, lhs, v_s):
            upd_scr[c, p] = jnp.concatenate([stack(k * d), -stack(b * d)], axis=0).astype(BF16)
            lhs_scr[c, p] = l
            vs_scr[c, p] = vs
        gram = [_dot_nt(l, jnp.concatenate([b, k], axis=0)) for l, b, k in zip(lhs, bt_s, kt_s)]
        gb = [g[:, :2 * C] for g in gram]
        gk = [g[:, 2 * C:] for g in gram]
        lb = [jnp.where(strict, g[:2 * C], 0.0) for g in gb]
        lk = [jnp.where(strict, g[:2 * C], 0.0) for g in gk]
        ark = [jnp.where(incl, g[2 * C:], 0.0) for g in gk]
        for (c, p), g in zip(chains, gb):
            arb_scr[c, p] = jnp.where(incl, g[2 * C:], 0.0).astype(BF16)

        ld = [jnp.where(blk16, x, 0.0) for x in lb]
        off = [x - d for x, d in zip(lb, ld)]
        m = [-d for d in ld]
        m2 = [_dot(a, a) for a in m]
        m34 = [_dot(jnp.concatenate([a, b], axis=0), b) for a, b in zip(m, m2)]
        m3 = [x[:2 * C] for x in m34]
        m4 = [x[2 * C:] for x in m34]
        lav = [_dot(jnp.concatenate([a, b], axis=0), v) for a, b, v in zip(lk, ark, v_s)]
        lkv = [x[:2 * C] for x in lav]
        arkv = [x[2 * C:] for x in lav]
        m8 = [_dot(a, a) for a in m4]
        m12 = [_dot(a, b) for a, b in zip(m4, m8)]
        dinv = [_dot(eye + a + b + c, eye + d + e + f)
                for a, b, c, d, e, f in zip(m, m2, m3, m4, m8, m12)]
        n1 = [_dot(d, o) for d, o in zip(dinv, off)]
        n2 = [_dot(a, a) for a in n1]
        x1 = [d + _dot(a, d) for d, a in zip(dinv, n2)]
        for (c, p), x, a, lv, av in zip(chains, x1, n1, lkv, arkv):
            tinv_scr[c, p] = (x - _dot(a, x)).astype(BF16)
            lkv_scr[c, p] = lv
            arkv_scr[c, p] = av
        return carry

    lax.fori_loop(0, n_chunks // RWKV_A_CHUNKS, indep, 0)

    def recur(ci, carry):
        sl = pl.ds(pl.multiple_of(ci * C, C), C)
        P = range(n_pairs)
        s = [s_scr[p] for p in P]
        ks = [_dot_nt(lhs_scr[ci, p], s[p]) for p in P]
        y = [_dot(tinv_scr[ci, p], ks[p][:2 * C] + lkv_scr[ci, p]) for p in P]
        sn = [_dot_tn(jnp.concatenate([vs_scr[ci, p], y[p].astype(BF16)], axis=0), upd_scr[ci, p])
              for p in P]
        for p in P:
            s_scr[p] = s[p] * jnp.exp(cum_scr[sl, lns[p]][C - 1:C, :]) + sn[p]
        arby = [_dot(arb_scr[ci, p], y[p]) for p in P]
        for p in P:
            o_s = ks[p][2 * C:] + arkv_scr[ci, p] - arby[p]
            o_scr[sl, lns[p]] = o_s[:C] + o_s[C:]
        return carry

    lax.fori_loop(0, n_chunks, recur, 0)

    o = o_scr[...]
    mean = per_head(o) * (1.0 / RW_HEAD)
    dlt = o - mean
    var = per_head(dlt * dlt) * (1.0 / RW_HEAD)
    on = dlt * lax.rsqrt(var + GN_EPS) * lnw_ref[...] + lnb_ref[...]
    bonus = per_head(r_ref[0] * k2_scr[...] * rk_ref[...]) * v_ref[0]
    g = _dot_hi(jax.nn.sigmoid(lora[:, 128:384]), gup_ref[...])
    o_ref[0] = ((on + bonus) * g).astype(o_ref.dtype)


def _rwkv(proj, w0, a0, k_k, k_a, r_k, ln_w, ln_b, w_up, a_up, g_up, tc):
    bsz, t, _ = proj.shape
    wl = RWKV_PAIRS * LANES
    row = lambda p: p.reshape(1, RW)
    wup_p = jnp.concatenate([w_up, jnp.zeros_like(a_up)], axis=0)
    aup_p = jnp.concatenate([jnp.zeros_like(w_up), a_up], axis=0)
    gup_p = jnp.concatenate([g_up, jnp.zeros((256 - g_up.shape[0], RW), F32)], axis=0)
    pspec = lambda off: pl.BlockSpec((1, tc, wl), lambda b, h, i: (b, i, off // wl + h))
    vec = pl.BlockSpec((1, wl), lambda b, h, i: (0, h))
    tile = pltpu.VMEM((tc, wl), F32)
    per_chain = lambda rows, dt: pltpu.VMEM((tc // RWKV_CHUNK, RWKV_PAIRS, rows, LANES), dt)
    return pl.pallas_call(
        _rwkv_kernel,
        grid=(bsz, RW // wl, t // tc),
        in_specs=[pspec(P_R), pspec(P_K), pspec(P_V),
                  pl.BlockSpec((1, tc, LORA_W), lambda b, h, i: (b, i, P_LORA // LORA_W)),
                  vec, vec, vec, vec, vec, vec, vec,
                  pl.BlockSpec((128, wl), lambda b, h, i: (0, h)),
                  pl.BlockSpec((128, wl), lambda b, h, i: (0, h)),
                  pl.BlockSpec((256, wl), lambda b, h, i: (0, h))],
        out_specs=pl.BlockSpec((1, tc, wl), lambda b, h, i: (b, i, h)),
        out_shape=jax.ShapeDtypeStruct((bsz, t, RW), BF16),
        scratch_shapes=[pltpu.VMEM((RWKV_PAIRS, LANES, LANES), F32),
                        tile, tile, tile, tile, tile, tile,
                        per_chain(LANES, BF16), per_chain(LANES, BF16), per_chain(LANES, BF16),
                        per_chain(2 * LANES, BF16), per_chain(2 * LANES, BF16),
                        per_chain(LANES, F32), per_chain(LANES, F32)],
        compiler_params=_cparams(3),
        name="rwkv",
    )(proj, proj, proj, proj, row(w0), row(a0), row(k_k), row(k_a), row(r_k), row(ln_w),
      row(ln_b), wup_p, aup_p, gup_p)


def _ssd_kernel(z_ref, x_ref, bc_ref, lora_ref, cwx_ref, cbx_ref, cwb_ref, cbb_ref, dtb_ref,
                alog_ref, dsk_ref, nw_ref, ex_ref, o_ref, extx, extb, st_scr):
    L = SSD_CHUNK

    @pl.when(pl.program_id(1) == 0)
    def _():
        extx[0:8, :] = jnp.zeros((8, extx.shape[1]), F32)
        extb[0:8, :] = jnp.zeros((8, extb.shape[1]), F32)
        st_scr[...] = jnp.zeros_like(st_scr)

    def conv(ext, cur, w_ref, b_ref):
        ext[8:8 + L, :] = cur
        acc = b_ref[...] + w_ref[0:1, :] * ext[5:5 + L, :]
        for j in range(1, SSM_CONV):
            acc = acc + w_ref[j:j + 1, :] * ext[5 + j:5 + j + L, :]
        ext[0:8, :] = ext[L:L + 8, :]
        return _silu(acc)

    xs = conv(extx, x_ref[0], cwx_ref, cbx_ref)
    bc = conv(extb, bc_ref[0], cwb_ref, cbb_ref)

    row = lax.broadcasted_iota(jnp.int32, (L, L), 0)
    col = lax.broadcasted_iota(jnp.int32, (L, L), 1)
    incl = row >= col
    tri = jnp.where(incl, 1.0, 0.0).astype(BF16)
    lane = lax.broadcasted_iota(jnp.int32, (1, LANES), 1)
    m_par = [(lane < SSM_HEAD).astype(F32), (lane >= SSM_HEAD).astype(F32)]

    ex = ex_ref[...]
    dts = jax.nn.softplus(lora_ref[0][:, LORA_DT:LORA_DT + LANES] + dtb_ref[...])
    a_slab = dts * (-jnp.exp(alog_ref[...]))
    acs_slab = _dot_ex(tri, a_slab)
    acs_t = acs_slab.T
    dtx = _dot_xe(dts, ex)
    acs_x = _dot_xe(acs_slab, ex)
    last = acs_x[L - 1:L, :]
    xdt = xs * dtx
    xdte = xdt * jnp.exp(last - acs_x)
    eacs = jnp.exp(acs_x)
    elast = jnp.exp(last)

    ys = []
    for j in range(SSM_HEADS // 2):
        g = j // (SSM_HEADS // 4)
        bm = bc[:, g * LANES:(g + 1) * LANES]
        cm = bc[:, (2 + g) * LANES:(3 + g) * LANES]
        pr = slice(j * LANES, (j + 1) * LANES)
        cb = _dot_nt(cm, bm)
        st = st_scr[j]
        y = eacs[:, pr] * _dot(cm, st)
        xp = xdt[:, pr]
        for par in range(2):
            h = 2 * j + par
            diff = acs_slab[:, h:h + 1] - acs_t[h:h + 1, :]
            mh = jnp.exp(jnp.where(incl, diff, -jnp.inf)) * cb
            y = y + _dot(mh, xp * m_par[par])
        st_scr[j] = st * elast[:, pr] + _dot_tn(bm, xdte[:, pr])
        ys.append(y)
    y = jnp.concatenate(ys, axis=1) + xs * dsk_ref[...]
    y = y * _silu(z_ref[0])
    half = SSM_W // 2
    outs = []
    for g in range(2):
        seg = y[:, g * half:(g + 1) * half]
        outs.append(seg * lax.rsqrt(jnp.mean(seg * seg, axis=-1, keepdims=True) + SSM_EPS))
    o_ref[0] = (jnp.concatenate(outs, axis=1) * nw_ref[...]).astype(o_ref.dtype)


def _ssd(proj, conv_w, conv_b, dt_bias, a_log, d_skip, norm_w):
    bsz, t, _ = proj.shape
    L = SSD_CHUNK
    pad_slab = lambda p: jnp.concatenate([p, jnp.zeros((LANES - SSM_HEADS,), F32)]).reshape(1, LANES)
    hh = jnp.arange(LANES)[:, None]
    ll = jnp.arange(SSM_W)[None, :]
    ex = (hh == ll // SSM_HEAD).astype(BF16)
    cwx, cwb = conv_w[:, :SSM_W], conv_w[:, SSM_W:]
    cbx, cbb = conv_b[:SSM_W].reshape(1, -1), conv_b[SSM_W:].reshape(1, -1)
    full = lambda a: pl.BlockSpec(a.shape, lambda b, i: (0,) * a.ndim)
    args = [cwx, cbx, cwb, cbb, pad_slab(dt_bias), pad_slab(a_log),
            jnp.repeat(d_skip, SSM_HEAD).reshape(1, SSM_W), norm_w.reshape(1, SSM_W), ex]
    return pl.pallas_call(
        _ssd_kernel,
        grid=(bsz, t // L),
        in_specs=[pl.BlockSpec((1, L, SSM_W), lambda b, i: (b, i, P_Z // SSM_W)),
                  pl.BlockSpec((1, L, SSM_W), lambda b, i: (b, i, P_XS // SSM_W)),
                  pl.BlockSpec((1, L, 512), lambda b, i: (b, i, P_BC // 512)),
                  pl.BlockSpec((1, L, LORA_W), lambda b, i: (b, i, P_LORA // LORA_W))]
                 + [full(a) for a in args],
        out_specs=pl.BlockSpec((1, L, SSM_W), lambda b, i: (b, i, 0)),
        out_shape=jax.ShapeDtypeStruct((bsz, t, SSM_W), BF16),
        scratch_shapes=[pltpu.VMEM((L + 8, SSM_W), F32), pltpu.VMEM((L + 8, 512), F32),
                        pltpu.VMEM((SSM_HEADS // 2, SSM_STATE, LANES), F32)],
        compiler_params=_cparams(2),
        name="ssd",
    )(proj, proj, proj, proj, *args)


def _outproj_kernel(orw_ref, oss_ref, x_ref, g1_ref, nw_ref, sh_ref, sc_ref, wo1_ref, wo2_ref,
                    wq_ref, h1_ref, n2_ref, q_ref):
    mix = _dot(orw_ref[0], wo1_ref[...]) + _dot(oss_ref[0], wo2_ref[...])
    h1 = x_ref[0] + g1_ref[0] * mix
    h1_ref[0] = h1
    n2 = _rms_mod(h1, nw_ref[...], sh_ref[0], sc_ref[0]).astype(BF16)
    n2_ref[0] = n2
    q_ref[0] = _dot(n2, wq_ref[...]).astype(q_ref.dtype)


def _out_proj(o_rw, o_ss, x, gate1, norm_w, shift, scale, w_out, w_query_p, tm):
    bsz, t, d = x.shape
    wo1 = w_out[:RW].astype(BF16)
    wo2 = w_out[RW:].astype(BF16)
    tok = lambda w: pl.BlockSpec((1, tm, w), lambda b, i: (b, i, 0))
    per_b = pl.BlockSpec((1, 1, d), lambda b, i: (b, 0, 0))
    const = lambda a: pl.BlockSpec(a.shape, lambda b, i: (0,) * a.ndim, pipeline_mode=pl.Buffered(1))
    nw = norm_w.reshape(1, d)
    return pl.pallas_call(
        _outproj_kernel,
        grid=(bsz, t // tm),
        in_specs=[tok(RW), tok(SSM_W), tok(d), per_b, const(nw), per_b, per_b,
                  const(wo1), const(wo2), const(w_query_p)],
        out_specs=[tok(d), tok(d), tok(d)],
        out_shape=[jax.ShapeDtypeStruct((bsz, t, d), F32), jax.ShapeDtypeStruct((bsz, t, d), BF16),
                   jax.ShapeDtypeStruct((bsz, t, d), BF16)],
        compiler_params=_cparams(2),
        name="out_proj",
    )(o_rw, o_ss, x, gate1.reshape(bsz, 1, d), nw, shift.reshape(bsz, 1, d),
      scale.reshape(bsz, 1, d), wo1, wo2, w_query_p)


_CANDS = [(i, j) for i in range(PEER_TOPK) for j in range(PEER_TOPK) if (i + 1) * (j + 1) <= PEER_TOPK]


def _route_kernel(q_ref, k1_ref, k2_ref, cnt_ref, e1_ref, rk_ref, e2_ref):
    nh, hd, K = PEER_HEADS, PEER_HALF, PEER_TOPK
    q = q_ref[...]

    def top_vals(s, want_rank):
        vals, prev = [], None
        rank = jnp.zeros_like(s) if want_rank else None
        for _ in range(K):
            if prev is None:
                cand = s
            else:
                below = s < prev
                cand = jnp.where(below, s, -jnp.inf)
                if want_rank:
                    rank = jnp.where(below, rank + 1.0, rank)
            prev = jnp.max(cand, axis=0, keepdims=True)
            vals.append(prev)
        if want_rank:
            rank = jnp.where(s < prev, rank + 1.0, rank)
        return vals, rank

    s1 = [_dot_nt(k1_ref[h], q[:, h * hd:(h + 1) * hd]) for h in range(nh)]
    s2 = [_dot_nt(k2_ref[h], q[:, (nh + h) * hd:(nh + h + 1) * hd]) for h in range(nh)]
    a_h = [top_vals(s, False)[0] for s in s1]
    b_hr = [top_vals(s, True) for s in s2]
    b_h = [x[0] for x in b_hr]
    a = [jnp.concatenate([a_h[h][i] for h in range(nh)], axis=0) for i in range(K)]
    b = [jnp.concatenate([b_h[h][i] for h in range(nh)], axis=0) for i in range(K)]

    sums = {c: a[c[0]] + b[c[1]] for c in _CANDS}
    tau = None
    for _ in range(K):
        best = None
        for c in _CANDS:
            v = sums[c] if tau is None else jnp.where(sums[c] < tau, sums[c], -jnp.inf)
            best = v if best is None else jnp.maximum(best, v)
        tau = best
    ea = [jnp.exp(ai - a[0]) for ai in a]
    eb = [jnp.exp(bj - b[0]) for bj in b]
    zsum = jnp.zeros_like(tau)
    cnt = [jnp.zeros_like(tau) for _ in range(K)]
    for (i, j) in _CANDS:
        sel = sums[(i, j)] >= tau
        zsum = zsum + jnp.where(sel, ea[i] * eb[j], 0.0)
        cnt[i] = cnt[i] + jnp.where(sel, 1.0, 0.0)
    zinv = 1.0 / zsum
    for h in range(nh):
        row = slice(h, h + 1)
        cfull = jnp.zeros_like(s1[h])
        for i in range(K):
            cfull = jnp.where(s1[h] == a_h[h][i], cnt[i][row, :], cfull)
        cnt_ref[h] = cfull
        e1_ref[h] = jnp.exp(s1[h] - a_h[h][0])
        rk_ref[h] = b_hr[h][1].astype(BF16)
        e2_ref[h] = (jnp.exp(s2[h] - b_h[h][0]) * zinv[row, :]).astype(BF16)


def _peer_route(q, sub_keys, tm):
    ntok = q.shape[0]
    nk, nh = PEER_KEYS, PEER_HEADS
    k1 = sub_keys[:, 0].astype(BF16)
    k2 = sub_keys[:, 1].astype(BF16)
    blk = pl.BlockSpec((nh, nk, tm), lambda i: (0, 0, i))
    f32_out = jax.ShapeDtypeStruct((nh, nk, ntok), F32)
    bf16_out = jax.ShapeDtypeStruct((nh, nk, ntok), BF16)
    return pl.pallas_call(
        _route_kernel,
        grid=(ntok // tm,),
        in_specs=[pl.BlockSpec((tm, q.shape[1]), lambda i: (i, 0)),
                  pl.BlockSpec(k1.shape, lambda i: (0, 0, 0)),
                  pl.BlockSpec(k2.shape, lambda i: (0, 0, 0))],
        out_specs=[blk, blk, blk, blk],
        out_shape=[f32_out, f32_out, bf16_out, bf16_out],
        compiler_params=_cparams(1),
        name="peer_route",
    )(q, k1, k2)


def _dense_kernel(n2_ref, u_ref, vt_ref, cnt_ref, e1_ref, rk_ref, e2_ref, o_ref, w_scr):
    j = pl.program_id(1)
    nk = PEER_KEYS
    ec = u_ref.shape[0]
    tm = n2_ref.shape[0]
    n_piece = 4
    pc = ec // n_piece
    zero = jnp.zeros((), BF16)
    n2 = n2_ref[...]

    def weights(p, act):
        for s in range(pc // nk):
            sl = p * (pc // nk) + s
            gsum = jnp.zeros((nk, tm), BF16)
            for h in range(PEER_HEADS):
                cb = jnp.broadcast_to(cnt_ref[h, sl:sl + 1, :], (nk, tm)).astype(BF16)
                eb = jnp.broadcast_to(e1_ref[h, sl:sl + 1, :], (nk, tm)).astype(BF16)
                gsum = gsum + jnp.where(rk_ref[h] < cb, e2_ref[h], zero) * eb
            a = act[s * nk:(s + 1) * nk, :]
            gelu = 0.5 * a * (1.0 + lax.erf(a * (2.0 ** -0.5)))
            w_scr[sl * nk:(sl + 1) * nk, :] = gsum * gelu.astype(BF16)

    acts = [_dot_nt(u_ref[p * pc:(p + 1) * pc, :], n2) for p in range(n_piece)]
    for p in range(n_piece):
        weights(p, acts[p])
        part = _dot(vt_ref[:, p * pc:(p + 1) * pc], w_scr[p * pc:(p + 1) * pc, :])
        prev = jnp.where(j == 0, 0.0, o_ref[...]) if p == 0 else o_ref[...]
        o_ref[...] = prev + part


def _peer_dense(n2, u_bf, vt_bf, cnt, e1, rk, e2, tm, ec):
    ntok, d = n2.shape
    n_exp = u_bf.shape[0]
    ns = ec // PEER_KEYS
    rows = pl.BlockSpec((PEER_HEADS, ns, tm), lambda i, j: (0, j, i))
    slab = pl.BlockSpec((PEER_HEADS, PEER_KEYS, tm), lambda i, j: (0, 0, i))
    return pl.pallas_call(
        _dense_kernel,
        grid=(ntok // tm, n_exp // ec),
        in_specs=[pl.BlockSpec((tm, d), lambda i, j: (i, 0)),
                  pl.BlockSpec((ec, d), lambda i, j: (j, 0)),
                  pl.BlockSpec((d, ec), lambda i, j: (0, j)),
                  rows, rows, slab, slab],
        out_specs=pl.BlockSpec((d, tm), lambda i, j: (0, i)),
        out_shape=jax.ShapeDtypeStruct((d, ntok), F32),
        scratch_shapes=[pltpu.VMEM((ec, tm), BF16)],
        compiler_params=_cparams(2),
        name="peer_dense",
    )(n2, u_bf, vt_bf, cnt, e1, rk, e2)


def _final_kernel(h1_ref, pt_ref, g2_ref, nw_ref, o_ref):
    h = h1_ref[0] + g2_ref[0] * pt_ref[...].T
    ms = jnp.mean(h * h, axis=-1, keepdims=True)
    o_ref[0] = (h * lax.rsqrt(ms + RMS_EPS) * nw_ref[...]).astype(o_ref.dtype)


def _final(h1, peer_t, gate2, norm_w, out_dtype, tm):
    bsz, t, d = h1.shape
    nt = t // tm
    return pl.pallas_call(
        _final_kernel,
        grid=(bsz, nt),
        in_specs=[pl.BlockSpec((1, tm, d), lambda b, i: (b, i, 0)),
                  pl.BlockSpec((d, tm), lambda b, i: (0, b * nt + i)),
                  pl.BlockSpec((1, 1, d), lambda b, i: (b, 0, 0)),
                  pl.BlockSpec((1, d), lambda b, i: (0, 0))],
        out_specs=pl.BlockSpec((1, tm, d), lambda b, i: (b, i, 0)),
        out_shape=jax.ShapeDtypeStruct((bsz, t, d), out_dtype),
        compiler_params=_cparams(2),
        name="final_norm",
    )(h1, peer_t, gate2.reshape(bsz, 1, d), norm_w.reshape(1, d))


def _pack_in_proj(w_in, mu):
    d = w_in.shape[0]
    rp = 3 * RW + 64 + 64 + 160
    zc = lambda n: jnp.zeros((d, n), w_in.dtype)
    z0 = rp
    dt0 = rp + SSM_W + SSM_W + 512
    w = jnp.concatenate([
        w_in[:, :3 * RW],
        w_in[:, z0:z0 + SSM_W],
        w_in[:, z0 + SSM_W:z0 + 2 * SSM_W],
        w_in[:, z0 + 2 * SSM_W:z0 + 2 * SSM_W + 512],
        w_in[:, 3 * RW:rp], zc(LORA_DT - 288),
        w_in[:, dt0:dt0 + SSM_HEADS], zc(LORA_W - LORA_DT - SSM_HEADS)], axis=1)
    m = jnp.concatenate([mu[:3 * RW], jnp.zeros((P_LORA - 3 * RW,), mu.dtype), mu[3 * RW:rp],
                         jnp.zeros((LORA_W - 288,), mu.dtype)])
    return w.astype(BF16), m.reshape(1, P_TOT)


def _tile(t, pref):
    return pref if t % pref == 0 else t


def kernel(x, c, ada_w, ada_b, norm1_w, w_in, rwkv_mu, rwkv_w0, rwkv_w_up, rwkv_a0, rwkv_a_up, rwkv_g_up, rwkv_k_k, rwkv_k_a, rwkv_r_k, rwkv_ln_w, rwkv_ln_b, mamba_conv_w, mamba_conv_b, mamba_dt_bias, mamba_a_log, mamba_d, mamba_norm_w, w_out, norm2_w, peer_w_query, peer_sub_keys, peer_u, peer_v, final_norm_w):
    out_dtype = x.dtype
    bsz, t, d = x.shape
    ntok = bsz * t
    assert ada_w.shape[0] == 1, "single-layer trunk"
    h = x.astype(F32)
    mod = _ada_mod(c.astype(F32), ada_w[0], ada_b[0])
    shift1, scale1, gate1, shift2, scale2, gate2 = jnp.split(mod, 6, axis=-1)
    w_packed, mu_packed = _pack_in_proj(w_in[0], rwkv_mu[0])
    proj = _in_proj(h, norm1_w[0], shift1, scale1, w_packed, mu_packed, _tile(t, 1024), 768)
    o_rw = _rwkv(proj, rwkv_w0[0], rwkv_a0[0], rwkv_k_k[0], rwkv_k_a[0], rwkv_r_k[0].reshape(-1),
                 rwkv_ln_w[0], rwkv_ln_b[0], rwkv_w_up[0], rwkv_a_up[0], rwkv_g_up[0],
                 _tile(t, 256))
    o_ss = _ssd(proj, mamba_conv_w[0], mamba_conv_b[0], mamba_dt_bias[0], mamba_a_log[0],
                mamba_d[0], mamba_norm_w[0])
    wq = peer_w_query[0].reshape(d, PEER_HEADS, 2, PEER_HALF).transpose(0, 2, 1, 3)
    wq = wq.reshape(d, 2 * PEER_HEADS * PEER_HALF).astype(BF16)
    h1, n2, q = _out_proj(o_rw, o_ss, h, gate1, norm2_w[0], shift2, scale2, w_out[0], wq,
                          _tile(t, 256))
    cnt, e1, rk, e2 = _peer_route(q.reshape(ntok, -1), peer_sub_keys[0], _tile(ntok, 256))
    peer_t = _peer_dense(n2.reshape(ntok, d), peer_u[0].astype(BF16), peer_v[0].T.astype(BF16),
                         cnt, e1, rk, e2, _tile(ntok, 512), 1024)
    return _final(h1, peer_t, gate2, final_norm_w, out_dtype, _tile(t, 512))
```

```python
import math

import jax
import jax.numpy as jnp
from jax import lax
from jax.experimental import pallas as pl
from jax.experimental.pallas import tpu as pltpu

F32 = jnp.float32
BF16 = jnp.bfloat16

D_MODEL = 2048
RW = 1024
RW_HEAD = 64
SSM_W = 1024
SSM_HEAD = 64
SSM_HEADS = 16
SSM_STATE = 128
SSM_CONV = 4
PEER_HEADS = 8
PEER_KEYS = 128
PEER_HALF = 128
PEER_TOPK = 16
RMS_EPS = 1e-6
GN_EPS = 64e-5
SSM_EPS = 1e-5

P_R, P_K, P_V, P_Z, P_XS, P_BC, P_LORA = 0, 1024, 2048, 3072, 4096, 5120, 5632
P_TOT = 6144
LORA_W = 512
LORA_DT = 384

RWKV_CHUNK = 64
SSD_CHUNK = 128
LANES = 128

VMEM_LIMIT = 56 * 1024 * 1024


def _cparams(n_axes):
    return pltpu.CompilerParams(
        dimension_semantics=("arbitrary",) * n_axes, vmem_limit_bytes=VMEM_LIMIT)


def _split3(x):
    h1 = x.astype(BF16)
    r1 = x - h1.astype(F32)
    h2 = r1.astype(BF16)
    r2 = r1 - h2.astype(F32)
    return h1, h2, r2.astype(BF16)


def _mm(a, b, dims):
    return lax.dot_general(a.astype(BF16), b.astype(BF16), (dims, ((), ())),
                           preferred_element_type=F32)


def _dot(a, b):
    return _mm(a, b, ((1,), (0,)))


def _dot_nt(a, b):
    return _mm(a, b, ((1,), (1,)))


def _dot_tn(a, b):
    return _mm(a, b, ((0,), (0,)))


def _dot_xe(x, e_bf16):
    h1, h2, h3 = _split3(x)
    return _dot(h1, e_bf16) + _dot(h2, e_bf16) + _dot(h3, e_bf16)


def _dot_ex(e_bf16, x):
    h1, h2, h3 = _split3(x)
    return _dot(e_bf16, h1) + _dot(e_bf16, h2) + _dot(e_bf16, h3)


def _dot_ex2(e_bf16, x):
    hi = x.astype(BF16)
    lo = (x - hi.astype(F32)).astype(BF16)
    return _dot(e_bf16, hi) + _dot(e_bf16, lo)


def _dot_hi(a, b):
    ah = a.astype(BF16)
    al = (a - ah.astype(F32)).astype(BF16)
    bh = b.astype(BF16)
    bl = (b - bh.astype(F32)).astype(BF16)
    return _dot(ah, bh) + _dot(al, bh) + _dot(ah, bl)


def _silu(x):
    return x * jax.nn.sigmoid(x)


def _ada_kernel(cb_ref, w_ref, b_ref, o_ref):
    w = w_ref[...]
    reps = w.shape[1] // LANES
    rows = []
    for b in range(cb_ref.shape[0]):
        cb = _silu(cb_ref[b])
        rows.append(jnp.sum(w * jnp.concatenate([cb] * reps, axis=1), axis=0, keepdims=True))
    o_ref[...] = jnp.concatenate(rows, axis=0) + b_ref[...]


def _ada_mod(c, ada_w, ada_b):
    bsz, d = c.shape
    n = ada_w.shape[1]
    tn = 1024
    cb = jnp.broadcast_to(c[:, :, None], (bsz, d, LANES))
    return pl.pallas_call(
        _ada_kernel,
        grid=(n // tn,),
        in_specs=[pl.BlockSpec((bsz, d, LANES), lambda j: (0, 0, 0)),
                  pl.BlockSpec((d, tn), lambda j: (0, j)),
                  pl.BlockSpec((1, tn), lambda j: (0, j))],
        out_specs=pl.BlockSpec((bsz, tn), lambda j: (0, j)),
        out_shape=jax.ShapeDtypeStruct((bsz, n), F32),
        compiler_params=_cparams(1),
        name="ada_mod",
    )(cb, ada_w, ada_b.reshape(1, n))


def _rms_mod(x, w, shift, scale):
    ms = jnp.mean(x * x, axis=-1, keepdims=True)
    return (x * lax.rsqrt(ms + RMS_EPS) * w) * (1.0 + scale) + shift


def _inproj_kernel(x_ref, nw_ref, sh_ref, sc_ref, w_ref, mu_ref, o_ref, n_scr, carry_scr):
    i = pl.program_id(1)
    j = pl.program_id(2)

    @pl.when(j == 0)
    def _():
        n_scr[...] = _rms_mod(x_ref[0], nw_ref[...], sh_ref[0], sc_ref[0]).astype(BF16)

    n = n_scr[...]
    tm = n.shape[0]
    pw = 256
    first = i == 0
    for c in range(w_ref.shape[1] // pw):
        cs = slice(c * pw, (c + 1) * pw)
        p = _dot(n, w_ref[:, cs])
        prev_last = jnp.where(first, 0.0, carry_scr[j, 7:8, cs])
        row = lax.broadcasted_iota(jnp.int32, p.shape, 0)
        shifted = jnp.where(row == 0, prev_last, pltpu.roll(p, 1, 0))
        carry_scr[j, :, cs] = p[tm - 8:tm, :]
        o_ref[0, :, cs] = p + (shifted - p) * mu_ref[:, cs]


def _in_proj(x, norm_w, shift, scale, w_packed, mu_packed, tm, tn):
    bsz, t, d = x.shape
    nb = P_TOT // tn
    return pl.pallas_call(
        _inproj_kernel,
        grid=(bsz, t // tm, nb),
        in_specs=[pl.BlockSpec((1, tm, d), lambda b, i, j: (b, i, 0)),
                  pl.BlockSpec((1, d), lambda b, i, j: (0, 0)),
                  pl.BlockSpec((1, 1, d), lambda b, i, j: (b, 0, 0)),
                  pl.BlockSpec((1, 1, d), lambda b, i, j: (b, 0, 0)),
                  pl.BlockSpec((d, tn), lambda b, i, j: (0, j)),
                  pl.BlockSpec((1, tn), lambda b, i, j: (0, j))],
        out_specs=pl.BlockSpec((1, tm, tn), lambda b, i, j: (b, i, j)),
        out_shape=jax.ShapeDtypeStruct((bsz, t, P_TOT), F32),
        scratch_shapes=[pltpu.VMEM((tm, d), BF16), pltpu.VMEM((nb, 8, tn), F32)],
        compiler_params=_cparams(3),
        name="in_proj",
    )(x, norm_w.reshape(1, d), shift.reshape(bsz, 1, d), scale.reshape(bsz, 1, d),
      w_packed, mu_packed)


RWKV_PAIRS = 8
RWKV_A_CHUNKS = 2


def _rwkv_kernel(r_ref, k_ref, v_ref, lora_ref, w0_ref, a0_ref, kk_ref, ka_ref, rk_ref,
                 lnw_ref, lnb_ref, wup_ref, aup_ref, gup_ref, o_ref,
                 s_scr, kkn_scr, k2_scr, beta_scr, logw_scr, cum_scr, o_scr,
                 tinv_scr, arb_scr, vs_scr, lhs_scr, upd_scr, lkv_scr, arkv_scr):
    C = RWKV_CHUNK
    tc = r_ref.shape[1]
    n_chunks = tc // C
    n_pairs = r_ref.shape[2] // LANES

    @pl.when(pl.program_id(2) == 0)
    def _():
        s_scr[...] = jnp.zeros_like(s_scr)

    lane = lax.broadcasted_iota(jnp.int32, (1, LANES), 1)
    first = lane < RW_HEAD
    m0 = first.astype(F32)
    m1 = 1.0 - m0
    row = lax.broadcasted_iota(jnp.int32, (LANES, LANES), 0)
    col = lax.broadcasted_iota(jnp.int32, (LANES, LANES), 1)
    strict = row > col
    incl = row >= col
    blk16 = (row // 16) == (col // 16)
    eye = jnp.where(row == col, 1.0, 0.0)
    r64 = lax.broadcasted_iota(jnp.int32, (C, C), 0)
    c64 = lax.broadcasted_iota(jnp.int32, (C, C), 1)
    tri64 = jnp.where(r64 >= c64, 1.0, 0.0).astype(BF16)

    def per_head(x):
        cols = []
        for p in range(n_pairs):
            xp = x[:, p * LANES:(p + 1) * LANES]
            s0 = jnp.sum(xp * m0, axis=-1, keepdims=True)
            s1 = jnp.sum(xp * m1, axis=-1, keepdims=True)
            cols.append(jnp.where(first, s0, s1))
        return jnp.concatenate(cols, axis=1)

    head_ones = jnp.where((row // RW_HEAD) == (col // RW_HEAD), 1.0, 0.0).astype(BF16)

    def per_head_mxu(x):
        return jnp.concatenate([_dot(x[:, p * LANES:(p + 1) * LANES], head_ones)
                                for p in range(n_pairs)], axis=1)

    lora = lora_ref[0]
    wa = lora[:, 0:128]
    w_lin = w0_ref[...] + _dot_hi(jnp.tanh(wa), wup_ref[...])
    logw = -math.exp(-0.5) / (1.0 + jnp.exp(-w_lin))
    a = jax.nn.sigmoid(a0_ref[...] + _dot_hi(wa, aup_ref[...]))
    k = k_ref[0]
    kk = k * kk_ref[...]
    kkn = kk * lax.rsqrt(jnp.maximum(per_head(kk * kk), 1e-24))
    kkn_scr[...] = kkn
    k2_scr[...] = k * (1.0 + (a - 1.0) * ka_ref[...])
    beta_scr[...] = kkn * a
    logw_scr[...] = logw
    for c in range(n_chunks):
        cum_scr[c * C:(c + 1) * C, :] = _dot_ex2(tri64, logw[c * C:(c + 1) * C, :])

    def stack(x):
        return jnp.concatenate([x * m0, x * m1], axis=0)

    lns = [slice(p * LANES, (p + 1) * LANES) for p in range(n_pairs)]

    def indep(it, carry):
        chains = [(it * RWKV_A_CHUNKS + u, p) for u in range(RWKV_A_CHUNKS) for p in range(n_pairs)]
        sls = [pl.ds(pl.multiple_of(c * C, C), C) for c, _ in chains]
        lnc = [lns[p] for _, p in chains]
        cum = [cum_scr[sl, ln] for sl, ln in zip(sls, lnc)]
        ig = [jnp.exp(-c) for c in cum]
        d_end = [jnp.exp(c[C - 1:C, :] - c) for c in cum]
        k2 = [k2_scr[sl, ln] for sl, ln in zip(sls, lnc)]
        beta = [beta_scr[sl, ln] for sl, ln in zip(sls, lnc)]
        v_s = [stack(v_ref[0, sl, ln]).astype(BF16) for sl, ln in zip(sls, lnc)]
        rt_s = [stack(r_ref[0, sl, ln] * jnp.exp(c)) for sl, ln, c in zip(sls, lnc, cum)]
        kap_s = [stack(kkn_scr[sl, ln] * jnp.exp(c - logw_scr[sl, ln]))
                 for sl, ln, c in zip(sls, lnc, cum)]
        bt_s = [stack(b * g) for b, g in zip(beta, ig)]
        kt_s = [stack(k * g) for k, g in zip(k2, ig)]
        lhs = [jnp.concatenate([a, b], axis=0).astype(BF16) for a, b in zip(kap_s, rt_s)]
        for (c, p), k, b, d, l, vs in zip(chains, k2, beta, d_end, lhs, v_s):
            upd_scr[c, p] = jnp.concatenate([stack(k * d), -stack(b * d)], axis=0).astype(BF16)
            lhs_scr[c, p] = l
            vs_scr[c, p] = vs
        gram = [_dot_nt(l, jnp.concatenate([b, k], axis=0)) for l, b, k in zip(lhs, bt_s, kt_s)]
        gb = [g[:, :2 * C] for g in gram]
        gk = [g[:, 2 * C:] for g in gram]
        lb = [jnp.where(strict, g[:2 * C], 0.0) for g in gb]
        lk = [jnp.where(strict, g[:2 * C], 0.0) for g in gk]
        ark = [jnp.where(incl, g[2 * C:], 0.0) for g in gk]
        for (c, p), g in zip(chains, gb):
            arb_scr[c, p] = jnp.where(incl, g[2 * C:], 0.0).astype(BF16)

        ld = [jnp.where(blk16, x, 0.0) for x in lb]
        off = [x - d for x, d in zip(lb, ld)]
        m = [-d for d in ld]
        m2 = [_dot(a, a) for a in m]
        m34 = [_dot(jnp.concatenate([a, b], axis=0), b) for a, b in zip(m, m2)]
        m3 = [x[:2 * C] for x in m34]
        m4 = [x[2 * C:] for x in m34]
        lav = [_dot(jnp.concatenate([a, b], axis=0), v) for a, b, v in zip(lk, ark, v_s)]
        lkv = [x[:2 * C] for x in lav]
        arkv = [x[2 * C:] for x in lav]
        m8 = [_dot(a, a) for a in m4]
        m12 = [_dot(a, b) for a, b in zip(m4, m8)]
        dinv = [_dot(eye + a + b + c, eye + d + e + f)
                for a, b, c, d, e, f in zip(m, m2, m3, m4, m8, m12)]
        n1 = [_dot(d, o) for d, o in zip(dinv, off)]
        n2 = [_dot(a, a) for a in n1]
        x1 = [d + _dot(a, d) for d, a in zip(dinv, n2)]
        for (c, p), x, a, lv, av in zip(chains, x1, n1, lkv, arkv):
            tinv_scr[c, p] = (x - _dot(a, x)).astype(BF16)
            lkv_scr[c, p] = lv
            arkv_scr[c, p] = av
        return carry

    lax.fori_loop(0, n_chunks // RWKV_A_CHUNKS, indep, 0)

    def recur(ci, carry):
        sl = pl.ds(pl.multiple_of(ci * C, C), C)
        P = range(n_pairs)
        s = [s_scr[p] for p in P]
        ks = [_dot_nt(lhs_scr[ci, p], s[p]) for p in P]
        y = [_dot(tinv_scr[ci, p], ks[p][:2 * C] + lkv_scr[ci, p]) for p in P]
        sn = [_dot_tn(jnp.concatenate([vs_scr[ci, p], y[p].astype(BF16)], axis=0), upd_scr[ci, p])
              for p in P]
        for p in P:
            s_scr[p] = s[p] * jnp.exp(cum_scr[sl, lns[p]][C - 1:C, :]) + sn[p]
        arby = [_dot(arb_scr[ci, p], y[p]) for p in P]
        for p in P:
            o_s = ks[p][2 * C:] + arkv_scr[ci, p] - arby[p]
            o_scr[sl, lns[p]] = o_s[:C] + o_s[C:]
        return carry

    lax.fori_loop(0, n_chunks, recur, 0)

    o = o_scr[...]
    mean = per_head_mxu(o) * (1.0 / RW_HEAD)
    dlt = o - mean
    var = per_head_mxu(dlt * dlt) * (1.0 / RW_HEAD)
    on = dlt * lax.rsqrt(var + GN_EPS) * lnw_ref[...] + lnb_ref[...]
    bonus = per_head(r_ref[0] * k2_scr[...] * rk_ref[...]) * v_ref[0]
    g = _dot_hi(jax.nn.sigmoid(lora[:, 128:384]), gup_ref[...])
    o_ref[0] = ((on + bonus) * g).astype(o_ref.dtype)


def _rwkv(proj, w0, a0, k_k, k_a, r_k, ln_w, ln_b, w_up, a_up, g_up, tc):
    bsz, t, _ = proj.shape
    wl = RWKV_PAIRS * LANES
    row = lambda p: p.reshape(1, RW)
    wup_p = jnp.concatenate([w_up, jnp.zeros_like(a_up)], axis=0)
    aup_p = jnp.concatenate([jnp.zeros_like(w_up), a_up], axis=0)
    gup_p = jnp.concatenate([g_up, jnp.zeros((256 - g_up.shape[0], RW), F32)], axis=0)
    pspec = lambda off: pl.BlockSpec((1, tc, wl), lambda b, h, i: (b, i, off // wl + h))
    vec = pl.BlockSpec((1, wl), lambda b, h, i: (0, h))
    tile = pltpu.VMEM((tc, wl), F32)
    per_chain = lambda rows, dt: pltpu.VMEM((tc // RWKV_CHUNK, RWKV_PAIRS, rows, LANES), dt)
    return pl.pallas_call(
        _rwkv_kernel,
        grid=(bsz, RW // wl, t // tc),
        in_specs=[pspec(P_R), pspec(P_K), pspec(P_V),
                  pl.BlockSpec((1, tc, LORA_W), lambda b, h, i: (b, i, P_LORA // LORA_W)),
                  vec, vec, vec, vec, vec, vec, vec,
                  pl.BlockSpec((128, wl), lambda b, h, i: (0, h)),
                  pl.BlockSpec((128, wl), lambda b, h, i: (0, h)),
                  pl.BlockSpec((256, wl), lambda b, h, i: (0, h))],
        out_specs=pl.BlockSpec((1, tc, wl), lambda b, h, i: (b, i, h)),
        out_shape=jax.ShapeDtypeStruct((bsz, t, RW), BF16),
        scratch_shapes=[pltpu.VMEM((RWKV_PAIRS, LANES, LANES), F32),
                        tile, tile, tile, tile, tile, tile,
                        per_chain(LANES, BF16), per_chain(LANES, BF16), per_chain(LANES, BF16),
                        per_chain(2 * LANES, BF16), per_chain(2 * LANES, BF16),
                        per_chain(LANES, F32), per_chain(LANES, F32)],
        compiler_params=_cparams(3),
        name="rwkv",
    )(proj, proj, proj, proj, row(w0), row(a0), row(k_k), row(k_a), row(r_k), row(ln_w),
      row(ln_b), wup_p, aup_p, gup_p)


def _ssd_kernel(z_ref, x_ref, bc_ref, lora_ref, cwx_ref, cbx_ref, cwb_ref, cbb_ref, dtb_ref,
                alog_ref, dsk_ref, nw_ref, ex_ref, o_ref, extx, extb, st_scr):
    L = SSD_CHUNK

    @pl.when(pl.program_id(1) == 0)
    def _():
        extx[0:8, :] = jnp.zeros((8, extx.shape[1]), F32)
        extb[0:8, :] = jnp.zeros((8, extb.shape[1]), F32)
        st_scr[...] = jnp.zeros_like(st_scr)

    def conv(ext, cur, w_ref, b_ref):
        ext[8:8 + L, :] = cur
        acc = b_ref[...] + w_ref[0:1, :] * ext[5:5 + L, :]
        for j in range(1, SSM_CONV):
            acc = acc + w_ref[j:j + 1, :] * ext[5 + j:5 + j + L, :]
        ext[0:8, :] = ext[L:L + 8, :]
        return _silu(acc)

    xs = conv(extx, x_ref[0], cwx_ref, cbx_ref)
    bc = conv(extb, bc_ref[0], cwb_ref, cbb_ref)

    row = lax.broadcasted_iota(jnp.int32, (L, L), 0)
    col = lax.broadcasted_iota(jnp.int32, (L, L), 1)
    incl = row >= col
    tri = jnp.where(incl, 1.0, 0.0).astype(BF16)
    lane = lax.broadcasted_iota(jnp.int32, (1, LANES), 1)
    m_par = [(lane < SSM_HEAD).astype(F32), (lane >= SSM_HEAD).astype(F32)]

    ex = ex_ref[...]
    dts = jax.nn.softplus(lora_ref[0][:, LORA_DT:LORA_DT + LANES] + dtb_ref[...])
    a_slab = dts * (-jnp.exp(alog_ref[...]))
    acs_slab = _dot_ex(tri, a_slab)
    acs_t = acs_slab.T
    dtx = _dot_xe(dts, ex)
    acs_x = _dot_xe(acs_slab, ex)
    last = acs_x[L - 1:L, :]
    xdt = xs * dtx
    xdte = xdt * jnp.exp(last - acs_x)
    eacs = jnp.exp(acs_x)
    elast = jnp.exp(last)

    ys = []
    for j in range(SSM_HEADS // 2):
        g = j // (SSM_HEADS // 4)
        bm = bc[:, g * LANES:(g + 1) * LANES]
        cm = bc[:, (2 + g) * LANES:(3 + g) * LANES]
        pr = slice(j * LANES, (j + 1) * LANES)
        cb = _dot_nt(cm, bm)
        st = st_scr[j]
        y = eacs[:, pr] * _dot(cm, st)
        xp = xdt[:, pr]
        for par in range(2):
            h = 2 * j + par
            diff = acs_slab[:, h:h + 1] - acs_t[h:h + 1, :]
            mh = jnp.exp(jnp.where(incl, diff, -jnp.inf)) * cb
            y = y + _dot(mh, xp * m_par[par])
        st_scr[j] = st * elast[:, pr] + _dot_tn(bm, xdte[:, pr])
        ys.append(y)
    y = jnp.concatenate(ys, axis=1) + xs * dsk_ref[...]
    y = y * _silu(z_ref[0])
    half = SSM_W // 2
    outs = []
    for g in range(2):
        seg = y[:, g * half:(g + 1) * half]
        outs.append(seg * lax.rsqrt(jnp.mean(seg * seg, axis=-1, keepdims=True) + SSM_EPS))
    o_ref[0] = (jnp.concatenate(outs, axis=1) * nw_ref[...]).astype(o_ref.dtype)


def _ssd(proj, conv_w, conv_b, dt_bias, a_log, d_skip, norm_w):
    bsz, t, _ = proj.shape
    L = SSD_CHUNK
    pad_slab = lambda p: jnp.concatenate([p, jnp.zeros((LANES - SSM_HEADS,), F32)]).reshape(1, LANES)
    hh = jnp.arange(LANES)[:, None]
    ll = jnp.arange(SSM_W)[None, :]
    ex = (hh == ll // SSM_HEAD).astype(BF16)
    cwx, cwb = conv_w[:, :SSM_W], conv_w[:, SSM_W:]
    cbx, cbb = conv_b[:SSM_W].reshape(1, -1), conv_b[SSM_W:].reshape(1, -1)
    full = lambda a: pl.BlockSpec(a.shape, lambda b, i: (0,) * a.ndim)
    args = [cwx, cbx, cwb, cbb, pad_slab(dt_bias), pad_slab(a_log),
            jnp.repeat(d_skip, SSM_HEAD).reshape(1, SSM_W), norm_w.reshape(1, SSM_W), ex]
    return pl.pallas_call(
        _ssd_kernel,
        grid=(bsz, t // L),
        in_specs=[pl.BlockSpec((1, L, SSM_W), lambda b, i: (b, i, P_Z // SSM_W)),
                  pl.BlockSpec((1, L, SSM_W), lambda b, i: (b, i, P_XS // SSM_W)),
                  pl.BlockSpec((1, L, 512), lambda b, i: (b, i, P_BC // 512)),
                  pl.BlockSpec((1, L, LORA_W), lambda b, i: (b, i, P_LORA // LORA_W))]
                 + [full(a) for a in args],
        out_specs=pl.BlockSpec((1, L, SSM_W), lambda b, i: (b, i, 0)),
        out_shape=jax.ShapeDtypeStruct((bsz, t, SSM_W), BF16),
        scratch_shapes=[pltpu.VMEM((L + 8, SSM_W), F32), pltpu.VMEM((L + 8, 512), F32),
                        pltpu.VMEM((SSM_HEADS // 2, SSM_STATE, LANES), F32)],
        compiler_params=_cparams(2),
        name="ssd",
    )(proj, proj, proj, proj, *args)


def _outproj_kernel(orw_ref, oss_ref, x_ref, g1_ref, nw_ref, sh_ref, sc_ref, wo1_ref, wo2_ref,
                    wq_ref, h1_ref, n2_ref, q_ref):
    mix = _dot(orw_ref[0], wo1_ref[...]) + _dot(oss_ref[0], wo2_ref[...])
    h1 = x_ref[0] + g1_ref[0] * mix
    h1_ref[0] = h1
    n2 = _rms_mod(h1, nw_ref[...], sh_ref[0], sc_ref[0]).astype(BF16)
    n2_ref[0] = n2
    q_ref[0] = _dot(n2, wq_ref[...]).astype(q_ref.dtype)


def _out_proj(o_rw, o_ss, x, gate1, norm_w, shift, scale, w_out, w_query_p, tm):
    bsz, t, d = x.shape
    wo1 = w_out[:RW].astype(BF16)
    wo2 = w_out[RW:].astype(BF16)
    tok = lambda w: pl.BlockSpec((1, tm, w), lambda b, i: (b, i, 0))
    per_b = pl.BlockSpec((1, 1, d), lambda b, i: (b, 0, 0))
    const = lambda a: pl.BlockSpec(a.shape, lambda b, i: (0,) * a.ndim, pipeline_mode=pl.Buffered(1))
    nw = norm_w.reshape(1, d)
    return pl.pallas_call(
        _outproj_kernel,
        grid=(bsz, t // tm),
        in_specs=[tok(RW), tok(SSM_W), tok(d), per_b, const(nw), per_b, per_b,
                  const(wo1), const(wo2), const(w_query_p)],
        out_specs=[tok(d), tok(d), tok(d)],
        out_shape=[jax.ShapeDtypeStruct((bsz, t, d), F32), jax.ShapeDtypeStruct((bsz, t, d), BF16),
                   jax.ShapeDtypeStruct((bsz, t, d), BF16)],
        compiler_params=_cparams(2),
        name="out_proj",
    )(o_rw, o_ss, x, gate1.reshape(bsz, 1, d), nw, shift.reshape(bsz, 1, d),
      scale.reshape(bsz, 1, d), wo1, wo2, w_query_p)


_CANDS = [(i, j) for i in range(PEER_TOPK) for j in range(PEER_TOPK) if (i + 1) * (j + 1) <= PEER_TOPK]


def _route_kernel(q_ref, k1_ref, k2_ref, cnt_ref, e1_ref, rk_ref, e2_ref):
    nh, hd, K = PEER_HEADS, PEER_HALF, PEER_TOPK
    q = q_ref[...]

    def top_vals(s, want_rank):
        vals, prev = [], None
        rank = jnp.zeros_like(s) if want_rank else None
        for _ in range(K):
            if prev is None:
                cand = s
            else:
                below = s < prev
                cand = jnp.where(below, s, -jnp.inf)
                if want_rank:
                    rank = jnp.where(below, rank + 1.0, rank)
            prev = jnp.max(cand, axis=0, keepdims=True)
            vals.append(prev)
        if want_rank:
            rank = jnp.where(s < prev, rank + 1.0, rank)
        return vals, rank

    s1 = [_dot_nt(k1_ref[h], q[:, h * hd:(h + 1) * hd]) for h in range(nh)]
    s2 = [_dot_nt(k2_ref[h], q[:, (nh + h) * hd:(nh + h + 1) * hd]) for h in range(nh)]
    a_h = [top_vals(s, False)[0] for s in s1]
    b_hr = [top_vals(s, True) for s in s2]
    b_h = [x[0] for x in b_hr]
    a = [jnp.concatenate([a_h[h][i] for h in range(nh)], axis=0) for i in range(K)]
    b = [jnp.concatenate([b_h[h][i] for h in range(nh)], axis=0) for i in range(K)]

    sums = {c: a[c[0]] + b[c[1]] for c in _CANDS}
    tau = None
    for _ in range(K):
        best = None
        for c in _CANDS:
            v = sums[c] if tau is None else jnp.where(sums[c] < tau, sums[c], -jnp.inf)
            best = v if best is None else jnp.maximum(best, v)
        tau = best
    ea = [jnp.exp(ai - a[0]) for ai in a]
    eb = [jnp.exp(bj - b[0]) for bj in b]
    zsum = jnp.zeros_like(tau)
    cnt = [jnp.zeros_like(tau) for _ in range(K)]
    for (i, j) in _CANDS:
        sel = sums[(i, j)] >= tau
        zsum = zsum + jnp.where(sel, ea[i] * eb[j], 0.0)
        cnt[i] = cnt[i] + jnp.where(sel, 1.0, 0.0)
    zinv = 1.0 / zsum
    for h in range(nh):
        row = slice(h, h + 1)
        cfull = jnp.zeros_like(s1[h])
        for i in range(K):
            cfull = jnp.where(s1[h] == a_h[h][i], cnt[i][row, :], cfull)
        cnt_ref[h] = cfull
        e1_ref[h] = jnp.exp(s1[h] - a_h[h][0])
        rk_ref[h] = b_hr[h][1].astype(BF16)
        e2_ref[h] = (jnp.exp(s2[h] - b_h[h][0]) * zinv[row, :]).astype(BF16)


def _peer_route(q, sub_keys, tm):
    ntok = q.shape[0]
    nk, nh = PEER_KEYS, PEER_HEADS
    k1 = sub_keys[:, 0].astype(BF16)
    k2 = sub_keys[:, 1].astype(BF16)
    blk = pl.BlockSpec((nh, nk, tm), lambda i: (0, 0, i))
    f32_out = jax.ShapeDtypeStruct((nh, nk, ntok), F32)
    bf16_out = jax.ShapeDtypeStruct((nh, nk, ntok), BF16)
    return pl.pallas_call(
        _route_kernel,
        grid=(ntok // tm,),
        in_specs=[pl.BlockSpec((tm, q.shape[1]), lambda i: (i, 0)),
                  pl.BlockSpec(k1.shape, lambda i: (0, 0, 0)),
                  pl.BlockSpec(k2.shape, lambda i: (0, 0, 0))],
        out_specs=[blk, blk, blk, blk],
        out_shape=[f32_out, f32_out, bf16_out, bf16_out],
        compiler_params=_cparams(1),
        name="peer_route",
    )(q, k1, k2)


def _dense_kernel(n2_ref, u_ref, vt_ref, cnt_ref, e1_ref, rk_ref, e2_ref, o_ref, w_scr):
    j = pl.program_id(1)
    nk = PEER_KEYS
    ec = u_ref.shape[0]
    tm = n2_ref.shape[0]
    n_piece = 4
    pc = ec // n_piece
    zero = jnp.zeros((), BF16)
    n2 = n2_ref[...]

    def weights(p, act):
        for s in range(pc // nk):
            sl = p * (pc // nk) + s
            gsum = jnp.zeros((nk, tm), BF16)
            for h in range(PEER_HEADS):
                cb = jnp.broadcast_to(cnt_ref[h, sl:sl + 1, :], (nk, tm)).astype(BF16)
                eb = jnp.broadcast_to(e1_ref[h, sl:sl + 1, :], (nk, tm)).astype(BF16)
                gsum = gsum + jnp.where(rk_ref[h] < cb, e2_ref[h], zero) * eb
            a = act[s * nk:(s + 1) * nk, :]
            gelu = 0.5 * a * (1.0 + lax.erf(a * (2.0 ** -0.5)))
            w_scr[sl * nk:(sl + 1) * nk, :] = gsum * gelu.astype(BF16)

    acts = [_dot_nt(u_ref[p * pc:(p + 1) * pc, :], n2) for p in range(n_piece)]
    for p in range(n_piece):
        weights(p, acts[p])
        part = _dot(vt_ref[:, p * pc:(p + 1) * pc], w_scr[p * pc:(p + 1) * pc, :])
        prev = jnp.where(j == 0, 0.0, o_ref[...]) if p == 0 else o_ref[...]
        o_ref[...] = prev + part


def _peer_dense(n2, u_bf, vt_bf, cnt, e1, rk, e2, tm, ec):
    ntok, d = n2.shape
    n_exp = u_bf.shape[0]
    ns = ec // PEER_KEYS
    rows = pl.BlockSpec((PEER_HEADS, ns, tm), lambda i, j: (0, j, i))
    slab = pl.BlockSpec((PEER_HEADS, PEER_KEYS, tm), lambda i, j: (0, 0, i))
    return pl.pallas_call(
        _dense_kernel,
        grid=(ntok // tm, n_exp // ec),
        in_specs=[pl.BlockSpec((tm, d), lambda i, j: (i, 0)),
                  pl.BlockSpec((ec, d), lambda i, j: (j, 0)),
                  pl.BlockSpec((d, ec), lambda i, j: (0, j)),
                  rows, rows, slab, slab],
        out_specs=pl.BlockSpec((d, tm), lambda i, j: (0, i)),
        out_shape=jax.ShapeDtypeStruct((d, ntok), F32),
        scratch_shapes=[pltpu.VMEM((ec, tm), BF16)],
        compiler_params=_cparams(2),
        name="peer_dense",
    )(n2, u_bf, vt_bf, cnt, e1, rk, e2)


def _final_kernel(h1_ref, pt_ref, g2_ref, nw_ref, o_ref):
    h = h1_ref[0] + g2_ref[0] * pt_ref[...].T
    ms = jnp.mean(h * h, axis=-1, keepdims=True)
    o_ref[0] = (h * lax.rsqrt(ms + RMS_EPS) * nw_ref[...]).astype(o_ref.dtype)


def _final(h1, peer_t, gate2, norm_w, out_dtype, tm):
    bsz, t, d = h1.shape
    nt = t // tm
    return pl.pallas_call(
        _final_kernel,
        grid=(bsz, nt),
        in_specs=[pl.BlockSpec((1, tm, d), lambda b, i: (b, i, 0)),
                  pl.BlockSpec((d, tm), lambda b, i: (0, b * nt + i)),
                  pl.BlockSpec((1, 1, d), lambda b, i: (b, 0, 0)),
                  pl.BlockSpec((1, d), lambda b, i: (0, 0))],
        out_specs=pl.BlockSpec((1, tm, d), lambda b, i: (b, i, 0)),
        out_shape=jax.ShapeDtypeStruct((bsz, t, d), out_dtype),
        compiler_params=_cparams(2),
        name="final_norm",
    )(h1, peer_t, gate2.reshape(bsz, 1, d), norm_w.reshape(1, d))


def _pack_in_proj(w_in, mu):
    d = w_in.shape[0]
    rp = 3 * RW + 64 + 64 + 160
    zc = lambda n: jnp.zeros((d, n), w_in.dtype)
    z0 = rp
    dt0 = rp + SSM_W + SSM_W + 512
    w = jnp.concatenate([
        w_in[:, :3 * RW],
        w_in[:, z0:z0 + SSM_W],
        w_in[:, z0 + SSM_W:z0 + 2 * SSM_W],
        w_in[:, z0 + 2 * SSM_W:z0 + 2 * SSM_W + 512],
        w_in[:, 3 * RW:rp], zc(LORA_DT - 288),
        w_in[:, dt0:dt0 + SSM_HEADS], zc(LORA_W - LORA_DT - SSM_HEADS)], axis=1)
    m = jnp.concatenate([mu[:3 * RW], jnp.zeros((P_LORA - 3 * RW,), mu.dtype), mu[3 * RW:rp],
                         jnp.zeros((LORA_W - 288,), mu.dtype)])
    return w.astype(BF16), m.reshape(1, P_TOT)


def _tile(t, pref):
    return pref if t % pref == 0 else t


def kernel(x, c, ada_w, ada_b, norm1_w, w_in, rwkv_mu, rwkv_w0, rwkv_w_up, rwkv_a0, rwkv_a_up, rwkv_g_up, rwkv_k_k, rwkv_k_a, rwkv_r_k, rwkv_ln_w, rwkv_ln_b, mamba_conv_w, mamba_conv_b, mamba_dt_bias, mamba_a_log, mamba_d, mamba_norm_w, w_out, norm2_w, peer_w_query, peer_sub_keys, peer_u, peer_v, final_norm_w):
    out_dtype = x.dtype
    bsz, t, d = x.shape
    ntok = bsz * t
    assert ada_w.shape[0] == 1, "single-layer trunk"
    h = x.astype(F32)
    mod = _ada_mod(c.astype(F32), ada_w[0], ada_b[0])
    shift1, scale1, gate1, shift2, scale2, gate2 = jnp.split(mod, 6, axis=-1)
    w_packed, mu_packed = _pack_in_proj(w_in[0], rwkv_mu[0])
    proj = _in_proj(h, norm1_w[0], shift1, scale1, w_packed, mu_packed, _tile(t, 1024), 768)
    o_rw = _rwkv(proj, rwkv_w0[0], rwkv_a0[0], rwkv_k_k[0], rwkv_k_a[0], rwkv_r_k[0].reshape(-1),
                 rwkv_ln_w[0], rwkv_ln_b[0], rwkv_w_up[0], rwkv_a_up[0], rwkv_g_up[0],
                 _tile(t, 256))
    o_ss = _ssd(proj, mamba_conv_w[0], mamba_conv_b[0], mamba_dt_bias[0], mamba_a_log[0],
                mamba_d[0], mamba_norm_w[0])
    wq = peer_w_query[0].reshape(d, PEER_HEADS, 2, PEER_HALF).transpose(0, 2, 1, 3)
    wq = wq.reshape(d, 2 * PEER_HEADS * PEER_HALF).astype(BF16)
    h1, n2, q = _out_proj(o_rw, o_ss, h, gate1, norm2_w[0], shift2, scale2, w_out[0], wq,
                          _tile(t, 256))
    cnt, e1, rk, e2 = _peer_route(q.reshape(ntok, -1), peer_sub_keys[0], _tile(ntok, 256))
    peer_t = _peer_dense(n2.reshape(ntok, d), peer_u[0].astype(BF16), peer_v[0].T.astype(BF16),
                         cnt, e1, rk, e2, _tile(ntok, 512), 1024)
    return _final(h1, peer_t, gate2, final_norm_w, out_dtype, _tile(t, 512))
```

```python
import math

import jax
import jax.numpy as jnp
from jax import lax
from jax.experimental import pallas as pl
from jax.experimental.pallas import tpu as pltpu

F32 = jnp.float32
BF16 = jnp.bfloat16

D_MODEL = 2048
RW = 1024
RW_HEAD = 64
SSM_W = 1024
SSM_HEAD = 64
SSM_HEADS = 16
SSM_STATE = 128
SSM_CONV = 4
PEER_HEADS = 8
PEER_KEYS = 128
PEER_HALF = 128
PEER_TOPK = 16
RMS_EPS = 1e-6
GN_EPS = 64e-5
SSM_EPS = 1e-5

P_R, P_K, P_V, P_Z, P_XS, P_BC, P_LORA = 0, 1024, 2048, 3072, 4096, 5120, 5632
P_TOT = 6144
LORA_W = 512
LORA_DT = 384

RWKV_CHUNK = 64
SSD_CHUNK = 128
LANES = 128

VMEM_LIMIT = 56 * 1024 * 1024


def _cparams(n_axes):
    return pltpu.CompilerParams(
        dimension_semantics=("arbitrary",) * n_axes, vmem_limit_bytes=VMEM_LIMIT)


def _split3(x):
    h1 = x.astype(BF16)
    r1 = x - h1.astype(F32)
    h2 = r1.astype(BF16)
    r2 = r1 - h2.astype(F32)
    return h1, h2, r2.astype(BF16)


def _mm(a, b, dims):
    return lax.dot_general(a.astype(BF16), b.astype(BF16), (dims, ((), ())),
                           preferred_element_type=F32)


def _dot(a, b):
    return _mm(a, b, ((1,), (0,)))


def _dot_nt(a, b):
    return _mm(a, b, ((1,), (1,)))


def _dot_tn(a, b):
    return _mm(a, b, ((0,), (0,)))


def _dot_xe(x, e_bf16):
    h1, h2, h3 = _split3(x)
    return _dot(h1, e_bf16) + _dot(h2, e_bf16) + _dot(h3, e_bf16)


def _dot_ex(e_bf16, x):
    h1, h2, h3 = _split3(x)
    return _dot(e_bf16, h1) + _dot(e_bf16, h2) + _dot(e_bf16, h3)


def _dot_ex2(e_bf16, x):
    hi = x.astype(BF16)
    lo = (x - hi.astype(F32)).astype(BF16)
    return _dot(e_bf16, hi) + _dot(e_bf16, lo)


def _dot_hi(a, b):
    ah = a.astype(BF16)
    al = (a - ah.astype(F32)).astype(BF16)
    bh = b.astype(BF16)
    bl = (b - bh.astype(F32)).astype(BF16)
    return _dot(ah, bh) + _dot(al, bh) + _dot(ah, bl)


def _silu(x):
    return x * jax.nn.sigmoid(x)


def _ada_kernel(cb_ref, w_ref, b_ref, o_ref):
    w = w_ref[...]
    reps = w.shape[1] // LANES
    rows = []
    for b in range(cb_ref.shape[0]):
        cb = _silu(cb_ref[b])
        rows.append(jnp.sum(w * jnp.concatenate([cb] * reps, axis=1), axis=0, keepdims=True))
    o_ref[...] = jnp.concatenate(rows, axis=0) + b_ref[...]


def _ada_mod(c, ada_w, ada_b):
    bsz, d = c.shape
    n = ada_w.shape[1]
    tn = 1024
    cb = jnp.broadcast_to(c[:, :, None], (bsz, d, LANES))
    return pl.pallas_call(
        _ada_kernel,
        grid=(n // tn,),
        in_specs=[pl.BlockSpec((bsz, d, LANES), lambda j: (0, 0, 0)),
                  pl.BlockSpec((d, tn), lambda j: (0, j)),
                  pl.BlockSpec((1, tn), lambda j: (0, j))],
        out_specs=pl.BlockSpec((bsz, tn), lambda j: (0, j)),
        out_shape=jax.ShapeDtypeStruct((bsz, n), F32),
        compiler_params=_cparams(1),
        name="ada_mod",
    )(cb, ada_w, ada_b.reshape(1, n))


def _rms_mod(x, w, shift, scale):
    ms = jnp.mean(x * x, axis=-1, keepdims=True)
    return (x * lax.rsqrt(ms + RMS_EPS) * w) * (1.0 + scale) + shift


def _inproj_kernel(x_ref, nw_ref, sh_ref, sc_ref, w_ref, mu_ref, o_ref, n_scr, carry_scr):
    i = pl.program_id(1)
    j = pl.program_id(2)

    @pl.when(j == 0)
    def _():
        n_scr[...] = _rms_mod(x_ref[0], nw_ref[...], sh_ref[0], sc_ref[0]).astype(BF16)

    n = n_scr[...]
    tm = n.shape[0]
    pw = 256
    first = i == 0
    for c in range(w_ref.shape[1] // pw):
        cs = slice(c * pw, (c + 1) * pw)
        p = _dot(n, w_ref[:, cs])
        prev_last = jnp.where(first, 0.0, carry_scr[j, 7:8, cs])
        row = lax.broadcasted_iota(jnp.int32, p.shape, 0)
        shifted = jnp.where(row == 0, prev_last, pltpu.roll(p, 1, 0))
        carry_scr[j, :, cs] = p[tm - 8:tm, :]
        o_ref[0, :, cs] = p + (shifted - p) * mu_ref[:, cs]


def _in_proj(x, norm_w, shift, scale, w_packed, mu_packed, tm, tn):
    bsz, t, d = x.shape
    nb = P_TOT // tn
    return pl.pallas_call(
        _inproj_kernel,
        grid=(bsz, t // tm, nb),
        in_specs=[pl.BlockSpec((1, tm, d), lambda b, i, j: (b, i, 0)),
                  pl.BlockSpec((1, d), lambda b, i, j: (0, 0)),
                  pl.BlockSpec((1, 1, d), lambda b, i, j: (b, 0, 0)),
                  pl.BlockSpec((1, 1, d), lambda b, i, j: (b, 0, 0)),
                  pl.BlockSpec((d, tn), lambda b, i, j: (0, j)),
                  pl.BlockSpec((1, tn), lambda b, i, j: (0, j))],
        out_specs=pl.BlockSpec((1, tm, tn), lambda b, i, j: (b, i, j)),
        out_shape=jax.ShapeDtypeStruct((bsz, t, P_TOT), F32),
        scratch_shapes=[pltpu.VMEM((tm, d), BF16), pltpu.VMEM((nb, 8, tn), F32)],
        compiler_params=_cparams(3),
        name="in_proj",
    )(x, norm_w.reshape(1, d), shift.reshape(bsz, 1, d), scale.reshape(bsz, 1, d),
      w_packed, mu_packed)


RWKV_PAIRS = 8
RWKV_A_CHUNKS = 2


def _rwkv_kernel(r_ref, k_ref, v_ref, lora_ref, w0_ref, a0_ref, kk_ref, ka_ref, rk_ref,
                 lnw_ref, lnb_ref, wup_ref, aup_ref, gup_ref, o_ref,
                 s_scr, kkn_scr, k2_scr, beta_scr, logw_scr, cum_scr, o_scr,
                 tinv_scr, arb_scr, vs_scr, lhs_scr, upd_scr, lkv_scr, arkv_scr):
    C = RWKV_CHUNK
    tc = r_ref.shape[1]
    n_chunks = tc // C
    n_pairs = r_ref.shape[2] // LANES

    @pl.when(pl.program_id(2) == 0)
    def _():
        s_scr[...] = jnp.zeros_like(s_scr)

    lane = lax.broadcasted_iota(jnp.int32, (1, LANES), 1)
    first = lane < RW_HEAD
    m0 = first.astype(F32)
    m1 = 1.0 - m0
    row = lax.broadcasted_iota(jnp.int32, (LANES, LANES), 0)
    col = lax.broadcasted_iota(jnp.int32, (LANES, LANES), 1)
    strict = row > col
    incl = row >= col
    blk16 = (row // 16) == (col // 16)
    eye = jnp.where(row == col, 1.0, 0.0)
    r64 = lax.broadcasted_iota(jnp.int32, (C, C), 0)
    c64 = lax.broadcasted_iota(jnp.int32, (C, C), 1)
    tri64 = jnp.where(r64 >= c64, 1.0, 0.0).astype(BF16)

    def per_head(x):
        cols = []
        for p in range(n_pairs):
            xp = x[:, p * LANES:(p + 1) * LANES]
            s0 = jnp.sum(xp * m0, axis=-1, keepdims=True)
            s1 = jnp.sum(xp * m1, axis=-1, keepdims=True)
            cols.append(jnp.where(first, s0, s1))
        return jnp.concatenate(cols, axis=1)

    head_ones = jnp.where((row // RW_HEAD) == (col // RW_HEAD), 1.0, 0.0).astype(BF16)

    def per_head_mxu(x):
        return jnp.concatenate([_dot(x[:, p * LANES:(p + 1) * LANES], head_ones)
                                for p in range(n_pairs)], axis=1)

    lora = lora_ref[0]
    wa = lora[:, 0:128]
    w_lin = w0_ref[...] + _dot_hi(jnp.tanh(wa), wup_ref[...])
    logw = -math.exp(-0.5) / (1.0 + jnp.exp(-w_lin))
    a = jax.nn.sigmoid(a0_ref[...] + _dot_hi(wa, aup_ref[...]))
    k = k_ref[0]
    kk = k * kk_ref[...]
    kkn = kk * lax.rsqrt(jnp.maximum(per_head(kk * kk), 1e-24))
    kkn_scr[...] = kkn
    k2_scr[...] = k * (1.0 + (a - 1.0) * ka_ref[...])
    beta_scr[...] = kkn * a
    logw_scr[...] = logw
    for c in range(n_chunks):
        cum_scr[c * C:(c + 1) * C, :] = _dot_ex2(tri64, logw[c * C:(c + 1) * C, :])

    def stack(x):
        return jnp.concatenate([x * m0, x * m1], axis=0)

    lns = [slice(p * LANES, (p + 1) * LANES) for p in range(n_pairs)]

    def indep(first_chunk):
        chains = [(first_chunk + u, p) for u in range(RWKV_A_CHUNKS) for p in range(n_pairs)]
        sls = [slice(c * C, (c + 1) * C) for c, _ in chains]
        lnc = [lns[p] for _, p in chains]
        cum = [cum_scr[sl, ln] for sl, ln in zip(sls, lnc)]
        ig = [jnp.exp(-c) for c in cum]
        d_end = [jnp.exp(c[C - 1:C, :] - c) for c in cum]
        k2 = [k2_scr[sl, ln] for sl, ln in zip(sls, lnc)]
        beta = [beta_scr[sl, ln] for sl, ln in zip(sls, lnc)]
        v_s = [stack(v_ref[0, sl, ln]).astype(BF16) for sl, ln in zip(sls, lnc)]
        yield
        rt_s = [stack(r_ref[0, sl, ln] * jnp.exp(c)) for sl, ln, c in zip(sls, lnc, cum)]
        kap_s = [stack(kkn_scr[sl, ln] * jnp.exp(c - logw_scr[sl, ln]))
                 for sl, ln, c in zip(sls, lnc, cum)]
        bt_s = [stack(b * g) for b, g in zip(beta, ig)]
        kt_s = [stack(k * g) for k, g in zip(k2, ig)]
        lhs = [jnp.concatenate([a, b], axis=0).astype(BF16) for a, b in zip(kap_s, rt_s)]
        yield
        for (c, p), k, b, d, l, vs in zip(chains, k2, beta, d_end, lhs, v_s):
            upd_scr[c, p] = jnp.concatenate([stack(k * d), -stack(b * d)], axis=0).astype(BF16)
            lhs_scr[c, p] = l
            vs_scr[c, p] = vs
        yield
        gram = [_dot_nt(l, jnp.concatenate([b, k], axis=0)) for l, b, k in zip(lhs, bt_s, kt_s)]
        yield
        gb = [g[:, :2 * C] for g in gram]
        gk = [g[:, 2 * C:] for g in gram]
        lb = [jnp.where(strict, g[:2 * C], 0.0) for g in gb]
        lk = [jnp.where(strict, g[:2 * C], 0.0) for g in gk]
        ark = [jnp.where(incl, g[2 * C:], 0.0) for g in gk]
        for (c, p), g in zip(chains, gb):
            arb_scr[c, p] = jnp.where(incl, g[2 * C:], 0.0).astype(BF16)
        yield

        ld = [jnp.where(blk16, x, 0.0) for x in lb]
        off = [x - d for x, d in zip(lb, ld)]
        m = [-d for d in ld]
        m2 = [_dot(a, a) for a in m]
        yield
        m34 = [_dot(jnp.concatenate([a, b], axis=0), b) for a, b in zip(m, m2)]
        m3 = [x[:2 * C] for x in m34]
        m4 = [x[2 * C:] for x in m34]
        yield
        lav = [_dot(jnp.concatenate([a, b], axis=0), v) for a, b, v in zip(lk, ark, v_s)]
        lkv = [x[:2 * C] for x in lav]
        arkv = [x[2 * C:] for x in lav]
        yield
        m8 = [_dot(a, a) for a in m4]
        yield
        m12 = [_dot(a, b) for a, b in zip(m4, m8)]
        yield
        dinv = [_dot(eye + a + b + c, eye + d + e + f)
                for a, b, c, d, e, f in zip(m, m2, m3, m4, m8, m12)]
        yield
        n1 = [_dot(d, o) for d, o in zip(dinv, off)]
        yield
        n2 = [_dot(a, a) for a in n1]
        yield
        x1 = [d + _dot(a, d) for d, a in zip(dinv, n2)]
        yield
        for (c, p), x, a, lv, av in zip(chains, x1, n1, lkv, arkv):
            tinv_scr[c, p] = (x - _dot(a, x)).astype(BF16)
            lkv_scr[c, p] = lv
            arkv_scr[c, p] = av
        yield

    def recur(chunks):
        P = range(n_pairs)
        for ci in chunks:
            sl = slice(ci * C, (ci + 1) * C)
            s = [s_scr[p] for p in P]
            ks = [_dot_nt(lhs_scr[ci, p], s[p]) for p in P]
            yield
            y = [_dot(tinv_scr[ci, p], ks[p][:2 * C] + lkv_scr[ci, p]) for p in P]
            yield
            sn = [_dot_tn(jnp.concatenate([vs_scr[ci, p], y[p].astype(BF16)], axis=0), upd_scr[ci, p])
                  for p in P]
            yield
            for p in P:
                s_scr[p] = s[p] * jnp.exp(cum_scr[sl, lns[p]][C - 1:C, :]) + sn[p]
            arby = [_dot(arb_scr[ci, p], y[p]) for p in P]
            yield
            for p in P:
                o_s = ks[p][2 * C:] + arkv_scr[ci, p] - arby[p]
                o_scr[sl, lns[p]] = o_s[:C] + o_s[C:]
            yield

    def drain(gen):
        for _ in gen:
            pass

    groups = [range(g, g + RWKV_A_CHUNKS) for g in range(0, n_chunks, RWKV_A_CHUNKS)]
    drain(indep(groups[0][0]))
    for prev, cur in zip(groups[:-1], groups[1:]):
        b_gen = recur(prev)
        for _ in indep(cur[0]):
            next(b_gen, None)
        drain(b_gen)
    drain(recur(groups[-1]))

    o = o_scr[...]
    mean = per_head_mxu(o) * (1.0 / RW_HEAD)
    dlt = o - mean
    var = per_head_mxu(dlt * dlt) * (1.0 / RW_HEAD)
    on = dlt * lax.rsqrt(var + GN_EPS) * lnw_ref[...] + lnb_ref[...]
    bonus = per_head(r_ref[0] * k2_scr[...] * rk_ref[...]) * v_ref[0]
    g = _dot_hi(jax.nn.sigmoid(lora[:, 128:384]), gup_ref[...])
    o_ref[0] = ((on + bonus) * g).astype(o_ref.dtype)


def _rwkv(proj, w0, a0, k_k, k_a, r_k, ln_w, ln_b, w_up, a_up, g_up, tc):
    bsz, t, _ = proj.shape
    wl = RWKV_PAIRS * LANES
    row = lambda p: p.reshape(1, RW)
    wup_p = jnp.concatenate([w_up, jnp.zeros_like(a_up)], axis=0)
    aup_p = jnp.concatenate([jnp.zeros_like(w_up), a_up], axis=0)
    gup_p = jnp.concatenate([g_up, jnp.zeros((256 - g_up.shape[0], RW), F32)], axis=0)
    pspec = lambda off: pl.BlockSpec((1, tc, wl), lambda b, h, i: (b, i, off // wl + h))
    vec = pl.BlockSpec((1, wl), lambda b, h, i: (0, h))
    tile = pltpu.VMEM((tc, wl), F32)
    per_chain = lambda rows, dt: pltpu.VMEM((tc // RWKV_CHUNK, RWKV_PAIRS, rows, LANES), dt)
    return pl.pallas_call(
        _rwkv_kernel,
        grid=(bsz, RW // wl, t // tc),
        in_specs=[pspec(P_R), pspec(P_K), pspec(P_V),
                  pl.BlockSpec((1, tc, LORA_W), lambda b, h, i: (b, i, P_LORA // LORA_W)),
                  vec, vec, vec, vec, vec, vec, vec,
                  pl.BlockSpec((128, wl), lambda b, h, i: (0, h)),
                  pl.BlockSpec((128, wl), lambda b, h, i: (0, h)),
                  pl.BlockSpec((256, wl), lambda b, h, i: (0, h))],
        out_specs=pl.BlockSpec((1, tc, wl), lambda b, h, i: (b, i, h)),
        out_shape=jax.ShapeDtypeStruct((bsz, t, RW), BF16),
        scratch_shapes=[pltpu.VMEM((RWKV_PAIRS, LANES, LANES), F32),
                        tile, tile, tile, tile, tile, tile,
                        per_chain(LANES, BF16), per_chain(LANES, BF16), per_chain(LANES, BF16),
                        per_chain(2 * LANES, BF16), per_chain(2 * LANES, BF16),
                        per_chain(LANES, F32), per_chain(LANES, F32)],
        compiler_params=_cparams(3),
        name="rwkv",
    )(proj, proj, proj, proj, row(w0), row(a0), row(k_k), row(k_a), row(r_k), row(ln_w),
      row(ln_b), wup_p, aup_p, gup_p)


def _ssd_kernel(z_ref, x_ref, bc_ref, lora_ref, cwx_ref, cbx_ref, cwb_ref, cbb_ref, dtb_ref,
                alog_ref, dsk_ref, nw_ref, ex_ref, o_ref, extx, extb, st_scr):
    L = SSD_CHUNK

    @pl.when(pl.program_id(1) == 0)
    def _():
        extx[0:8, :] = jnp.zeros((8, extx.shape[1]), F32)
        extb[0:8, :] = jnp.zeros((8, extb.shape[1]), F32)
        st_scr[...] = jnp.zeros_like(st_scr)

    def conv(ext, cur, w_ref, b_ref):
        ext[8:8 + L, :] = cur
        acc = b_ref[...] + w_ref[0:1, :] * ext[5:5 + L, :]
        for j in range(1, SSM_CONV):
            acc = acc + w_ref[j:j + 1, :] * ext[5 + j:5 + j + L, :]
        ext[0:8, :] = ext[L:L + 8, :]
        return _silu(acc)

    xs = conv(extx, x_ref[0], cwx_ref, cbx_ref)
    bc = conv(extb, bc_ref[0], cwb_ref, cbb_ref)

    row = lax.broadcasted_iota(jnp.int32, (L, L), 0)
    col = lax.broadcasted_iota(jnp.int32, (L, L), 1)
    incl = row >= col
    tri = jnp.where(incl, 1.0, 0.0).astype(BF16)
    lane = lax.broadcasted_iota(jnp.int32, (1, LANES), 1)
    m_par = [(lane < SSM_HEAD).astype(F32), (lane >= SSM_HEAD).astype(F32)]

    ex = ex_ref[...]
    dts = jax.nn.softplus(lora_ref[0][:, LORA_DT:LORA_DT + LANES] + dtb_ref[...])
    a_slab = dts * (-jnp.exp(alog_ref[...]))
    acs_slab = _dot_ex(tri, a_slab)
    acs_t = acs_slab.T
    dtx = _dot_xe(dts, ex)
    acs_x = _dot_xe(acs_slab, ex)
    last = acs_x[L - 1:L, :]
    xdt = xs * dtx
    xdte = xdt * jnp.exp(last - acs_x)
    eacs = jnp.exp(acs_x)
    elast = jnp.exp(last)

    ys = []
    for j in range(SSM_HEADS // 2):
        g = j // (SSM_HEADS // 4)
        bm = bc[:, g * LANES:(g + 1) * LANES]
        cm = bc[:, (2 + g) * LANES:(3 + g) * LANES]
        pr = slice(j * LANES, (j + 1) * LANES)
        cb = _dot_nt(cm, bm)
        st = st_scr[j]
        y = eacs[:, pr] * _dot(cm, st)
        xp = xdt[:, pr]
        for par in range(2):
            h = 2 * j + par
            diff = acs_slab[:, h:h + 1] - acs_t[h:h + 1, :]
            mh = jnp.exp(jnp.where(incl, diff, -jnp.inf)) * cb
            y = y + _dot(mh, xp * m_par[par])
        st_scr[j] = st * elast[:, pr] + _dot_tn(bm, xdte[:, pr])
        ys.append(y)
    y = jnp.concatenate(ys, axis=1) + xs * dsk_ref[...]
    y = y * _silu(z_ref[0])
    half = SSM_W // 2
    outs = []
    for g in range(2):
        seg = y[:, g * half:(g + 1) * half]
        outs.append(seg * lax.rsqrt(jnp.mean(seg * seg, axis=-1, keepdims=True) + SSM_EPS))
    o_ref[0] = (jnp.concatenate(outs, axis=1) * nw_ref[...]).astype(o_ref.dtype)


def _ssd(proj, conv_w, conv_b, dt_bias, a_log, d_skip, norm_w):
    bsz, t, _ = proj.shape
    L = SSD_CHUNK
    pad_slab = lambda p: jnp.concatenate([p, jnp.zeros((LANES - SSM_HEADS,), F32)]).reshape(1, LANES)
    hh = jnp.arange(LANES)[:, None]
    ll = jnp.arange(SSM_W)[None, :]
    ex = (hh == ll // SSM_HEAD).astype(BF16)
    cwx, cwb = conv_w[:, :SSM_W], conv_w[:, SSM_W:]
    cbx, cbb = conv_b[:SSM_W].reshape(1, -1), conv_b[SSM_W:].reshape(1, -1)
    full = lambda a: pl.BlockSpec(a.shape, lambda b, i: (0,) * a.ndim)
    args = [cwx, cbx, cwb, cbb, pad_slab(dt_bias), pad_slab(a_log),
            jnp.repeat(d_skip, SSM_HEAD).reshape(1, SSM_W), norm_w.reshape(1, SSM_W), ex]
    return pl.pallas_call(
        _ssd_kernel,
        grid=(bsz, t // L),
        in_specs=[pl.BlockSpec((1, L, SSM_W), lambda b, i: (b, i, P_Z // SSM_W)),
                  pl.BlockSpec((1, L, SSM_W), lambda b, i: (b, i, P_XS // SSM_W)),
                  pl.BlockSpec((1, L, 512), lambda b, i: (b, i, P_BC // 512)),
                  pl.BlockSpec((1, L, LORA_W), lambda b, i: (b, i, P_LORA // LORA_W))]
                 + [full(a) for a in args],
        out_specs=pl.BlockSpec((1, L, SSM_W), lambda b, i: (b, i, 0)),
        out_shape=jax.ShapeDtypeStruct((bsz, t, SSM_W), BF16),
        scratch_shapes=[pltpu.VMEM((L + 8, SSM_W), F32), pltpu.VMEM((L + 8, 512), F32),
                        pltpu.VMEM((SSM_HEADS // 2, SSM_STATE, LANES), F32)],
        compiler_params=_cparams(2),
        name="ssd",
    )(proj, proj, proj, proj, *args)


def _outproj_kernel(orw_ref, oss_ref, x_ref, g1_ref, nw_ref, sh_ref, sc_ref, wo1_ref, wo2_ref,
                    wq_ref, h1_ref, n2_ref, q_ref):
    mix = _dot(orw_ref[0], wo1_ref[...]) + _dot(oss_ref[0], wo2_ref[...])
    h1 = x_ref[0] + g1_ref[0] * mix
    h1_ref[0] = h1
    n2 = _rms_mod(h1, nw_ref[...], sh_ref[0], sc_ref[0]).astype(BF16)
    n2_ref[0] = n2
    q_ref[0] = _dot(n2, wq_ref[...]).astype(q_ref.dtype)


def _out_proj(o_rw, o_ss, x, gate1, norm_w, shift, scale, w_out, w_query_p, tm):
    bsz, t, d = x.shape
    wo1 = w_out[:RW].astype(BF16)
    wo2 = w_out[RW:].astype(BF16)
    tok = lambda w: pl.BlockSpec((1, tm, w), lambda b, i: (b, i, 0))
    per_b = pl.BlockSpec((1, 1, d), lambda b, i: (b, 0, 0))
    const = lambda a: pl.BlockSpec(a.shape, lambda b, i: (0,) * a.ndim, pipeline_mode=pl.Buffered(1))
    nw = norm_w.reshape(1, d)
    return pl.pallas_call(
        _outproj_kernel,
        grid=(bsz, t // tm),
        in_specs=[tok(RW), tok(SSM_W), tok(d), per_b, const(nw), per_b, per_b,
                  const(wo1), const(wo2), const(w_query_p)],
        out_specs=[tok(d), tok(d), tok(d)],
        out_shape=[jax.ShapeDtypeStruct((bsz, t, d), F32), jax.ShapeDtypeStruct((bsz, t, d), BF16),
                   jax.ShapeDtypeStruct((bsz, t, d), BF16)],
        compiler_params=_cparams(2),
        name="out_proj",
    )(o_rw, o_ss, x, gate1.reshape(bsz, 1, d), nw, shift.reshape(bsz, 1, d),
      scale.reshape(bsz, 1, d), wo1, wo2, w_query_p)


_CANDS = [(i, j) for i in range(PEER_TOPK) for j in range(PEER_TOPK) if (i + 1) * (j + 1) <= PEER_TOPK]


def _route_kernel(q_ref, k1_ref, k2_ref, cnt_ref, e1_ref, rk_ref, e2_ref):
    nh, hd, K = PEER_HEADS, PEER_HALF, PEER_TOPK
    q = q_ref[...]

    def top_vals(s, want_rank):
        vals, prev = [], None
        rank = jnp.zeros_like(s) if want_rank else None
        for _ in range(K):
            if prev is None:
                cand = s
            else:
                below = s < prev
                cand = jnp.where(below, s, -jnp.inf)
                if want_rank:
                    rank = jnp.where(below, rank + 1.0, rank)
            prev = jnp.max(cand, axis=0, keepdims=True)
            vals.append(prev)
        if want_rank:
            rank = jnp.where(s < prev, rank + 1.0, rank)
        return vals, rank

    s1 = [_dot_nt(k1_ref[h], q[:, h * hd:(h + 1) * hd]) for h in range(nh)]
    s2 = [_dot_nt(k2_ref[h], q[:, (nh + h) * hd:(nh + h + 1) * hd]) for h in range(nh)]
    a_h = [top_vals(s, False)[0] for s in s1]
    b_hr = [top_vals(s, True) for s in s2]
    b_h = [x[0] for x in b_hr]
    a = [jnp.concatenate([a_h[h][i] for h in range(nh)], axis=0) for i in range(K)]
    b = [jnp.concatenate([b_h[h][i] for h in range(nh)], axis=0) for i in range(K)]

    sums = {c: a[c[0]] + b[c[1]] for c in _CANDS}
    tau = None
    for _ in range(K):
        best = None
        for c in _CANDS:
            v = sums[c] if tau is None else jnp.where(sums[c] < tau, sums[c], -jnp.inf)
            best = v if best is None else jnp.maximum(best, v)
        tau = best
    ea = [jnp.exp(ai - a[0]) for ai in a]
    eb = [jnp.exp(bj - b[0]) for bj in b]
    zsum = jnp.zeros_like(tau)
    cnt = [jnp.zeros_like(tau) for _ in range(K)]
    for (i, j) in _CANDS:
        sel = sums[(i, j)] >= tau
        zsum = zsum + jnp.where(sel, ea[i] * eb[j], 0.0)
        cnt[i] = cnt[i] + jnp.where(sel, 1.0, 0.0)
    zinv = 1.0 / zsum
    for h in range(nh):
        row = slice(h, h + 1)
        cfull = jnp.zeros_like(s1[h])
        for i in range(K):
            cfull = jnp.where(s1[h] == a_h[h][i], cnt[i][row, :], cfull)
        cnt_ref[h] = cfull
        e1_ref[h] = jnp.exp(s1[h] - a_h[h][0])
        rk_ref[h] = b_hr[h][1].astype(BF16)
        e2_ref[h] = (jnp.exp(s2[h] - b_h[h][0]) * zinv[row, :]).astype(BF16)


def _peer_route(q, sub_keys, tm):
    ntok = q.shape[0]
    nk, nh = PEER_KEYS, PEER_HEADS
    k1 = sub_keys[:, 0].astype(BF16)
    k2 = sub_keys[:, 1].astype(BF16)
    blk = pl.BlockSpec((nh, nk, tm), lambda i: (0, 0, i))
    f32_out = jax.ShapeDtypeStruct((nh, nk, ntok), F32)
    bf16_out = jax.ShapeDtypeStruct((nh, nk, ntok), BF16)
    return pl.pallas_call(
        _route_kernel,
        grid=(ntok // tm,),
        in_specs=[pl.BlockSpec((tm, q.shape[1]), lambda i: (i, 0)),
                  pl.BlockSpec(k1.shape, lambda i: (0, 0, 0)),
                  pl.BlockSpec(k2.shape, lambda i: (0, 0, 0))],
        out_specs=[blk, blk, blk, blk],
        out_shape=[f32_out, f32_out, bf16_out, bf16_out],
        compiler_params=_cparams(1),
        name="peer_route",
    )(q, k1, k2)


def _dense_kernel(n2_ref, u_ref, vt_ref, cnt_ref, e1_ref, rk_ref, e2_ref, o_ref, w_scr):
    j = pl.program_id(1)
    nk = PEER_KEYS
    ec = u_ref.shape[0]
    tm = n2_ref.shape[0]
    n_piece = 4
    pc = ec // n_piece
    zero = jnp.zeros((), BF16)
    n2 = n2_ref[...]

    def weights(p, act):
        for s in range(pc // nk):
            sl = p * (pc // nk) + s
            gsum = jnp.zeros((nk, tm), BF16)
            for h in range(PEER_HEADS):
                cb = jnp.broadcast_to(cnt_ref[h, sl:sl + 1, :], (nk, tm)).astype(BF16)
                eb = jnp.broadcast_to(e1_ref[h, sl:sl + 1, :], (nk, tm)).astype(BF16)
                gsum = gsum + jnp.where(rk_ref[h] < cb, e2_ref[h], zero) * eb
            a = act[s * nk:(s + 1) * nk, :]
            gelu = 0.5 * a * (1.0 + lax.erf(a * (2.0 ** -0.5)))
            w_scr[sl * nk:(sl + 1) * nk, :] = gsum * gelu.astype(BF16)

    acts = [_dot_nt(u_ref[p * pc:(p + 1) * pc, :], n2) for p in range(n_piece)]
    for p in range(n_piece):
        weights(p, acts[p])
        part = _dot(vt_ref[:, p * pc:(p + 1) * pc], w_scr[p * pc:(p + 1) * pc, :])
        prev = jnp.where(j == 0, 0.0, o_ref[...]) if p == 0 else o_ref[...]
        o_ref[...] = prev + part


def _peer_dense(n2, u_bf, vt_bf, cnt, e1, rk, e2, tm, ec):
    ntok, d = n2.shape
    n_exp = u_bf.shape[0]
    ns = ec // PEER_KEYS
    rows = pl.BlockSpec((PEER_HEADS, ns, tm), lambda i, j: (0, j, i))
    slab = pl.BlockSpec((PEER_HEADS, PEER_KEYS, tm), lambda i, j: (0, 0, i))
    return pl.pallas_call(
        _dense_kernel,
        grid=(ntok // tm, n_exp // ec),
        in_specs=[pl.BlockSpec((tm, d), lambda i, j: (i, 0)),
                  pl.BlockSpec((ec, d), lambda i, j: (j, 0)),
                  pl.BlockSpec((d, ec), lambda i, j: (0, j)),
                  rows, rows, slab, slab],
        out_specs=pl.BlockSpec((d, tm), lambda i, j: (0, i)),
        out_shape=jax.ShapeDtypeStruct((d, ntok), F32),
        scratch_shapes=[pltpu.VMEM((ec, tm), BF16)],
        compiler_params=_cparams(2),
        name="peer_dense",
    )(n2, u_bf, vt_bf, cnt, e1, rk, e2)


def _final_kernel(h1_ref, pt_ref, g2_ref, nw_ref, o_ref):
    h = h1_ref[0] + g2_ref[0] * pt_ref[...].T
    ms = jnp.mean(h * h, axis=-1, keepdims=True)
    o_ref[0] = (h * lax.rsqrt(ms + RMS_EPS) * nw_ref[...]).astype(o_ref.dtype)


def _final(h1, peer_t, gate2, norm_w, out_dtype, tm):
    bsz, t, d = h1.shape
    nt = t // tm
    return pl.pallas_call(
        _final_kernel,
        grid=(bsz, nt),
        in_specs=[pl.BlockSpec((1, tm, d), lambda b, i: (b, i, 0)),
                  pl.BlockSpec((d, tm), lambda b, i: (0, b * nt + i)),
                  pl.BlockSpec((1, 1, d), lambda b, i: (b, 0, 0)),
                  pl.BlockSpec((1, d), lambda b, i: (0, 0))],
        out_specs=pl.BlockSpec((1, tm, d), lambda b, i: (b, i, 0)),
        out_shape=jax.ShapeDtypeStruct((bsz, t, d), out_dtype),
        compiler_params=_cparams(2),
        name="final_norm",
    )(h1, peer_t, gate2.reshape(bsz, 1, d), norm_w.reshape(1, d))


def _pack_in_proj(w_in, mu):
    d = w_in.shape[0]
    rp = 3 * RW + 64 + 64 + 160
    zc = lambda n: jnp.zeros((d, n), w_in.dtype)
    z0 = rp
    dt0 = rp + SSM_W + SSM_W + 512
    w = jnp.concatenate([
        w_in[:, :3 * RW],
        w_in[:, z0:z0 + SSM_W],
        w_in[:, z0 + SSM_W:z0 + 2 * SSM_W],
        w_in[:, z0 + 2 * SSM_W:z0 + 2 * SSM_W + 512],
        w_in[:, 3 * RW:rp], zc(LORA_DT - 288),
        w_in[:, dt0:dt0 + SSM_HEADS], zc(LORA_W - LORA_DT - SSM_HEADS)], axis=1)
    m = jnp.concatenate([mu[:3 * RW], jnp.zeros((P_LORA - 3 * RW,), mu.dtype), mu[3 * RW:rp],
                         jnp.zeros((LORA_W - 288,), mu.dtype)])
    return w.astype(BF16), m.reshape(1, P_TOT)


def _tile(t, pref):
    return pref if t % pref == 0 else t


def kernel(x, c, ada_w, ada_b, norm1_w, w_in, rwkv_mu, rwkv_w0, rwkv_w_up, rwkv_a0, rwkv_a_up, rwkv_g_up, rwkv_k_k, rwkv_k_a, rwkv_r_k, rwkv_ln_w, rwkv_ln_b, mamba_conv_w, mamba_conv_b, mamba_dt_bias, mamba_a_log, mamba_d, mamba_norm_w, w_out, norm2_w, peer_w_query, peer_sub_keys, peer_u, peer_v, final_norm_w):
    out_dtype = x.dtype
    bsz, t, d = x.shape
    ntok = bsz * t
    assert ada_w.shape[0] == 1, "single-layer trunk"
    h = x.astype(F32)
    mod = _ada_mod(c.astype(F32), ada_w[0], ada_b[0])
    shift1, scale1, gate1, shift2, scale2, gate2 = jnp.split(mod, 6, axis=-1)
    w_packed, mu_packed = _pack_in_proj(w_in[0], rwkv_mu[0])
    proj = _in_proj(h, norm1_w[0], shift1, scale1, w_packed, mu_packed, _tile(t, 1024), 768)
    o_rw = _rwkv(proj, rwkv_w0[0], rwkv_a0[0], rwkv_k_k[0], rwkv_k_a[0], rwkv_r_k[0].reshape(-1),
                 rwkv_ln_w[0], rwkv_ln_b[0], rwkv_w_up[0], rwkv_a_up[0], rwkv_g_up[0],
                 _tile(t, 256))
    o_ss = _ssd(proj, mamba_conv_w[0], mamba_conv_b[0], mamba_dt_bias[0], mamba_a_log[0],
                mamba_d[0], mamba_norm_w[0])
    wq = peer_w_query[0].reshape(d, PEER_HEADS, 2, PEER_HALF).transpose(0, 2, 1, 3)
    wq = wq.reshape(d, 2 * PEER_HEADS * PEER_HALF).astype(BF16)
    h1, n2, q = _out_proj(o_rw, o_ss, h, gate1, norm2_w[0], shift2, scale2, w_out[0], wq,
                          _tile(t, 256))
    cnt, e1, rk, e2 = _peer_route(q.reshape(ntok, -1), peer_sub_keys[0], _tile(ntok, 256))
    peer_t = _peer_dense(n2.reshape(ntok, d), peer_u[0].astype(BF16), peer_v[0].T.astype(BF16),
                         cnt, e1, rk, e2, _tile(ntok, 512), 1024)
    return _final(h1, peer_t, gate2, final_norm_w, out_dtype, _tile(t, 512))
```

```python
import math

import jax
import jax.numpy as jnp
from jax import lax
from jax.experimental import pallas as pl
from jax.experimental.pallas import tpu as pltpu

F32 = jnp.float32
BF16 = jnp.bfloat16

D_MODEL = 2048
RW = 1024
RW_HEAD = 64
SSM_W = 1024
SSM_HEAD = 64
SSM_HEADS = 16
SSM_STATE = 128
SSM_CONV = 4
PEER_HEADS = 8
PEER_KEYS = 128
PEER_HALF = 128
PEER_TOPK = 16
RMS_EPS = 1e-6
GN_EPS = 64e-5
SSM_EPS = 1e-5

P_R, P_K, P_V, P_Z, P_XS, P_BC, P_LORA = 0, 1024, 2048, 3072, 4096, 5120, 5632
P_TOT = 6144
LORA_W = 512
LORA_DT = 384

RWKV_CHUNK = 64
SSD_CHUNK = 128
LANES = 128

VMEM_LIMIT = 56 * 1024 * 1024


def _cparams(n_axes):
    return pltpu.CompilerParams(
        dimension_semantics=("arbitrary",) * n_axes, vmem_limit_bytes=VMEM_LIMIT)


def _split3(x):
    h1 = x.astype(BF16)
    r1 = x - h1.astype(F32)
    h2 = r1.astype(BF16)
    r2 = r1 - h2.astype(F32)
    return h1, h2, r2.astype(BF16)


def _mm(a, b, dims):
    return lax.dot_general(a.astype(BF16), b.astype(BF16), (dims, ((), ())),
                           preferred_element_type=F32)


def _dot(a, b):
    return _mm(a, b, ((1,), (0,)))


def _dot_nt(a, b):
    return _mm(a, b, ((1,), (1,)))


def _dot_tn(a, b):
    return _mm(a, b, ((0,), (0,)))


def _dot_xe(x, e_bf16):
    h1, h2, h3 = _split3(x)
    return _dot(h1, e_bf16) + _dot(h2, e_bf16) + _dot(h3, e_bf16)


def _dot_ex(e_bf16, x):
    h1, h2, h3 = _split3(x)
    return _dot(e_bf16, h1) + _dot(e_bf16, h2) + _dot(e_bf16, h3)


def _dot_ex2(e_bf16, x):
    hi = x.astype(BF16)
    lo = (x - hi.astype(F32)).astype(BF16)
    return _dot(e_bf16, hi) + _dot(e_bf16, lo)


def _dot_hi(a, b):
    ah = a.astype(BF16)
    al = (a - ah.astype(F32)).astype(BF16)
    bh = b.astype(BF16)
    bl = (b - bh.astype(F32)).astype(BF16)
    return _dot(ah, bh) + _dot(al, bh) + _dot(ah, bl)


def _silu(x):
    return x * jax.nn.sigmoid(x)


def _ada_kernel(cb_ref, w_ref, b_ref, o_ref):
    w = w_ref[...]
    reps = w.shape[1] // LANES
    rows = []
    for b in range(cb_ref.shape[0]):
        cb = _silu(cb_ref[b])
        rows.append(jnp.sum(w * jnp.concatenate([cb] * reps, axis=1), axis=0, keepdims=True))
    o_ref[...] = jnp.concatenate(rows, axis=0) + b_ref[...]


def _ada_mod(c, ada_w, ada_b):
    bsz, d = c.shape
    n = ada_w.shape[1]
    tn = 1024
    cb = jnp.broadcast_to(c[:, :, None], (bsz, d, LANES))
    return pl.pallas_call(
        _ada_kernel,
        grid=(n // tn,),
        in_specs=[pl.BlockSpec((bsz, d, LANES), lambda j: (0, 0, 0)),
                  pl.BlockSpec((d, tn), lambda j: (0, j)),
                  pl.BlockSpec((1, tn), lambda j: (0, j))],
        out_specs=pl.BlockSpec((bsz, tn), lambda j: (0, j)),
        out_shape=jax.ShapeDtypeStruct((bsz, n), F32),
        compiler_params=_cparams(1),
        name="ada_mod",
    )(cb, ada_w, ada_b.reshape(1, n))


def _rms_mod(x, w, shift, scale):
    ms = jnp.mean(x * x, axis=-1, keepdims=True)
    return (x * lax.rsqrt(ms + RMS_EPS) * w) * (1.0 + scale) + shift


def _inproj_kernel(x_ref, nw_ref, sh_ref, sc_ref, w_ref, mu_ref, o_ref, n_scr, carry_scr):
    i = pl.program_id(1)
    j = pl.program_id(2)

    @pl.when(j == 0)
    def _():
        n_scr[...] = _rms_mod(x_ref[0], nw_ref[...], sh_ref[0], sc_ref[0]).astype(BF16)

    n = n_scr[...]
    tm = n.shape[0]
    pw = 256
    first = i == 0
    for c in range(w_ref.shape[1] // pw):
        cs = slice(c * pw, (c + 1) * pw)
        p = _dot(n, w_ref[:, cs])
        prev_last = jnp.where(first, 0.0, carry_scr[j, 7:8, cs])
        row = lax.broadcasted_iota(jnp.int32, p.shape, 0)
        shifted = jnp.where(row == 0, prev_last, pltpu.roll(p, 1, 0))
        carry_scr[j, :, cs] = p[tm - 8:tm, :]
        o_ref[0, :, cs] = p + (shifted - p) * mu_ref[:, cs]


def _in_proj(x, norm_w, shift, scale, w_packed, mu_packed, tm, tn):
    bsz, t, d = x.shape
    nb = P_TOT // tn
    return pl.pallas_call(
        _inproj_kernel,
        grid=(bsz, t // tm, nb),
        in_specs=[pl.BlockSpec((1, tm, d), lambda b, i, j: (b, i, 0)),
                  pl.BlockSpec((1, d), lambda b, i, j: (0, 0)),
                  pl.BlockSpec((1, 1, d), lambda b, i, j: (b, 0, 0)),
                  pl.BlockSpec((1, 1, d), lambda b, i, j: (b, 0, 0)),
                  pl.BlockSpec((d, tn), lambda b, i, j: (0, j)),
                  pl.BlockSpec((1, tn), lambda b, i, j: (0, j))],
        out_specs=pl.BlockSpec((1, tm, tn), lambda b, i, j: (b, i, j)),
        out_shape=jax.ShapeDtypeStruct((bsz, t, P_TOT), F32),
        scratch_shapes=[pltpu.VMEM((tm, d), BF16), pltpu.VMEM((nb, 8, tn), F32)],
        compiler_params=_cparams(3),
        name="in_proj",
    )(x, norm_w.reshape(1, d), shift.reshape(bsz, 1, d), scale.reshape(bsz, 1, d),
      w_packed, mu_packed)


RWKV_PAIRS = 8
RWKV_A_CHUNKS = 2


def _rwkv_kernel(r_ref, k_ref, v_ref, lora_ref, w0_ref, a0_ref, kk_ref, ka_ref, rk_ref,
                 lnw_ref, lnb_ref, wup_ref, aup_ref, gup_ref, o_ref,
                 s_scr, kkn_scr, k2_scr, beta_scr, logw_scr, cum_scr, o_scr,
                 tinv_scr, arb_scr, vs_scr, lhs_scr, upd_scr, lkv_scr, arkv_scr):
    C = RWKV_CHUNK
    tc = r_ref.shape[1]
    n_chunks = tc // C
    n_pairs = r_ref.shape[2] // LANES

    @pl.when(pl.program_id(2) == 0)
    def _():
        s_scr[...] = jnp.zeros_like(s_scr)

    lane = lax.broadcasted_iota(jnp.int32, (1, LANES), 1)
    first = lane < RW_HEAD
    m0 = first.astype(F32)
    m1 = 1.0 - m0
    row = lax.broadcasted_iota(jnp.int32, (LANES, LANES), 0)
    col = lax.broadcasted_iota(jnp.int32, (LANES, LANES), 1)
    strict = row > col
    incl = row >= col
    blk16 = (row // 16) == (col // 16)
    eye = jnp.where(row == col, 1.0, 0.0)
    r64 = lax.broadcasted_iota(jnp.int32, (C, C), 0)
    c64 = lax.broadcasted_iota(jnp.int32, (C, C), 1)
    tri64 = jnp.where(r64 >= c64, 1.0, 0.0).astype(BF16)

    def per_head(x):
        cols = []
        for p in range(n_pairs):
            xp = x[:, p * LANES:(p + 1) * LANES]
            s0 = jnp.sum(xp * m0, axis=-1, keepdims=True)
            s1 = jnp.sum(xp * m1, axis=-1, keepdims=True)
            cols.append(jnp.where(first, s0, s1))
        return jnp.concatenate(cols, axis=1)

    head_ones = jnp.where((row // RW_HEAD) == (col // RW_HEAD), 1.0, 0.0).astype(BF16)

    def per_head_mxu(x):
        return jnp.concatenate([_dot(x[:, p * LANES:(p + 1) * LANES], head_ones)
                                for p in range(n_pairs)], axis=1)

    lora = lora_ref[0]
    wa = lora[:, 0:128]
    w_lin = w0_ref[...] + _dot_hi(jnp.tanh(wa), wup_ref[...])
    logw = -math.exp(-0.5) / (1.0 + jnp.exp(-w_lin))
    a = jax.nn.sigmoid(a0_ref[...] + _dot_hi(wa, aup_ref[...]))
    k = k_ref[0]
    kk = k * kk_ref[...]
    kkn = kk * lax.rsqrt(jnp.maximum(per_head(kk * kk), 1e-24))
    kkn_scr[...] = kkn
    k2_scr[...] = k * (1.0 + (a - 1.0) * ka_ref[...])
    beta_scr[...] = kkn * a
    logw_scr[...] = logw
    for c in range(n_chunks):
        cum_scr[c * C:(c + 1) * C, :] = _dot_ex2(tri64, logw[c * C:(c + 1) * C, :])

    def stack(x):
        return jnp.concatenate([x * m0, x * m1], axis=0)

    lns = [slice(p * LANES, (p + 1) * LANES) for p in range(n_pairs)]

    def indep(first_chunk):
        chains = [(first_chunk + u, p) for u in range(RWKV_A_CHUNKS) for p in range(n_pairs)]
        sls = [slice(c * C, (c + 1) * C) for c, _ in chains]
        lnc = [lns[p] for _, p in chains]
        cum = [cum_scr[sl, ln] for sl, ln in zip(sls, lnc)]
        ig = [jnp.exp(-c) for c in cum]
        d_end = [jnp.exp(c[C - 1:C, :] - c) for c in cum]
        k2 = [k2_scr[sl, ln] for sl, ln in zip(sls, lnc)]
        beta = [beta_scr[sl, ln] for sl, ln in zip(sls, lnc)]
        v_s = [stack(v_ref[0, sl, ln]).astype(BF16) for sl, ln in zip(sls, lnc)]
        yield
        rt_s = [stack(r_ref[0, sl, ln] * jnp.exp(c)) for sl, ln, c in zip(sls, lnc, cum)]
        kap_s = [stack(kkn_scr[sl, ln] * jnp.exp(c - logw_scr[sl, ln]))
                 for sl, ln, c in zip(sls, lnc, cum)]
        bt_s = [stack(b * g) for b, g in zip(beta, ig)]
        kt_s = [stack(k * g) for k, g in zip(k2, ig)]
        lhs = [jnp.concatenate([a, b], axis=0).astype(BF16) for a, b in zip(kap_s, rt_s)]
        yield
        for (c, p), k, b, d, l, vs in zip(chains, k2, beta, d_end, lhs, v_s):
            upd_scr[c, p] = jnp.concatenate([stack(k * d), -stack(b * d)], axis=0).astype(BF16)
            lhs_scr[c, p] = l
            vs_scr[c, p] = vs
        yield
        gram = [_dot_nt(l, jnp.concatenate([b, k], axis=0)) for l, b, k in zip(lhs, bt_s, kt_s)]
        yield
        gb = [g[:, :2 * C] for g in gram]
        gk = [g[:, 2 * C:] for g in gram]
        lb = [jnp.where(strict, g[:2 * C], 0.0) for g in gb]
        lk = [jnp.where(strict, g[:2 * C], 0.0) for g in gk]
        ark = [jnp.where(incl, g[2 * C:], 0.0) for g in gk]
        for (c, p), g in zip(chains, gb):
            arb_scr[c, p] = jnp.where(incl, g[2 * C:], 0.0).astype(BF16)
        yield

        ld = [jnp.where(blk16, x, 0.0) for x in lb]
        off = [x - d for x, d in zip(lb, ld)]
        m = [-d for d in ld]
        m2 = [_dot(a, a) for a in m]
        yield
        m34 = [_dot(jnp.concatenate([a, b], axis=0), b) for a, b in zip(m, m2)]
        m3 = [x[:2 * C] for x in m34]
        m4 = [x[2 * C:] for x in m34]
        yield
        lav = [_dot(jnp.concatenate([a, b], axis=0), v) for a, b, v in zip(lk, ark, v_s)]
        lkv = [x[:2 * C] for x in lav]
        arkv = [x[2 * C:] for x in lav]
        yield
        m8 = [_dot(a, a) for a in m4]
        yield
        m12 = [_dot(a, b) for a, b in zip(m4, m8)]
        yield
        dinv = [_dot(eye + a + b + c, eye + d + e + f)
                for a, b, c, d, e, f in zip(m, m2, m3, m4, m8, m12)]
        yield
        n1 = [_dot(d, o) for d, o in zip(dinv, off)]
        yield
        n2 = [_dot(a, a) for a in n1]
        yield
        x1 = [d + _dot(a, d) for d, a in zip(dinv, n2)]
        yield
        for (c, p), x, a, lv, av in zip(chains, x1, n1, lkv, arkv):
            tinv_scr[c, p] = (x - _dot(a, x)).astype(BF16)
            lkv_scr[c, p] = lv
            arkv_scr[c, p] = av
        yield

    def recur(chunks):
        P = range(n_pairs)
        for ci in chunks:
            sl = slice(ci * C, (ci + 1) * C)
            s = [s_scr[p] for p in P]
            ks = [_dot_nt(lhs_scr[ci, p], s[p]) for p in P]
            yield
            y = [_dot(tinv_scr[ci, p], ks[p][:2 * C] + lkv_scr[ci, p]) for p in P]
            yield
            sn = [_dot_tn(jnp.concatenate([vs_scr[ci, p], y[p].astype(BF16)], axis=0), upd_scr[ci, p])
                  for p in P]
            yield
            for p in P:
                s_scr[p] = s[p] * jnp.exp(cum_scr[sl, lns[p]][C - 1:C, :]) + sn[p]
            arby = [_dot(arb_scr[ci, p], y[p]) for p in P]
            yield
            for p in P:
                o_s = ks[p][2 * C:] + arkv_scr[ci, p] - arby[p]
                o_scr[sl, lns[p]] = o_s[:C] + o_s[C:]
            yield

    def drain(gen):
        for _ in gen:
            pass

    groups = [range(g, g + RWKV_A_CHUNKS) for g in range(0, n_chunks, RWKV_A_CHUNKS)]
    drain(indep(groups[0][0]))
    for prev, cur in zip(groups[:-1], groups[1:]):
        b_gen = recur(prev)
        for _ in indep(cur[0]):
            next(b_gen, None)
        drain(b_gen)
    drain(recur(groups[-1]))

    o = o_scr[...]
    mean = per_head_mxu(o) * (1.0 / RW_HEAD)
    dlt = o - mean
    var = per_head_mxu(dlt * dlt) * (1.0 / RW_HEAD)
    on = dlt * lax.rsqrt(var + GN_EPS) * lnw_ref[...] + lnb_ref[...]
    bonus = per_head(r_ref[0] * k2_scr[...] * rk_ref[...]) * v_ref[0]
    g = _dot_hi(jax.nn.sigmoid(lora[:, 128:384]), gup_ref[...])
    o_ref[0] = ((on + bonus) * g).astype(o_ref.dtype)


def _rwkv(proj, w0, a0, k_k, k_a, r_k, ln_w, ln_b, w_up, a_up, g_up, tc):
    bsz, t, _ = proj.shape
    wl = RWKV_PAIRS * LANES
    row = lambda p: p.reshape(1, RW)
    wup_p = jnp.concatenate([w_up, jnp.zeros_like(a_up)], axis=0)
    aup_p = jnp.concatenate([jnp.zeros_like(w_up), a_up], axis=0)
    gup_p = jnp.concatenate([g_up, jnp.zeros((256 - g_up.shape[0], RW), F32)], axis=0)
    pspec = lambda off: pl.BlockSpec((1, tc, wl), lambda b, h, i: (b, i, off // wl + h))
    vec = pl.BlockSpec((1, wl), lambda b, h, i: (0, h))
    tile = pltpu.VMEM((tc, wl), F32)
    per_chain = lambda rows, dt: pltpu.VMEM((tc // RWKV_CHUNK, RWKV_PAIRS, rows, LANES), dt)
    return pl.pallas_call(
        _rwkv_kernel,
        grid=(bsz, RW // wl, t // tc),
        in_specs=[pspec(P_R), pspec(P_K), pspec(P_V),
                  pl.BlockSpec((1, tc, LORA_W), lambda b, h, i: (b, i, P_LORA // LORA_W)),
                  vec, vec, vec, vec, vec, vec, vec,
                  pl.BlockSpec((128, wl), lambda b, h, i: (0, h)),
                  pl.BlockSpec((128, wl), lambda b, h, i: (0, h)),
                  pl.BlockSpec((256, wl), lambda b, h, i: (0, h))],
        out_specs=pl.BlockSpec((1, tc, wl), lambda b, h, i: (b, i, h)),
        out_shape=jax.ShapeDtypeStruct((bsz, t, RW), BF16),
        scratch_shapes=[pltpu.VMEM((RWKV_PAIRS, LANES, LANES), F32),
                        tile, tile, tile, tile, tile, tile,
                        per_chain(LANES, BF16), per_chain(LANES, BF16), per_chain(LANES, BF16),
                        per_chain(2 * LANES, BF16), per_chain(2 * LANES, BF16),
                        per_chain(LANES, F32), per_chain(LANES, F32)],
        compiler_params=_cparams(3),
        name="rwkv",
    )(proj, proj, proj, proj, row(w0), row(a0), row(k_k), row(k_a), row(r_k), row(ln_w),
      row(ln_b), wup_p, aup_p, gup_p)


def _ssd_kernel(z_ref, x_ref, bc_ref, lora_ref, cwx_ref, cbx_ref, cwb_ref, cbb_ref, dtb_ref,
                alog_ref, dsk_ref, nw_ref, ex_ref, o_ref, extx, extb, st_scr):
    L = SSD_CHUNK

    @pl.when(pl.program_id(1) == 0)
    def _():
        extx[0:8, :] = jnp.zeros((8, extx.shape[1]), F32)
        extb[0:8, :] = jnp.zeros((8, extb.shape[1]), F32)
        st_scr[...] = jnp.zeros_like(st_scr)

    def conv(ext, cur, w_ref, b_ref):
        ext[8:8 + L, :] = cur
        acc = b_ref[...] + w_ref[0:1, :] * ext[5:5 + L, :]
        for j in range(1, SSM_CONV):
            acc = acc + w_ref[j:j + 1, :] * ext[5 + j:5 + j + L, :]
        ext[0:8, :] = ext[L:L + 8, :]
        return _silu(acc)

    xs = conv(extx, x_ref[0], cwx_ref, cbx_ref)
    bc = conv(extb, bc_ref[0], cwb_ref, cbb_ref)

    row = lax.broadcasted_iota(jnp.int32, (L, L), 0)
    col = lax.broadcasted_iota(jnp.int32, (L, L), 1)
    incl = row >= col
    tri = jnp.where(incl, 1.0, 0.0).astype(BF16)
    lane = lax.broadcasted_iota(jnp.int32, (1, LANES), 1)
    m_par = [(lane < SSM_HEAD).astype(F32), (lane >= SSM_HEAD).astype(F32)]

    ex = ex_ref[...]
    dts = jax.nn.softplus(lora_ref[0][:, LORA_DT:LORA_DT + LANES] + dtb_ref[...])
    a_slab = dts * (-jnp.exp(alog_ref[...]))
    acs_slab = _dot_ex(tri, a_slab)
    acs_t = acs_slab.T
    dtx = _dot_xe(dts, ex)
    acs_x = _dot_xe(acs_slab, ex)
    last = acs_x[L - 1:L, :]
    xdt = xs * dtx
    xdte = xdt * jnp.exp(last - acs_x)
    eacs = jnp.exp(acs_x)
    elast = jnp.exp(last)

    ys = []
    for j in range(SSM_HEADS // 2):
        g = j // (SSM_HEADS // 4)
        bm = bc[:, g * LANES:(g + 1) * LANES]
        cm = bc[:, (2 + g) * LANES:(3 + g) * LANES]
        pr = slice(j * LANES, (j + 1) * LANES)
        cb = _dot_nt(cm, bm)
        st = st_scr[j]
        y = eacs[:, pr] * _dot(cm, st)
        xp = xdt[:, pr]
        for par in range(2):
            h = 2 * j + par
            diff = acs_slab[:, h:h + 1] - acs_t[h:h + 1, :]
            mh = jnp.exp(jnp.where(incl, diff, -jnp.inf)) * cb
            y = y + _dot(mh, xp * m_par[par])
        st_scr[j] = st * elast[:, pr] + _dot_tn(bm, xdte[:, pr])
        ys.append(y)
    y = jnp.concatenate(ys, axis=1) + xs * dsk_ref[...]
    y = y * _silu(z_ref[0])
    half = SSM_W // 2
    outs = []
    for g in range(2):
        seg = y[:, g * half:(g + 1) * half]
        outs.append(seg * lax.rsqrt(jnp.mean(seg * seg, axis=-1, keepdims=True) + SSM_EPS))
    o_ref[0] = (jnp.concatenate(outs, axis=1) * nw_ref[...]).astype(o_ref.dtype)


def _ssd(proj, conv_w, conv_b, dt_bias, a_log, d_skip, norm_w):
    bsz, t, _ = proj.shape
    L = SSD_CHUNK
    pad_slab = lambda p: jnp.concatenate([p, jnp.zeros((LANES - SSM_HEADS,), F32)]).reshape(1, LANES)
    hh = jnp.arange(LANES)[:, None]
    ll = jnp.arange(SSM_W)[None, :]
    ex = (hh == ll // SSM_HEAD).astype(BF16)
    cwx, cwb = conv_w[:, :SSM_W], conv_w[:, SSM_W:]
    cbx, cbb = conv_b[:SSM_W].reshape(1, -1), conv_b[SSM_W:].reshape(1, -1)
    full = lambda a: pl.BlockSpec(a.shape, lambda b, i: (0,) * a.ndim)
    args = [cwx, cbx, cwb, cbb, pad_slab(dt_bias), pad_slab(a_log),
            jnp.repeat(d_skip, SSM_HEAD).reshape(1, SSM_W), norm_w.reshape(1, SSM_W), ex]
    return pl.pallas_call(
        _ssd_kernel,
        grid=(bsz, t // L),
        in_specs=[pl.BlockSpec((1, L, SSM_W), lambda b, i: (b, i, P_Z // SSM_W)),
                  pl.BlockSpec((1, L, SSM_W), lambda b, i: (b, i, P_XS // SSM_W)),
                  pl.BlockSpec((1, L, 512), lambda b, i: (b, i, P_BC // 512)),
                  pl.BlockSpec((1, L, LORA_W), lambda b, i: (b, i, P_LORA // LORA_W))]
                 + [full(a) for a in args],
        out_specs=pl.BlockSpec((1, L, SSM_W), lambda b, i: (b, i, 0)),
        out_shape=jax.ShapeDtypeStruct((bsz, t, SSM_W), BF16),
        scratch_shapes=[pltpu.VMEM((L + 8, SSM_W), F32), pltpu.VMEM((L + 8, 512), F32),
                        pltpu.VMEM((SSM_HEADS // 2, SSM_STATE, LANES), F32)],
        compiler_params=_cparams(2),
        name="ssd",
    )(proj, proj, proj, proj, *args)


def _outproj_kernel(orw_ref, oss_ref, x_ref, g1_ref, nw_ref, sh_ref, sc_ref, wo1_ref, wo2_ref,
                    wq_ref, h1_ref, n2_ref, q_ref):
    mix = _dot(orw_ref[0], wo1_ref[...]) + _dot(oss_ref[0], wo2_ref[...])
    h1 = x_ref[0] + g1_ref[0] * mix
    h1_ref[0] = h1
    n2 = _rms_mod(h1, nw_ref[...], sh_ref[0], sc_ref[0]).astype(BF16)
    n2_ref[0] = n2
    q_ref[0] = _dot(n2, wq_ref[...]).astype(q_ref.dtype)


def _out_proj(o_rw, o_ss, x, gate1, norm_w, shift, scale, w_out, w_query_p, tm):
    bsz, t, d = x.shape
    wo1 = w_out[:RW].astype(BF16)
    wo2 = w_out[RW:].astype(BF16)
    tok = lambda w: pl.BlockSpec((1, tm, w), lambda b, i: (b, i, 0))
    per_b = pl.BlockSpec((1, 1, d), lambda b, i: (b, 0, 0))
    const = lambda a: pl.BlockSpec(a.shape, lambda b, i: (0,) * a.ndim, pipeline_mode=pl.Buffered(1))
    nw = norm_w.reshape(1, d)
    return pl.pallas_call(
        _outproj_kernel,
        grid=(bsz, t // tm),
        in_specs=[tok(RW), tok(SSM_W), tok(d), per_b, const(nw), per_b, per_b,
                  const(wo1), const(wo2), const(w_query_p)],
        out_specs=[tok(d), tok(d), tok(d)],
        out_shape=[jax.ShapeDtypeStruct((bsz, t, d), F32), jax.ShapeDtypeStruct((bsz, t, d), BF16),
                   jax.ShapeDtypeStruct((bsz, t, d), BF16)],
        compiler_params=_cparams(2),
        name="out_proj",
    )(o_rw, o_ss, x, gate1.reshape(bsz, 1, d), nw, shift.reshape(bsz, 1, d),
      scale.reshape(bsz, 1, d), wo1, wo2, w_query_p)


_CANDS = [(i, j) for i in range(PEER_TOPK) for j in range(PEER_TOPK) if (i + 1) * (j + 1) <= PEER_TOPK]


def _oddeven_merge(lo, hi, r):
    step = r * 2
    if step < hi - lo:
        yield from _oddeven_merge(lo, hi, step)
        yield from _oddeven_merge(lo + r, hi, step)
        yield from [(i, i + r) for i in range(lo + r, hi - r, step)]
    else:
        yield (lo, lo + r)


def _oddeven_sort(lo, hi):
    if hi - lo >= 1:
        mid = lo + (hi - lo) // 2
        yield from _oddeven_sort(lo, mid)
        yield from _oddeven_sort(mid + 1, hi)
        yield from _oddeven_merge(lo, hi, 1)


_SORT16 = list(_oddeven_sort(0, PEER_TOPK - 1))
_BITONIC16 = [(i, i + s) for s in (8, 4, 2, 1) for i in range(PEER_TOPK) if not i & s]


def _route_kernel(q_ref, k1_ref, k2_ref, cnt_ref, e1_ref, rk_ref, e2_ref):
    nh, hd, K = PEER_HEADS, PEER_HALF, PEER_TOPK
    q = q_ref[...]

    sub = 8
    assert PEER_KEYS == sub * K

    def top_sorted(s):
        v = [s[sub * i:sub * (i + 1), :] for i in range(K)]

        def exchange(i, j):
            hi, lo = jnp.maximum(v[i], v[j]), jnp.minimum(v[i], v[j])
            v[i], v[j] = hi, lo

        for i, j in _SORT16:
            exchange(i, j)
        for shift in (4, 2, 1):
            w = [pltpu.roll(x, shift, 0) for x in v]
            v = [jnp.maximum(v[i], w[K - 1 - i]) for i in range(K)]
            for i, j in _BITONIC16:
                exchange(i, j)
        return v

    def rank_of(x, b):
        t1 = x < b[7]
        t2 = x < jnp.where(t1, b[11], b[3])
        t3 = x < jnp.where(t1, jnp.where(t2, b[13], b[9]), jnp.where(t2, b[5], b[1]))
        lo = jnp.where(t2, jnp.where(t3, b[6], b[4]), jnp.where(t3, b[2], b[0]))
        hi = jnp.where(t2, jnp.where(t3, b[14], b[12]), jnp.where(t3, b[10], b[8]))
        t4 = x < jnp.where(t1, hi, lo)
        t5 = x < b[15]
        one = lambda t, val: jnp.where(t, val, 0.0)
        return one(t1, 8.0) + one(t2, 4.0) + one(t3, 2.0) + one(t4, 1.0) + one(t5, 1.0)

    s1 = [_dot_nt(k1_ref[h], q[:, h * hd:(h + 1) * hd]) for h in range(nh)]
    s2 = [_dot_nt(k2_ref[h], q[:, (nh + h) * hd:(nh + h + 1) * hd]) for h in range(nh)]
    a_full = [top_sorted(s) for s in s1]
    b_full = [top_sorted(s) for s in s2]
    a_h = [[x[0:1, :] for x in v] for v in a_full]
    b_h = [[x[0:1, :] for x in v] for v in b_full]
    rank2 = [jnp.concatenate([rank_of(s2[h][sub * i:sub * (i + 1), :], b_full[h]) for i in range(K)],
                             axis=0) for h in range(nh)]
    a = [jnp.concatenate([a_h[h][i] for h in range(nh)], axis=0) for i in range(K)]
    b = [jnp.concatenate([b_h[h][i] for h in range(nh)], axis=0) for i in range(K)]

    sums = {c: a[c[0]] + b[c[1]] for c in _CANDS}
    tau = None
    for _ in range(K):
        best = None
        for c in _CANDS:
            v = sums[c] if tau is None else jnp.where(sums[c] < tau, sums[c], -jnp.inf)
            best = v if best is None else jnp.maximum(best, v)
        tau = best
    ea = [jnp.exp(ai - a[0]) for ai in a]
    eb = [jnp.exp(bj - b[0]) for bj in b]
    zsum = jnp.zeros_like(tau)
    cnt = [jnp.zeros_like(tau) for _ in range(K)]
    for (i, j) in _CANDS:
        sel = sums[(i, j)] >= tau
        zsum = zsum + jnp.where(sel, ea[i] * eb[j], 0.0)
        cnt[i] = cnt[i] + jnp.where(sel, 1.0, 0.0)
    zinv = 1.0 / zsum
    for h in range(nh):
        row = slice(h, h + 1)
        cfull = jnp.zeros_like(s1[h])
        for i in range(K):
            cfull = jnp.where(s1[h] == a_h[h][i], cnt[i][row, :], cfull)
        cnt_ref[h] = cfull
        e1_ref[h] = jnp.exp(s1[h] - a_h[h][0])
        rk_ref[h] = rank2[h].astype(BF16)
        e2_ref[h] = (jnp.exp(s2[h] - b_h[h][0]) * zinv[row, :]).astype(BF16)


def _peer_route(q, sub_keys, tm):
    ntok = q.shape[0]
    nk, nh = PEER_KEYS, PEER_HEADS
    k1 = sub_keys[:, 0].astype(BF16)
    k2 = sub_keys[:, 1].astype(BF16)
    blk = pl.BlockSpec((nh, nk, tm), lambda i: (0, 0, i))
    f32_out = jax.ShapeDtypeStruct((nh, nk, ntok), F32)
    bf16_out = jax.ShapeDtypeStruct((nh, nk, ntok), BF16)
    return pl.pallas_call(
        _route_kernel,
        grid=(ntok // tm,),
        in_specs=[pl.BlockSpec((tm, q.shape[1]), lambda i: (i, 0)),
                  pl.BlockSpec(k1.shape, lambda i: (0, 0, 0)),
                  pl.BlockSpec(k2.shape, lambda i: (0, 0, 0))],
        out_specs=[blk, blk, blk, blk],
        out_shape=[f32_out, f32_out, bf16_out, bf16_out],
        compiler_params=_cparams(1),
        name="peer_route",
    )(q, k1, k2)


def _dense_kernel(n2_ref, u_ref, vt_ref, cnt_ref, e1_ref, rk_ref, e2_ref, o_ref, w_scr):
    j = pl.program_id(1)
    nk = PEER_KEYS
    ec = u_ref.shape[0]
    tm = n2_ref.shape[0]
    n_piece = 4
    pc = ec // n_piece
    zero = jnp.zeros((), BF16)
    n2 = n2_ref[...]

    def weights(p, act):
        for s in range(pc // nk):
            sl = p * (pc // nk) + s
            gsum = jnp.zeros((nk, tm), BF16)
            for h in range(PEER_HEADS):
                cb = jnp.broadcast_to(cnt_ref[h, sl:sl + 1, :], (nk, tm)).astype(BF16)
                eb = jnp.broadcast_to(e1_ref[h, sl:sl + 1, :], (nk, tm)).astype(BF16)
                gsum = gsum + jnp.where(rk_ref[h] < cb, e2_ref[h], zero) * eb
            a = act[s * nk:(s + 1) * nk, :]
            gelu = 0.5 * a * (1.0 + lax.erf(a * (2.0 ** -0.5)))
            w_scr[sl * nk:(sl + 1) * nk, :] = gsum * gelu.astype(BF16)

    acts = [_dot_nt(u_ref[p * pc:(p + 1) * pc, :], n2) for p in range(n_piece)]
    for p in range(n_piece):
        weights(p, acts[p])
        part = _dot(vt_ref[:, p * pc:(p + 1) * pc], w_scr[p * pc:(p + 1) * pc, :])
        prev = jnp.where(j == 0, 0.0, o_ref[...]) if p == 0 else o_ref[...]
        o_ref[...] = prev + part


def _peer_dense(n2, u_bf, vt_bf, cnt, e1, rk, e2, tm, ec):
    ntok, d = n2.shape
    n_exp = u_bf.shape[0]
    ns = ec // PEER_KEYS
    rows = pl.BlockSpec((PEER_HEADS, ns, tm), lambda i, j: (0, j, i))
    slab = pl.BlockSpec((PEER_HEADS, PEER_KEYS, tm), lambda i, j: (0, 0, i))
    return pl.pallas_call(
        _dense_kernel,
        grid=(ntok // tm, n_exp // ec),
        in_specs=[pl.BlockSpec((tm, d), lambda i, j: (i, 0)),
                  pl.BlockSpec((ec, d), lambda i, j: (j, 0)),
                  pl.BlockSpec((d, ec), lambda i, j: (0, j)),
                  rows, rows, slab, slab],
        out_specs=pl.BlockSpec((d, tm), lambda i, j: (0, i)),
        out_shape=jax.ShapeDtypeStruct((d, ntok), F32),
        scratch_shapes=[pltpu.VMEM((ec, tm), BF16)],
        compiler_params=_cparams(2),
        name="peer_dense",
    )(n2, u_bf, vt_bf, cnt, e1, rk, e2)


def _final_kernel(h1_ref, pt_ref, g2_ref, nw_ref, o_ref):
    h = h1_ref[0] + g2_ref[0] * pt_ref[...].T
    ms = jnp.mean(h * h, axis=-1, keepdims=True)
    o_ref[0] = (h * lax.rsqrt(ms + RMS_EPS) * nw_ref[...]).astype(o_ref.dtype)


def _final(h1, peer_t, gate2, norm_w, out_dtype, tm):
    bsz, t, d = h1.shape
    nt = t // tm
    return pl.pallas_call(
        _final_kernel,
        grid=(bsz, nt),
        in_specs=[pl.BlockSpec((1, tm, d), lambda b, i: (b, i, 0)),
                  pl.BlockSpec((d, tm), lambda b, i: (0, b * nt + i)),
                  pl.BlockSpec((1, 1, d), lambda b, i: (b, 0, 0)),
                  pl.BlockSpec((1, d), lambda b, i: (0, 0))],
        out_specs=pl.BlockSpec((1, tm, d), lambda b, i: (b, i, 0)),
        out_shape=jax.ShapeDtypeStruct((bsz, t, d), out_dtype),
        compiler_params=_cparams(2),
        name="final_norm",
    )(h1, peer_t, gate2.reshape(bsz, 1, d), norm_w.reshape(1, d))


def _pack_in_proj(w_in, mu):
    d = w_in.shape[0]
    rp = 3 * RW + 64 + 64 + 160
    zc = lambda n: jnp.zeros((d, n), w_in.dtype)
    z0 = rp
    dt0 = rp + SSM_W + SSM_W + 512
    w = jnp.concatenate([
        w_in[:, :3 * RW],
        w_in[:, z0:z0 + SSM_W],
        w_in[:, z0 + SSM_W:z0 + 2 * SSM_W],
        w_in[:, z0 + 2 * SSM_W:z0 + 2 * SSM_W + 512],
        w_in[:, 3 * RW:rp], zc(LORA_DT - 288),
        w_in[:, dt0:dt0 + SSM_HEADS], zc(LORA_W - LORA_DT - SSM_HEADS)], axis=1)
    m = jnp.concatenate([mu[:3 * RW], jnp.zeros((P_LORA - 3 * RW,), mu.dtype), mu[3 * RW:rp],
                         jnp.zeros((LORA_W - 288,), mu.dtype)])
    return w.astype(BF16), m.reshape(1, P_TOT)


def _tile(t, pref):
    return pref if t % pref == 0 else t


def kernel(x, c, ada_w, ada_b, norm1_w, w_in, rwkv_mu, rwkv_w0, rwkv_w_up, rwkv_a0, rwkv_a_up, rwkv_g_up, rwkv_k_k, rwkv_k_a, rwkv_r_k, rwkv_ln_w, rwkv_ln_b, mamba_conv_w, mamba_conv_b, mamba_dt_bias, mamba_a_log, mamba_d, mamba_norm_w, w_out, norm2_w, peer_w_query, peer_sub_keys, peer_u, peer_v, final_norm_w):
    out_dtype = x.dtype
    bsz, t, d = x.shape
    ntok = bsz * t
    assert ada_w.shape[0] == 1, "single-layer trunk"
    h = x.astype(F32)
    mod = _ada_mod(c.astype(F32), ada_w[0], ada_b[0])
    shift1, scale1, gate1, shift2, scale2, gate2 = jnp.split(mod, 6, axis=-1)
    w_packed, mu_packed = _pack_in_proj(w_in[0], rwkv_mu[0])
    proj = _in_proj(h, norm1_w[0], shift1, scale1, w_packed, mu_packed, _tile(t, 1024), 768)
    o_rw = _rwkv(proj, rwkv_w0[0], rwkv_a0[0], rwkv_k_k[0], rwkv_k_a[0], rwkv_r_k[0].reshape(-1),
                 rwkv_ln_w[0], rwkv_ln_b[0], rwkv_w_up[0], rwkv_a_up[0], rwkv_g_up[0],
                 _tile(t, 256))
    o_ss = _ssd(proj, mamba_conv_w[0], mamba_conv_b[0], mamba_dt_bias[0], mamba_a_log[0],
                mamba_d[0], mamba_norm_w[0])
    wq = peer_w_query[0].reshape(d, PEER_HEADS, 2, PEER_HALF).transpose(0, 2, 1, 3)
    wq = wq.reshape(d, 2 * PEER_HEADS * PEER_HALF).astype(BF16)
    h1, n2, q = _out_proj(o_rw, o_ss, h, gate1, norm2_w[0], shift2, scale2, w_out[0], wq,
                          _tile(t, 256))
    cnt, e1, rk, e2 = _peer_route(q.reshape(ntok, -1), peer_sub_keys[0], _tile(ntok, 256))
    peer_t = _peer_dense(n2.reshape(ntok, d), peer_u[0].astype(BF16), peer_v[0].T.astype(BF16),
                         cnt, e1, rk, e2, _tile(ntok, 512), 1024)
    return _final(h1, peer_t, gate2, final_norm_w, out_dtype, _tile(t, 512))
```

```python
import math

import jax
import jax.numpy as jnp
from jax import lax
from jax.experimental import pallas as pl
from jax.experimental.pallas import tpu as pltpu

F32 = jnp.float32
BF16 = jnp.bfloat16

D_MODEL = 2048
RW = 1024
RW_HEAD = 64
SSM_W = 1024
SSM_HEAD = 64
SSM_HEADS = 16
SSM_STATE = 128
SSM_CONV = 4
PEER_HEADS = 8
PEER_KEYS = 128
PEER_HALF = 128
PEER_TOPK = 16
RMS_EPS = 1e-6
GN_EPS = 64e-5
SSM_EPS = 1e-5

P_R, P_K, P_V, P_Z, P_XS, P_BC, P_LORA = 0, 1024, 2048, 3072, 4096, 5120, 5632
P_TOT = 6144
LORA_W = 512
LORA_DT = 384

RWKV_CHUNK = 64
SSD_CHUNK = 128
LANES = 128

V7X_VMEM_BYTES = 64 * 1024 * 1024
VMEM_LIMIT = V7X_VMEM_BYTES - 8 * 1024 * 1024

TM_IN_PROJ, TN_IN_PROJ = 1024, 768
TM_RWKV = 256
TM_OUT_PROJ = 256
TM_ROUTE = 256
TM_DENSE, EC_DENSE = 512, 1024
TM_FINAL = 512


def _cparams(n_axes):
    return pltpu.CompilerParams(
        dimension_semantics=("arbitrary",) * n_axes, vmem_limit_bytes=VMEM_LIMIT)


def _split3(x):
    h1 = x.astype(BF16)
    r1 = x - h1.astype(F32)
    h2 = r1.astype(BF16)
    r2 = r1 - h2.astype(F32)
    return h1, h2, r2.astype(BF16)


def _mm(a, b, dims):
    return lax.dot_general(a.astype(BF16), b.astype(BF16), (dims, ((), ())),
                           preferred_element_type=F32)


def _dot(a, b):
    return _mm(a, b, ((1,), (0,)))


def _dot_nt(a, b):
    return _mm(a, b, ((1,), (1,)))


def _dot_tn(a, b):
    return _mm(a, b, ((0,), (0,)))


def _dot_xe(x, e_bf16):
    h1, h2, h3 = _split3(x)
    return _dot(h1, e_bf16) + _dot(h2, e_bf16) + _dot(h3, e_bf16)


def _dot_ex(e_bf16, x):
    h1, h2, h3 = _split3(x)
    return _dot(e_bf16, h1) + _dot(e_bf16, h2) + _dot(e_bf16, h3)


def _dot_ex2(e_bf16, x):
    hi = x.astype(BF16)
    lo = (x - hi.astype(F32)).astype(BF16)
    return _dot(e_bf16, hi) + _dot(e_bf16, lo)


def _dot_hi(a, b):
    ah = a.astype(BF16)
    al = (a - ah.astype(F32)).astype(BF16)
    bh = b.astype(BF16)
    bl = (b - bh.astype(F32)).astype(BF16)
    return _dot(ah, bh) + _dot(al, bh) + _dot(ah, bl)


def _silu(x):
    return x * jax.nn.sigmoid(x)


def _ada_kernel(cb_ref, w_ref, b_ref, o_ref):
    w = w_ref[...]
    reps = w.shape[1] // LANES
    rows = []
    for b in range(cb_ref.shape[0]):
        cb = _silu(cb_ref[b])
        rows.append(jnp.sum(w * jnp.concatenate([cb] * reps, axis=1), axis=0, keepdims=True))
    o_ref[...] = jnp.concatenate(rows, axis=0) + b_ref[...]


def _ada_mod(c, ada_w, ada_b):
    bsz, d = c.shape
    n = ada_w.shape[1]
    tn = 1024
    cb = jnp.broadcast_to(c[:, :, None], (bsz, d, LANES))
    return pl.pallas_call(
        _ada_kernel,
        grid=(n // tn,),
        in_specs=[pl.BlockSpec((bsz, d, LANES), lambda j: (0, 0, 0)),
                  pl.BlockSpec((d, tn), lambda j: (0, j)),
                  pl.BlockSpec((1, tn), lambda j: (0, j))],
        out_specs=pl.BlockSpec((bsz, tn), lambda j: (0, j)),
        out_shape=jax.ShapeDtypeStruct((bsz, n), F32),
        compiler_params=_cparams(1),
        name="ada_mod",
    )(cb, ada_w, ada_b.reshape(1, n))


def _rms_mod(x, w, shift, scale):
    ms = jnp.mean(x * x, axis=-1, keepdims=True)
    return (x * lax.rsqrt(ms + RMS_EPS) * w) * (1.0 + scale) + shift


def _inproj_kernel(x_ref, nw_ref, sh_ref, sc_ref, w_ref, mu_ref, o_ref, n_scr, carry_scr):
    i = pl.program_id(1)
    j = pl.program_id(2)

    @pl.when(j == 0)
    def _():
        n_scr[...] = _rms_mod(x_ref[0], nw_ref[...], sh_ref[0], sc_ref[0]).astype(BF16)

    n = n_scr[...]
    tm = n.shape[0]
    pw = 256
    first = i == 0
    for c in range(w_ref.shape[1] // pw):
        cs = slice(c * pw, (c + 1) * pw)
        p = _dot(n, w_ref[:, cs])
        prev_last = jnp.where(first, 0.0, carry_scr[j, 7:8, cs])
        row = lax.broadcasted_iota(jnp.int32, p.shape, 0)
        shifted = jnp.where(row == 0, prev_last, pltpu.roll(p, 1, 0))
        carry_scr[j, :, cs] = p[tm - 8:tm, :]
        o_ref[0, :, cs] = p + (shifted - p) * mu_ref[:, cs]


def _in_proj(x, norm_w, shift, scale, w_packed, mu_packed, tm, tn):
    bsz, t, d = x.shape
    nb = P_TOT // tn
    return pl.pallas_call(
        _inproj_kernel,
        grid=(bsz, t // tm, nb),
        in_specs=[pl.BlockSpec((1, tm, d), lambda b, i, j: (b, i, 0)),
                  pl.BlockSpec((1, d), lambda b, i, j: (0, 0)),
                  pl.BlockSpec((1, 1, d), lambda b, i, j: (b, 0, 0)),
                  pl.BlockSpec((1, 1, d), lambda b, i, j: (b, 0, 0)),
                  pl.BlockSpec((d, tn), lambda b, i, j: (0, j)),
                  pl.BlockSpec((1, tn), lambda b, i, j: (0, j))],
        out_specs=pl.BlockSpec((1, tm, tn), lambda b, i, j: (b, i, j)),
        out_shape=jax.ShapeDtypeStruct((bsz, t, P_TOT), F32),
        scratch_shapes=[pltpu.VMEM((tm, d), BF16), pltpu.VMEM((nb, 8, tn), F32)],
        compiler_params=_cparams(3),
        name="in_proj",
    )(x, norm_w.reshape(1, d), shift.reshape(bsz, 1, d), scale.reshape(bsz, 1, d),
      w_packed, mu_packed)


RWKV_PAIRS = 8
RWKV_A_CHUNKS = 2


def _rwkv_kernel(r_ref, k_ref, v_ref, lora_ref, w0_ref, a0_ref, kk_ref, ka_ref, rk_ref,
                 lnw_ref, lnb_ref, wup_ref, aup_ref, gup_ref, o_ref,
                 s_scr, kkn_scr, k2_scr, beta_scr, logw_scr, cum_scr, o_scr,
                 tinv_scr, arb_scr, vs_scr, lhs_scr, upd_scr, lkv_scr, arkv_scr):
    C = RWKV_CHUNK
    tc = r_ref.shape[1]
    n_chunks = tc // C
    n_pairs = r_ref.shape[2] // LANES

    @pl.when(pl.program_id(2) == 0)
    def _():
        s_scr[...] = jnp.zeros_like(s_scr)

    lane = lax.broadcasted_iota(jnp.int32, (1, LANES), 1)
    first = lane < RW_HEAD
    m0 = first.astype(F32)
    m1 = 1.0 - m0
    row = lax.broadcasted_iota(jnp.int32, (LANES, LANES), 0)
    col = lax.broadcasted_iota(jnp.int32, (LANES, LANES), 1)
    strict = row > col
    incl = row >= col
    blk16 = (row // 16) == (col // 16)
    eye = jnp.where(row == col, 1.0, 0.0)
    r64 = lax.broadcasted_iota(jnp.int32, (C, C), 0)
    c64 = lax.broadcasted_iota(jnp.int32, (C, C), 1)
    tri64 = jnp.where(r64 >= c64, 1.0, 0.0).astype(BF16)

    def per_head(x):
        cols = []
        for p in range(n_pairs):
            xp = x[:, p * LANES:(p + 1) * LANES]
            s0 = jnp.sum(xp * m0, axis=-1, keepdims=True)
            s1 = jnp.sum(xp * m1, axis=-1, keepdims=True)
            cols.append(jnp.where(first, s0, s1))
        return jnp.concatenate(cols, axis=1)

    head_ones = jnp.where((row // RW_HEAD) == (col // RW_HEAD), 1.0, 0.0).astype(BF16)

    def per_head_mxu(x):
        return jnp.concatenate([_dot(x[:, p * LANES:(p + 1) * LANES], head_ones)
                                for p in range(n_pairs)], axis=1)

    lora = lora_ref[0]
    wa = lora[:, 0:128]
    w_lin = w0_ref[...] + _dot_hi(jnp.tanh(wa), wup_ref[...])
    logw = -math.exp(-0.5) / (1.0 + jnp.exp(-w_lin))
    a = jax.nn.sigmoid(a0_ref[...] + _dot_hi(wa, aup_ref[...]))
    k = k_ref[0]
    kk = k * kk_ref[...]
    kkn = kk * lax.rsqrt(jnp.maximum(per_head(kk * kk), 1e-24))
    kkn_scr[...] = kkn
    k2_scr[...] = k * (1.0 + (a - 1.0) * ka_ref[...])
    beta_scr[...] = kkn * a
    logw_scr[...] = logw
    for c in range(n_chunks):
        cum_scr[c * C:(c + 1) * C, :] = _dot_ex2(tri64, logw[c * C:(c + 1) * C, :])

    def stack(x):
        return jnp.concatenate([x * m0, x * m1], axis=0)

    lns = [slice(p * LANES, (p + 1) * LANES) for p in range(n_pairs)]

    def indep(first_chunk):
        chains = [(first_chunk + u, p) for u in range(RWKV_A_CHUNKS) for p in range(n_pairs)]
        sls = [slice(c * C, (c + 1) * C) for c, _ in chains]
        lnc = [lns[p] for _, p in chains]
        cum = [cum_scr[sl, ln] for sl, ln in zip(sls, lnc)]
        ig = [jnp.exp(-c) for c in cum]
        d_end = [jnp.exp(c[C - 1:C, :] - c) for c in cum]
        k2 = [k2_scr[sl, ln] for sl, ln in zip(sls, lnc)]
        beta = [beta_scr[sl, ln] for sl, ln in zip(sls, lnc)]
        v_s = [stack(v_ref[0, sl, ln]).astype(BF16) for sl, ln in zip(sls, lnc)]
        yield
        rt_s = [stack(r_ref[0, sl, ln] * jnp.exp(c)) for sl, ln, c in zip(sls, lnc, cum)]
        kap_s = [stack(kkn_scr[sl, ln] * jnp.exp(c - logw_scr[sl, ln]))
                 for sl, ln, c in zip(sls, lnc, cum)]
        bt_s = [stack(b * g) for b, g in zip(beta, ig)]
        kt_s = [stack(k * g) for k, g in zip(k2, ig)]
        lhs = [jnp.concatenate([a, b], axis=0).astype(BF16) for a, b in zip(kap_s, rt_s)]
        yield
        for (c, p), k, b, d, l, vs in zip(chains, k2, beta, d_end, lhs, v_s):
            upd_scr[c, p] = jnp.concatenate([stack(k * d), -stack(b * d)], axis=0).astype(BF16)
            lhs_scr[c, p] = l
            vs_scr[c, p] = vs
        yield
        gram = [_dot_nt(l, jnp.concatenate([b, k], axis=0)) for l, b, k in zip(lhs, bt_s, kt_s)]
        yield
        gb = [g[:, :2 * C] for g in gram]
        gk = [g[:, 2 * C:] for g in gram]
        lb = [jnp.where(strict, g[:2 * C], 0.0) for g in gb]
        lk = [jnp.where(strict, g[:2 * C], 0.0) for g in gk]
        ark = [jnp.where(incl, g[2 * C:], 0.0) for g in gk]
        for (c, p), g in zip(chains, gb):
            arb_scr[c, p] = jnp.where(incl, g[2 * C:], 0.0).astype(BF16)
        yield

        ld = [jnp.where(blk16, x, 0.0) for x in lb]
        off = [x - d for x, d in zip(lb, ld)]
        m = [-d for d in ld]
        m2 = [_dot(a, a) for a in m]
        yield
        m34 = [_dot(jnp.concatenate([a, b], axis=0), b) for a, b in zip(m, m2)]
        m3 = [x[:2 * C] for x in m34]
        m4 = [x[2 * C:] for x in m34]
        yield
        lav = [_dot(jnp.concatenate([a, b], axis=0), v) for a, b, v in zip(lk, ark, v_s)]
        lkv = [x[:2 * C] for x in lav]
        arkv = [x[2 * C:] for x in lav]
        yield
        m8 = [_dot(a, a) for a in m4]
        yield
        m12 = [_dot(a, b) for a, b in zip(m4, m8)]
        yield
        dinv = [_dot(eye + a + b + c, eye + d + e + f)
                for a, b, c, d, e, f in zip(m, m2, m3, m4, m8, m12)]
        yield
        n1 = [_dot(d, o) for d, o in zip(dinv, off)]
        yield
        n2 = [_dot(a, a) for a in n1]
        yield
        x1 = [d + _dot(a, d) for d, a in zip(dinv, n2)]
        yield
        for (c, p), x, a, lv, av in zip(chains, x1, n1, lkv, arkv):
            tinv_scr[c, p] = (x - _dot(a, x)).astype(BF16)
            lkv_scr[c, p] = lv
            arkv_scr[c, p] = av
        yield

    def recur(chunks):
        P = range(n_pairs)
        for ci in chunks:
            sl = slice(ci * C, (ci + 1) * C)
            s = [s_scr[p] for p in P]
            ks = [_dot_nt(lhs_scr[ci, p], s[p]) for p in P]
            yield
            y = [_dot(tinv_scr[ci, p], ks[p][:2 * C] + lkv_scr[ci, p]) for p in P]
            yield
            sn = [_dot_tn(jnp.concatenate([vs_scr[ci, p], y[p].astype(BF16)], axis=0), upd_scr[ci, p])
                  for p in P]
            yield
            for p in P:
                s_scr[p] = s[p] * jnp.exp(cum_scr[sl, lns[p]][C - 1:C, :]) + sn[p]
            arby = [_dot(arb_scr[ci, p], y[p]) for p in P]
            yield
            for p in P:
                o_s = ks[p][2 * C:] + arkv_scr[ci, p] - arby[p]
                o_scr[sl, lns[p]] = o_s[:C] + o_s[C:]
            yield

    def drain(gen):
        for _ in gen:
            pass

    groups = [range(g, g + RWKV_A_CHUNKS) for g in range(0, n_chunks, RWKV_A_CHUNKS)]
    drain(indep(groups[0][0]))
    for prev, cur in zip(groups[:-1], groups[1:]):
        b_gen = recur(prev)
        for _ in indep(cur[0]):
            next(b_gen, None)
        drain(b_gen)
    drain(recur(groups[-1]))

    o = o_scr[...]
    mean = per_head_mxu(o) * (1.0 / RW_HEAD)
    dlt = o - mean
    var = per_head_mxu(dlt * dlt) * (1.0 / RW_HEAD)
    on = dlt * lax.rsqrt(var + GN_EPS) * lnw_ref[...] + lnb_ref[...]
    bonus = per_head(r_ref[0] * k2_scr[...] * rk_ref[...]) * v_ref[0]
    g = _dot_hi(jax.nn.sigmoid(lora[:, 128:384]), gup_ref[...])
    o_ref[0] = ((on + bonus) * g).astype(o_ref.dtype)


def _rwkv(proj, w0, a0, k_k, k_a, r_k, ln_w, ln_b, w_up, a_up, g_up, tc):
    bsz, t, _ = proj.shape
    wl = RWKV_PAIRS * LANES
    row = lambda p: p.reshape(1, RW)
    wup_p = jnp.concatenate([w_up, jnp.zeros_like(a_up)], axis=0)
    aup_p = jnp.concatenate([jnp.zeros_like(w_up), a_up], axis=0)
    gup_p = jnp.concatenate([g_up, jnp.zeros((256 - g_up.shape[0], RW), F32)], axis=0)
    pspec = lambda off: pl.BlockSpec((1, tc, wl), lambda b, h, i: (b, i, off // wl + h))
    vec = pl.BlockSpec((1, wl), lambda b, h, i: (0, h))
    tile = pltpu.VMEM((tc, wl), F32)
    per_chain = lambda rows, dt: pltpu.VMEM((tc // RWKV_CHUNK, RWKV_PAIRS, rows, LANES), dt)
    return pl.pallas_call(
        _rwkv_kernel,
        grid=(bsz, RW // wl, t // tc),
        in_specs=[pspec(P_R), pspec(P_K), pspec(P_V),
                  pl.BlockSpec((1, tc, LORA_W), lambda b, h, i: (b, i, P_LORA // LORA_W)),
                  vec, vec, vec, vec, vec, vec, vec,
                  pl.BlockSpec((128, wl), lambda b, h, i: (0, h)),
                  pl.BlockSpec((128, wl), lambda b, h, i: (0, h)),
                  pl.BlockSpec((256, wl), lambda b, h, i: (0, h))],
        out_specs=pl.BlockSpec((1, tc, wl), lambda b, h, i: (b, i, h)),
        out_shape=jax.ShapeDtypeStruct((bsz, t, RW), BF16),
        scratch_shapes=[pltpu.VMEM((RWKV_PAIRS, LANES, LANES), F32),
                        tile, tile, tile, tile, tile, tile,
                        per_chain(LANES, BF16), per_chain(LANES, BF16), per_chain(LANES, BF16),
                        per_chain(2 * LANES, BF16), per_chain(2 * LANES, BF16),
                        per_chain(LANES, F32), per_chain(LANES, F32)],
        compiler_params=_cparams(3),
        name="rwkv",
    )(proj, proj, proj, proj, row(w0), row(a0), row(k_k), row(k_a), row(r_k), row(ln_w),
      row(ln_b), wup_p, aup_p, gup_p)


def _ssd_kernel(z_ref, x_ref, bc_ref, lora_ref, cwx_ref, cbx_ref, cwb_ref, cbb_ref, dtb_ref,
                alog_ref, dsk_ref, nw_ref, ex_ref, o_ref, extx, extb, st_scr):
    L = SSD_CHUNK

    @pl.when(pl.program_id(1) == 0)
    def _():
        extx[0:8, :] = jnp.zeros((8, extx.shape[1]), F32)
        extb[0:8, :] = jnp.zeros((8, extb.shape[1]), F32)
        st_scr[...] = jnp.zeros_like(st_scr)

    def conv(ext, cur, w_ref, b_ref):
        ext[8:8 + L, :] = cur
        acc = b_ref[...] + w_ref[0:1, :] * ext[5:5 + L, :]
        for j in range(1, SSM_CONV):
            acc = acc + w_ref[j:j + 1, :] * ext[5 + j:5 + j + L, :]
        ext[0:8, :] = ext[L:L + 8, :]
        return _silu(acc)

    xs = conv(extx, x_ref[0], cwx_ref, cbx_ref)
    bc = conv(extb, bc_ref[0], cwb_ref, cbb_ref)

    row = lax.broadcasted_iota(jnp.int32, (L, L), 0)
    col = lax.broadcasted_iota(jnp.int32, (L, L), 1)
    incl = row >= col
    tri = jnp.where(incl, 1.0, 0.0).astype(BF16)
    lane = lax.broadcasted_iota(jnp.int32, (1, LANES), 1)
    m_par = [(lane < SSM_HEAD).astype(F32), (lane >= SSM_HEAD).astype(F32)]

    ex = ex_ref[...]
    dts = jax.nn.softplus(lora_ref[0][:, LORA_DT:LORA_DT + LANES] + dtb_ref[...])
    a_slab = dts * (-jnp.exp(alog_ref[...]))
    acs_slab = _dot_ex(tri, a_slab)
    acs_t = acs_slab.T
    dtx = _dot_xe(dts, ex)
    acs_x = _dot_xe(acs_slab, ex)
    last = acs_x[L - 1:L, :]
    xdt = xs * dtx
    xdte = xdt * jnp.exp(last - acs_x)
    eacs = jnp.exp(acs_x)
    elast = jnp.exp(last)

    ys = []
    for j in range(SSM_HEADS // 2):
        g = j // (SSM_HEADS // 4)
        bm = bc[:, g * LANES:(g + 1) * LANES]
        cm = bc[:, (2 + g) * LANES:(3 + g) * LANES]
        pr = slice(j * LANES, (j + 1) * LANES)
        cb = _dot_nt(cm, bm)
        st = st_scr[j]
        y = eacs[:, pr] * _dot(cm, st)
        xp = xdt[:, pr]
        for par in range(2):
            h = 2 * j + par
            diff = acs_slab[:, h:h + 1] - acs_t[h:h + 1, :]
            mh = jnp.exp(jnp.where(incl, diff, -jnp.inf)) * cb
            y = y + _dot(mh, xp * m_par[par])
        st_scr[j] = st * elast[:, pr] + _dot_tn(bm, xdte[:, pr])
        ys.append(y)
    y = jnp.concatenate(ys, axis=1) + xs * dsk_ref[...]
    y = y * _silu(z_ref[0])
    half = SSM_W // 2
    outs = []
    for g in range(2):
        seg = y[:, g * half:(g + 1) * half]
        outs.append(seg * lax.rsqrt(jnp.mean(seg * seg, axis=-1, keepdims=True) + SSM_EPS))
    o_ref[0] = (jnp.concatenate(outs, axis=1) * nw_ref[...]).astype(o_ref.dtype)


def _ssd(proj, conv_w, conv_b, dt_bias, a_log, d_skip, norm_w):
    bsz, t, _ = proj.shape
    L = SSD_CHUNK
    pad_slab = lambda p: jnp.concatenate([p, jnp.zeros((LANES - SSM_HEADS,), F32)]).reshape(1, LANES)
    hh = jnp.arange(LANES)[:, None]
    ll = jnp.arange(SSM_W)[None, :]
    ex = (hh == ll // SSM_HEAD).astype(BF16)
    cwx, cwb = conv_w[:, :SSM_W], conv_w[:, SSM_W:]
    cbx, cbb = conv_b[:SSM_W].reshape(1, -1), conv_b[SSM_W:].reshape(1, -1)
    full = lambda a: pl.BlockSpec(a.shape, lambda b, i: (0,) * a.ndim)
    args = [cwx, cbx, cwb, cbb, pad_slab(dt_bias), pad_slab(a_log),
            jnp.repeat(d_skip, SSM_HEAD).reshape(1, SSM_W), norm_w.reshape(1, SSM_W), ex]
    return pl.pallas_call(
        _ssd_kernel,
        grid=(bsz, t // L),
        in_specs=[pl.BlockSpec((1, L, SSM_W), lambda b, i: (b, i, P_Z // SSM_W)),
                  pl.BlockSpec((1, L, SSM_W), lambda b, i: (b, i, P_XS // SSM_W)),
                  pl.BlockSpec((1, L, 512), lambda b, i: (b, i, P_BC // 512)),
                  pl.BlockSpec((1, L, LORA_W), lambda b, i: (b, i, P_LORA // LORA_W))]
                 + [full(a) for a in args],
        out_specs=pl.BlockSpec((1, L, SSM_W), lambda b, i: (b, i, 0)),
        out_shape=jax.ShapeDtypeStruct((bsz, t, SSM_W), BF16),
        scratch_shapes=[pltpu.VMEM((L + 8, SSM_W), F32), pltpu.VMEM((L + 8, 512), F32),
                        pltpu.VMEM((SSM_HEADS // 2, SSM_STATE, LANES), F32)],
        compiler_params=_cparams(2),
        name="ssd",
    )(proj, proj, proj, proj, *args)


def _outproj_kernel(orw_ref, oss_ref, x_ref, g1_ref, nw_ref, sh_ref, sc_ref, wo1_ref, wo2_ref,
                    wq_ref, h1_ref, n2_ref, q_ref):
    mix = _dot(orw_ref[0], wo1_ref[...]) + _dot(oss_ref[0], wo2_ref[...])
    h1 = x_ref[0] + g1_ref[0] * mix
    h1_ref[0] = h1
    n2 = _rms_mod(h1, nw_ref[...], sh_ref[0], sc_ref[0]).astype(BF16)
    n2_ref[0] = n2
    q_ref[0] = _dot(n2, wq_ref[...]).astype(q_ref.dtype)


def _out_proj(o_rw, o_ss, x, gate1, norm_w, shift, scale, w_out, w_query_p, tm):
    bsz, t, d = x.shape
    wo1 = w_out[:RW].astype(BF16)
    wo2 = w_out[RW:].astype(BF16)
    tok = lambda w: pl.BlockSpec((1, tm, w), lambda b, i: (b, i, 0))
    per_b = pl.BlockSpec((1, 1, d), lambda b, i: (b, 0, 0))
    const = lambda a: pl.BlockSpec(a.shape, lambda b, i: (0,) * a.ndim, pipeline_mode=pl.Buffered(1))
    nw = norm_w.reshape(1, d)
    return pl.pallas_call(
        _outproj_kernel,
        grid=(bsz, t // tm),
        in_specs=[tok(RW), tok(SSM_W), tok(d), per_b, const(nw), per_b, per_b,
                  const(wo1), const(wo2), const(w_query_p)],
        out_specs=[tok(d), tok(d), tok(d)],
        out_shape=[jax.ShapeDtypeStruct((bsz, t, d), F32), jax.ShapeDtypeStruct((bsz, t, d), BF16),
                   jax.ShapeDtypeStruct((bsz, t, d), BF16)],
        compiler_params=_cparams(2),
        name="out_proj",
    )(o_rw, o_ss, x, gate1.reshape(bsz, 1, d), nw, shift.reshape(bsz, 1, d),
      scale.reshape(bsz, 1, d), wo1, wo2, w_query_p)


_CANDS = [(i, j) for i in range(PEER_TOPK) for j in range(PEER_TOPK) if (i + 1) * (j + 1) <= PEER_TOPK]


def _oddeven_merge(lo, hi, r):
    step = r * 2
    if step < hi - lo:
        yield from _oddeven_merge(lo, hi, step)
        yield from _oddeven_merge(lo + r, hi, step)
        yield from [(i, i + r) for i in range(lo + r, hi - r, step)]
    else:
        yield (lo, lo + r)


def _oddeven_sort(lo, hi):
    if hi - lo >= 1:
        mid = lo + (hi - lo) // 2
        yield from _oddeven_sort(lo, mid)
        yield from _oddeven_sort(mid + 1, hi)
        yield from _oddeven_merge(lo, hi, 1)


_SORT16 = list(_oddeven_sort(0, PEER_TOPK - 1))
_SORT_PAD = range(1 << (len(_CANDS) - 1).bit_length())
_SORT_CANDS = list(_oddeven_sort(0, len(_SORT_PAD) - 1))
_BITONIC16 = [(i, i + s) for s in (8, 4, 2, 1) for i in range(PEER_TOPK) if not i & s]


def _route_kernel(q_ref, k1_ref, k2_ref, cnt_ref, e1_ref, rk_ref, e2_ref):
    nh, hd, K = PEER_HEADS, PEER_HALF, PEER_TOPK
    q = q_ref[...]

    sub = 8
    assert PEER_KEYS == sub * K

    def top_sorted(s):
        v = [s[sub * i:sub * (i + 1), :] for i in range(K)]

        def exchange(i, j):
            hi, lo = jnp.maximum(v[i], v[j]), jnp.minimum(v[i], v[j])
            v[i], v[j] = hi, lo

        for i, j in _SORT16:
            exchange(i, j)
        for shift in (4, 2, 1):
            w = [pltpu.roll(x, shift, 0) for x in v]
            v = [jnp.maximum(v[i], w[K - 1 - i]) for i in range(K)]
            for i, j in _BITONIC16:
                exchange(i, j)
        return v

    def rank_of(x, b):
        t1 = x < b[7]
        t2 = x < jnp.where(t1, b[11], b[3])
        t3 = x < jnp.where(t1, jnp.where(t2, b[13], b[9]), jnp.where(t2, b[5], b[1]))
        lo = jnp.where(t2, jnp.where(t3, b[6], b[4]), jnp.where(t3, b[2], b[0]))
        hi = jnp.where(t2, jnp.where(t3, b[14], b[12]), jnp.where(t3, b[10], b[8]))
        t4 = x < jnp.where(t1, hi, lo)
        t5 = x < b[15]
        one = lambda t, val: jnp.where(t, val, 0.0)
        return one(t1, 8.0) + one(t2, 4.0) + one(t3, 2.0) + one(t4, 1.0) + one(t5, 1.0)

    s1 = [_dot_nt(k1_ref[h], q[:, h * hd:(h + 1) * hd]) for h in range(nh)]
    s2 = [_dot_nt(k2_ref[h], q[:, (nh + h) * hd:(nh + h + 1) * hd]) for h in range(nh)]
    a_full = [top_sorted(s) for s in s1]
    b_full = [top_sorted(s) for s in s2]
    a_h = [[x[0:1, :] for x in v] for v in a_full]
    b_h = [[x[0:1, :] for x in v] for v in b_full]
    rank2 = [jnp.concatenate([rank_of(s2[h][sub * i:sub * (i + 1), :], b_full[h]) for i in range(K)],
                             axis=0) for h in range(nh)]
    a = [jnp.concatenate([a_h[h][i] for h in range(nh)], axis=0) for i in range(K)]
    b = [jnp.concatenate([b_h[h][i] for h in range(nh)], axis=0) for i in range(K)]

    sums = {c: a[c[0]] + b[c[1]] for c in _CANDS}
    wires = [sums[c] for c in _CANDS] + [None] * (len(_SORT_PAD) - len(_CANDS))
    for i, j in _SORT_CANDS:
        hi, lo = wires[i], wires[j]
        if lo is None:
            continue
        if hi is None:
            wires[i], wires[j] = lo, None
        else:
            wires[i], wires[j] = jnp.maximum(hi, lo), jnp.minimum(hi, lo)
    tau = wires[K - 1]
    ea = [jnp.exp(ai - a[0]) for ai in a]
    eb = [jnp.exp(bj - b[0]) for bj in b]
    zsum = jnp.zeros_like(tau)
    cnt = [jnp.zeros_like(tau) for _ in range(K)]
    for (i, j) in _CANDS:
        sel = sums[(i, j)] >= tau
        zsum = zsum + jnp.where(sel, ea[i] * eb[j], 0.0)
        cnt[i] = cnt[i] + jnp.where(sel, 1.0, 0.0)
    zinv = 1.0 / zsum
    for h in range(nh):
        row = slice(h, h + 1)
        cfull = jnp.zeros_like(s1[h])
        for i in range(K):
            cfull = jnp.where(s1[h] == a_h[h][i], cnt[i][row, :], cfull)
        cnt_ref[h] = cfull
        e1_ref[h] = jnp.exp(s1[h] - a_h[h][0])
        rk_ref[h] = rank2[h].astype(BF16)
        e2_ref[h] = (jnp.exp(s2[h] - b_h[h][0]) * zinv[row, :]).astype(BF16)


def _peer_route(q, sub_keys, tm):
    ntok = q.shape[0]
    nk, nh = PEER_KEYS, PEER_HEADS
    k1 = sub_keys[:, 0].astype(BF16)
    k2 = sub_keys[:, 1].astype(BF16)
    blk = pl.BlockSpec((nh, nk, tm), lambda i: (0, 0, i))
    f32_out = jax.ShapeDtypeStruct((nh, nk, ntok), F32)
    bf16_out = jax.ShapeDtypeStruct((nh, nk, ntok), BF16)
    return pl.pallas_call(
        _route_kernel,
        grid=(ntok // tm,),
        in_specs=[pl.BlockSpec((tm, q.shape[1]), lambda i: (i, 0)),
                  pl.BlockSpec(k1.shape, lambda i: (0, 0, 0)),
                  pl.BlockSpec(k2.shape, lambda i: (0, 0, 0))],
        out_specs=[blk, blk, blk, blk],
        out_shape=[f32_out, f32_out, bf16_out, bf16_out],
        compiler_params=_cparams(1),
        name="peer_route",
    )(q, k1, k2)


def _dense_kernel(n2_ref, u_ref, vt_ref, cnt_ref, e1_ref, rk_ref, e2_ref, o_ref, w_scr):
    j = pl.program_id(1)
    nk = PEER_KEYS
    ec = u_ref.shape[0]
    tm = n2_ref.shape[0]
    n_piece = 4
    pc = ec // n_piece
    zero = jnp.zeros((), BF16)
    n2 = n2_ref[...]

    def weights(p, act):
        for s in range(pc // nk):
            sl = p * (pc // nk) + s
            gsum = jnp.zeros((nk, tm), BF16)
            for h in range(PEER_HEADS):
                cb = jnp.broadcast_to(cnt_ref[h, sl:sl + 1, :], (nk, tm)).astype(BF16)
                eb = jnp.broadcast_to(e1_ref[h, sl:sl + 1, :], (nk, tm)).astype(BF16)
                gsum = gsum + jnp.where(rk_ref[h] < cb, e2_ref[h], zero) * eb
            a = act[s * nk:(s + 1) * nk, :]
            gelu = 0.5 * a * (1.0 + lax.erf(a * (2.0 ** -0.5)))
            w_scr[sl * nk:(sl + 1) * nk, :] = gsum * gelu.astype(BF16)

    acts = [_dot_nt(u_ref[p * pc:(p + 1) * pc, :], n2) for p in range(n_piece)]
    for p in range(n_piece):
        weights(p, acts[p])
        part = _dot(vt_ref[:, p * pc:(p + 1) * pc], w_scr[p * pc:(p + 1) * pc, :])
        prev = jnp.where(j == 0, 0.0, o_ref[...]) if p == 0 else o_ref[...]
        o_ref[...] = prev + part


def _peer_dense(n2, u_bf, vt_bf, cnt, e1, rk, e2, tm, ec):
    ntok, d = n2.shape
    n_exp = u_bf.shape[0]
    ns = ec // PEER_KEYS
    rows = pl.BlockSpec((PEER_HEADS, ns, tm), lambda i, j: (0, j, i))
    slab = pl.BlockSpec((PEER_HEADS, PEER_KEYS, tm), lambda i, j: (0, 0, i))
    return pl.pallas_call(
        _dense_kernel,
        grid=(ntok // tm, n_exp // ec),
        in_specs=[pl.BlockSpec((tm, d), lambda i, j: (i, 0)),
                  pl.BlockSpec((ec, d), lambda i, j: (j, 0)),
                  pl.BlockSpec((d, ec), lambda i, j: (0, j)),
                  rows, rows, slab, slab],
        out_specs=pl.BlockSpec((d, tm), lambda i, j: (0, i)),
        out_shape=jax.ShapeDtypeStruct((d, ntok), F32),
        scratch_shapes=[pltpu.VMEM((ec, tm), BF16)],
        compiler_params=_cparams(2),
        name="peer_dense",
    )(n2, u_bf, vt_bf, cnt, e1, rk, e2)


def _final_kernel(h1_ref, pt_ref, g2_ref, nw_ref, o_ref):
    h = h1_ref[0] + g2_ref[0] * pt_ref[...].T
    ms = jnp.mean(h * h, axis=-1, keepdims=True)
    o_ref[0] = (h * lax.rsqrt(ms + RMS_EPS) * nw_ref[...]).astype(o_ref.dtype)


def _final(h1, peer_t, gate2, norm_w, out_dtype, tm):
    bsz, t, d = h1.shape
    nt = t // tm
    return pl.pallas_call(
        _final_kernel,
        grid=(bsz, nt),
        in_specs=[pl.BlockSpec((1, tm, d), lambda b, i: (b, i, 0)),
                  pl.BlockSpec((d, tm), lambda b, i: (0, b * nt + i)),
                  pl.BlockSpec((1, 1, d), lambda b, i: (b, 0, 0)),
                  pl.BlockSpec((1, d), lambda b, i: (0, 0))],
        out_specs=pl.BlockSpec((1, tm, d), lambda b, i: (b, i, 0)),
        out_shape=jax.ShapeDtypeStruct((bsz, t, d), out_dtype),
        compiler_params=_cparams(2),
        name="final_norm",
    )(h1, peer_t, gate2.reshape(bsz, 1, d), norm_w.reshape(1, d))


def _pack_in_proj(w_in, mu):
    d = w_in.shape[0]
    rp = 3 * RW + 64 + 64 + 160
    zc = lambda n: jnp.zeros((d, n), w_in.dtype)
    z0 = rp
    dt0 = rp + SSM_W + SSM_W + 512
    w = jnp.concatenate([
        w_in[:, :3 * RW],
        w_in[:, z0:z0 + SSM_W],
        w_in[:, z0 + SSM_W:z0 + 2 * SSM_W],
        w_in[:, z0 + 2 * SSM_W:z0 + 2 * SSM_W + 512],
        w_in[:, 3 * RW:rp], zc(LORA_DT - 288),
        w_in[:, dt0:dt0 + SSM_HEADS], zc(LORA_W - LORA_DT - SSM_HEADS)], axis=1)
    m = jnp.concatenate([mu[:3 * RW], jnp.zeros((P_LORA - 3 * RW,), mu.dtype), mu[3 * RW:rp],
                         jnp.zeros((LORA_W - 288,), mu.dtype)])
    return w.astype(BF16), m.reshape(1, P_TOT)


def _tile(t, pref):
    return pref if t % pref == 0 else t


def kernel(x, c, ada_w, ada_b, norm1_w, w_in, rwkv_mu, rwkv_w0, rwkv_w_up, rwkv_a0, rwkv_a_up, rwkv_g_up, rwkv_k_k, rwkv_k_a, rwkv_r_k, rwkv_ln_w, rwkv_ln_b, mamba_conv_w, mamba_conv_b, mamba_dt_bias, mamba_a_log, mamba_d, mamba_norm_w, w_out, norm2_w, peer_w_query, peer_sub_keys, peer_u, peer_v, final_norm_w):
    out_dtype = x.dtype
    bsz, t, d = x.shape
    ntok = bsz * t
    assert ada_w.shape[0] == 1, "single-layer trunk"
    h = x.astype(F32)
    mod = _ada_mod(c.astype(F32), ada_w[0], ada_b[0])
    shift1, scale1, gate1, shift2, scale2, gate2 = jnp.split(mod, 6, axis=-1)
    w_packed, mu_packed = _pack_in_proj(w_in[0], rwkv_mu[0])
    proj = _in_proj(h, norm1_w[0], shift1, scale1, w_packed, mu_packed, _tile(t, TM_IN_PROJ), TN_IN_PROJ)
    o_rw = _rwkv(proj, rwkv_w0[0], rwkv_a0[0], rwkv_k_k[0], rwkv_k_a[0], rwkv_r_k[0].reshape(-1),
                 rwkv_ln_w[0], rwkv_ln_b[0], rwkv_w_up[0], rwkv_a_up[0], rwkv_g_up[0],
                 _tile(t, TM_RWKV))
    o_ss = _ssd(proj, mamba_conv_w[0], mamba_conv_b[0], mamba_dt_bias[0], mamba_a_log[0],
                mamba_d[0], mamba_norm_w[0])
    wq = peer_w_query[0].reshape(d, PEER_HEADS, 2, PEER_HALF).transpose(0, 2, 1, 3)
    wq = wq.reshape(d, 2 * PEER_HEADS * PEER_HALF).astype(BF16)
    h1, n2, q = _out_proj(o_rw, o_ss, h, gate1, norm2_w[0], shift2, scale2, w_out[0], wq,
                          _tile(t, TM_OUT_PROJ))
    cnt, e1, rk, e2 = _peer_route(q.reshape(ntok, -1), peer_sub_keys[0], _tile(ntok, TM_ROUTE))
    peer_t = _peer_dense(n2.reshape(ntok, d), peer_u[0].astype(BF16), peer_v[0].T.astype(BF16),
                         cnt, e1, rk, e2, _tile(ntok, TM_DENSE), EC_DENSE)
    return _final(h1, peer_t, gate2, final_norm_w, out_dtype, _tile(t, TM_FINAL))
```

```python
import math

import jax
import jax.numpy as jnp
from jax import lax
from jax.experimental import pallas as pl
from jax.experimental.pallas import tpu as pltpu

F32 = jnp.float32
BF16 = jnp.bfloat16

D_MODEL = 2048
RW = 1024
RW_HEAD = 64
SSM_W = 1024
SSM_HEAD = 64
SSM_HEADS = 16
SSM_STATE = 128
SSM_CONV = 4
PEER_HEADS = 8
PEER_KEYS = 128
PEER_HALF = 128
PEER_TOPK = 16
RMS_EPS = 1e-6
GN_EPS = 64e-5
SSM_EPS = 1e-5

P_R, P_K, P_V, P_Z, P_XS, P_BC, P_LORA = 0, 1024, 2048, 3072, 4096, 5120, 5632
P_TOT = 6144
LORA_W = 512
LORA_DT = 384

RWKV_CHUNK = 64
SSD_CHUNK = 128
LANES = 128

V7X_VMEM_BYTES = 64 * 1024 * 1024
VMEM_LIMIT = V7X_VMEM_BYTES - 8 * 1024 * 1024

TM_IN_PROJ, TN_IN_PROJ = 1024, 768
TM_RWKV = 256
TM_OUT_PROJ = 256
TM_ROUTE = 256
TM_DENSE, EC_DENSE = 512, 1024
TM_FINAL = 512


def _cparams(n_axes):
    return pltpu.CompilerParams(
        dimension_semantics=("arbitrary",) * n_axes, vmem_limit_bytes=VMEM_LIMIT)


def _split3(x):
    h1 = x.astype(BF16)
    r1 = x - h1.astype(F32)
    h2 = r1.astype(BF16)
    r2 = r1 - h2.astype(F32)
    return h1, h2, r2.astype(BF16)


def _mm(a, b, dims):
    return lax.dot_general(a.astype(BF16), b.astype(BF16), (dims, ((), ())),
                           preferred_element_type=F32)


def _dot(a, b):
    return _mm(a, b, ((1,), (0,)))


def _dot_nt(a, b):
    return _mm(a, b, ((1,), (1,)))


def _dot_tn(a, b):
    return _mm(a, b, ((0,), (0,)))


def _dot_xe(x, e_bf16):
    h1, h2, h3 = _split3(x)
    return _dot(h1, e_bf16) + _dot(h2, e_bf16) + _dot(h3, e_bf16)


def _dot_ex(e_bf16, x):
    h1, h2, h3 = _split3(x)
    return _dot(e_bf16, h1) + _dot(e_bf16, h2) + _dot(e_bf16, h3)


def _dot_ex2(e_bf16, x):
    hi = x.astype(BF16)
    lo = (x - hi.astype(F32)).astype(BF16)
    return _dot(e_bf16, hi) + _dot(e_bf16, lo)


def _dot_hi(a, b):
    ah = a.astype(BF16)
    al = (a - ah.astype(F32)).astype(BF16)
    bh = b.astype(BF16)
    bl = (b - bh.astype(F32)).astype(BF16)
    return _dot(ah, bh) + _dot(al, bh) + _dot(ah, bl)


def _silu(x):
    return x * jax.nn.sigmoid(x)


def _ada_kernel(cb_ref, w_ref, b_ref, o_ref):
    w = w_ref[...]
    reps = w.shape[1] // LANES
    rows = []
    for b in range(cb_ref.shape[0]):
        cb = _silu(cb_ref[b])
        rows.append(jnp.sum(w * jnp.concatenate([cb] * reps, axis=1), axis=0, keepdims=True))
    o_ref[...] = jnp.concatenate(rows, axis=0) + b_ref[...]


def _ada_mod(c, ada_w, ada_b):
    bsz, d = c.shape
    n = ada_w.shape[1]
    tn = 1024
    cb = jnp.broadcast_to(c[:, :, None], (bsz, d, LANES))
    return pl.pallas_call(
        _ada_kernel,
        grid=(n // tn,),
        in_specs=[pl.BlockSpec((bsz, d, LANES), lambda j: (0, 0, 0)),
                  pl.BlockSpec((d, tn), lambda j: (0, j)),
                  pl.BlockSpec((1, tn), lambda j: (0, j))],
        out_specs=pl.BlockSpec((bsz, tn), lambda j: (0, j)),
        out_shape=jax.ShapeDtypeStruct((bsz, n), F32),
        compiler_params=_cparams(1),
        name="ada_mod",
    )(cb, ada_w, ada_b.reshape(1, n))


def _rms_mod(x, w, shift, scale):
    ms = jnp.mean(x * x, axis=-1, keepdims=True)
    return (x * lax.rsqrt(ms + RMS_EPS)) * (w * (1.0 + scale)) + shift


def _inproj_kernel(x_ref, nw_ref, sh_ref, sc_ref, w_ref, mu_ref, o_ref, n_scr, carry_scr):
    i = pl.program_id(1)
    j = pl.program_id(2)

    @pl.when(j == 0)
    def _():
        n_scr[...] = _rms_mod(x_ref[0], nw_ref[...], sh_ref[0], sc_ref[0]).astype(BF16)

    n = n_scr[...]
    tm = n.shape[0]
    pw = 256
    first = i == 0
    for c in range(w_ref.shape[1] // pw):
        cs = slice(c * pw, (c + 1) * pw)
        p = _dot(n, w_ref[:, cs])
        prev_last = jnp.where(first, 0.0, carry_scr[j, 7:8, cs])
        row = lax.broadcasted_iota(jnp.int32, p.shape, 0)
        shifted = jnp.where(row == 0, prev_last, pltpu.roll(p, 1, 0))
        carry_scr[j, :, cs] = p[tm - 8:tm, :]
        o_ref[0, :, cs] = p + (shifted - p) * mu_ref[:, cs]


def _in_proj(x, norm_w, shift, scale, w_packed, mu_packed, tm, tn):
    bsz, t, d = x.shape
    nb = P_TOT // tn
    return pl.pallas_call(
        _inproj_kernel,
        grid=(bsz, t // tm, nb),
        in_specs=[pl.BlockSpec((1, tm, d), lambda b, i, j: (b, i, 0)),
                  pl.BlockSpec((1, d), lambda b, i, j: (0, 0)),
                  pl.BlockSpec((1, 1, d), lambda b, i, j: (b, 0, 0)),
                  pl.BlockSpec((1, 1, d), lambda b, i, j: (b, 0, 0)),
                  pl.BlockSpec((d, tn), lambda b, i, j: (0, j)),
                  pl.BlockSpec((1, tn), lambda b, i, j: (0, j))],
        out_specs=pl.BlockSpec((1, tm, tn), lambda b, i, j: (b, i, j)),
        out_shape=jax.ShapeDtypeStruct((bsz, t, P_TOT), F32),
        scratch_shapes=[pltpu.VMEM((tm, d), BF16), pltpu.VMEM((nb, 8, tn), F32)],
        compiler_params=_cparams(3),
        name="in_proj",
    )(x, norm_w.reshape(1, d), shift.reshape(bsz, 1, d), scale.reshape(bsz, 1, d),
      w_packed, mu_packed)


RWKV_PAIRS = 8
RWKV_A_CHUNKS = 2


def _rwkv_kernel(r_ref, k_ref, v_ref, lora_ref, w0_ref, a0_ref, kk_ref, ka_ref, rk_ref,
                 lnw_ref, lnb_ref, wup_ref, aup_ref, gup_ref, o_ref,
                 s_scr, kkn_scr, k2_scr, beta_scr, logw_scr, cum_scr, o_scr,
                 tinv_scr, arb_scr, vs_scr, lhs_scr, upd_scr, lkv_scr, arkv_scr):
    C = RWKV_CHUNK
    tc = r_ref.shape[1]
    n_chunks = tc // C
    n_pairs = r_ref.shape[2] // LANES

    @pl.when(pl.program_id(2) == 0)
    def _():
        s_scr[...] = jnp.zeros_like(s_scr)

    lane = lax.broadcasted_iota(jnp.int32, (1, LANES), 1)
    first = lane < RW_HEAD
    m0 = first.astype(F32)
    m1 = 1.0 - m0
    row = lax.broadcasted_iota(jnp.int32, (LANES, LANES), 0)
    col = lax.broadcasted_iota(jnp.int32, (LANES, LANES), 1)
    strict = row > col
    incl = row >= col
    blk16 = (row // 16) == (col // 16)
    eye = jnp.where(row == col, 1.0, 0.0)
    r64 = lax.broadcasted_iota(jnp.int32, (C, C), 0)
    c64 = lax.broadcasted_iota(jnp.int32, (C, C), 1)
    tri64 = jnp.where(r64 >= c64, 1.0, 0.0).astype(BF16)

    def per_head(x):
        cols = []
        for p in range(n_pairs):
            xp = x[:, p * LANES:(p + 1) * LANES]
            s0 = jnp.sum(xp * m0, axis=-1, keepdims=True)
            s1 = jnp.sum(xp * m1, axis=-1, keepdims=True)
            cols.append(jnp.where(first, s0, s1))
        return jnp.concatenate(cols, axis=1)

    head_ones = jnp.where((row // RW_HEAD) == (col // RW_HEAD), 1.0, 0.0).astype(BF16)

    def per_head_mxu(x):
        return jnp.concatenate([_dot(x[:, p * LANES:(p + 1) * LANES], head_ones)
                                for p in range(n_pairs)], axis=1)

    lora = lora_ref[0]
    wa = lora[:, 0:128]
    w_lin = w0_ref[...] + _dot_hi(jnp.tanh(wa), wup_ref[...])
    logw = -math.exp(-0.5) / (1.0 + jnp.exp(-w_lin))
    a = jax.nn.sigmoid(a0_ref[...] + _dot_hi(wa, aup_ref[...]))
    k = k_ref[0]
    kk = k * kk_ref[...]
    kkn = kk * lax.rsqrt(jnp.maximum(per_head(kk * kk), 1e-24))
    kkn_scr[...] = kkn
    k2_scr[...] = k * (1.0 + (a - 1.0) * ka_ref[...])
    beta_scr[...] = kkn * a
    logw_scr[...] = logw
    for c in range(n_chunks):
        cum_scr[c * C:(c + 1) * C, :] = _dot_ex2(tri64, logw[c * C:(c + 1) * C, :])

    def stack(x):
        return jnp.concatenate([x * m0, x * m1], axis=0)

    lns = [slice(p * LANES, (p + 1) * LANES) for p in range(n_pairs)]

    def indep(first_chunk):
        chains = [(first_chunk + u, p) for u in range(RWKV_A_CHUNKS) for p in range(n_pairs)]
        sls = [slice(c * C, (c + 1) * C) for c, _ in chains]
        lnc = [lns[p] for _, p in chains]
        cum = [cum_scr[sl, ln] for sl, ln in zip(sls, lnc)]
        ig = [jnp.exp(-c) for c in cum]
        d_end = [jnp.exp(c[C - 1:C, :] - c) for c in cum]
        k2 = [k2_scr[sl, ln] for sl, ln in zip(sls, lnc)]
        beta = [beta_scr[sl, ln] for sl, ln in zip(sls, lnc)]
        v_s = [stack(v_ref[0, sl, ln]).astype(BF16) for sl, ln in zip(sls, lnc)]
        yield
        rt_s = [stack(r_ref[0, sl, ln] * jnp.exp(c)) for sl, ln, c in zip(sls, lnc, cum)]
        kap_s = [stack(kkn_scr[sl, ln] * jnp.exp(c - logw_scr[sl, ln]))
                 for sl, ln, c in zip(sls, lnc, cum)]
        bt_s = [stack(b * g) for b, g in zip(beta, ig)]
        kt_s = [stack(k * g) for k, g in zip(k2, ig)]
        lhs = [jnp.concatenate([a, b], axis=0).astype(BF16) for a, b in zip(kap_s, rt_s)]
        yield
        for (c, p), k, b, d, l, vs in zip(chains, k2, beta, d_end, lhs, v_s):
            upd_scr[c, p] = jnp.concatenate([stack(k * d), -stack(b * d)], axis=0).astype(BF16)
            lhs_scr[c, p] = l
            vs_scr[c, p] = vs
        yield
        gram = [_dot_nt(l, jnp.concatenate([b, k], axis=0)) for l, b, k in zip(lhs, bt_s, kt_s)]
        yield
        gb = [g[:, :2 * C] for g in gram]
        gk = [g[:, 2 * C:] for g in gram]
        lb = [jnp.where(strict, g[:2 * C], 0.0) for g in gb]
        lk = [jnp.where(strict, g[:2 * C], 0.0) for g in gk]
        ark = [jnp.where(incl, g[2 * C:], 0.0) for g in gk]
        for (c, p), g in zip(chains, gb):
            arb_scr[c, p] = jnp.where(incl, g[2 * C:], 0.0).astype(BF16)
        yield

        ld = [jnp.where(blk16, x, 0.0) for x in lb]
        off = [x - d for x, d in zip(lb, ld)]
        m = [-d for d in ld]
        m2 = [_dot(a, a) for a in m]
        yield
        m34 = [_dot(jnp.concatenate([a, b], axis=0), b) for a, b in zip(m, m2)]
        m3 = [x[:2 * C] for x in m34]
        m4 = [x[2 * C:] for x in m34]
        yield
        lav = [_dot(jnp.concatenate([a, b], axis=0), v) for a, b, v in zip(lk, ark, v_s)]
        lkv = [x[:2 * C] for x in lav]
        arkv = [x[2 * C:] for x in lav]
        yield
        m8 = [_dot(a, a) for a in m4]
        yield
        m12 = [_dot(a, b) for a, b in zip(m4, m8)]
        yield
        dinv = [_dot(eye + a + b + c, eye + d + e + f)
                for a, b, c, d, e, f in zip(m, m2, m3, m4, m8, m12)]
        yield
        n1 = [_dot(d, o) for d, o in zip(dinv, off)]
        yield
        n2 = [_dot(a, a) for a in n1]
        yield
        x1 = [d + _dot(a, d) for d, a in zip(dinv, n2)]
        yield
        for (c, p), x, a, lv, av in zip(chains, x1, n1, lkv, arkv):
            tinv_scr[c, p] = (x - _dot(a, x)).astype(BF16)
            lkv_scr[c, p] = lv
            arkv_scr[c, p] = av
        yield

    def recur(chunks):
        P = range(n_pairs)
        for ci in chunks:
            sl = slice(ci * C, (ci + 1) * C)
            s = [s_scr[p] for p in P]
            ks = [_dot_nt(lhs_scr[ci, p], s[p]) for p in P]
            yield
            y = [_dot(tinv_scr[ci, p], ks[p][:2 * C] + lkv_scr[ci, p]) for p in P]
            yield
            sn = [_dot_tn(jnp.concatenate([vs_scr[ci, p], y[p].astype(BF16)], axis=0), upd_scr[ci, p])
                  for p in P]
            yield
            for p in P:
                s_scr[p] = s[p] * jnp.exp(cum_scr[sl, lns[p]][C - 1:C, :]) + sn[p]
            arby = [_dot(arb_scr[ci, p], y[p]) for p in P]
            yield
            for p in P:
                o_s = ks[p][2 * C:] + arkv_scr[ci, p] - arby[p]
                o_scr[sl, lns[p]] = o_s[:C] + o_s[C:]
            yield

    def drain(gen):
        for _ in gen:
            pass

    groups = [range(g, g + RWKV_A_CHUNKS) for g in range(0, n_chunks, RWKV_A_CHUNKS)]
    drain(indep(groups[0][0]))
    for prev, cur in zip(groups[:-1], groups[1:]):
        b_gen = recur(prev)
        for _ in indep(cur[0]):
            next(b_gen, None)
        drain(b_gen)
    drain(recur(groups[-1]))

    o = o_scr[...]
    mean = per_head_mxu(o) * (1.0 / RW_HEAD)
    dlt = o - mean
    var = per_head_mxu(dlt * dlt) * (1.0 / RW_HEAD)
    on = dlt * lax.rsqrt(var + GN_EPS) * lnw_ref[...] + lnb_ref[...]
    bonus = per_head(r_ref[0] * k2_scr[...] * rk_ref[...]) * v_ref[0]
    g = _dot_hi(jax.nn.sigmoid(lora[:, 128:384]), gup_ref[...])
    o_ref[0] = ((on + bonus) * g).astype(o_ref.dtype)


def _rwkv(proj, w0, a0, k_k, k_a, r_k, ln_w, ln_b, w_up, a_up, g_up, tc):
    bsz, t, _ = proj.shape
    wl = RWKV_PAIRS * LANES
    row = lambda p: p.reshape(1, RW)
    wup_p = jnp.concatenate([w_up, jnp.zeros_like(a_up)], axis=0)
    aup_p = jnp.concatenate([jnp.zeros_like(w_up), a_up], axis=0)
    gup_p = jnp.concatenate([g_up, jnp.zeros((256 - g_up.shape[0], RW), F32)], axis=0)
    pspec = lambda off: pl.BlockSpec((1, tc, wl), lambda b, h, i: (b, i, off // wl + h))
    vec = pl.BlockSpec((1, wl), lambda b, h, i: (0, h))
    tile = pltpu.VMEM((tc, wl), F32)
    per_chain = lambda rows, dt: pltpu.VMEM((tc // RWKV_CHUNK, RWKV_PAIRS, rows, LANES), dt)
    return pl.pallas_call(
        _rwkv_kernel,
        grid=(bsz, RW // wl, t // tc),
        in_specs=[pspec(P_R), pspec(P_K), pspec(P_V),
                  pl.BlockSpec((1, tc, LORA_W), lambda b, h, i: (b, i, P_LORA // LORA_W)),
                  vec, vec, vec, vec, vec, vec, vec,
                  pl.BlockSpec((128, wl), lambda b, h, i: (0, h)),
                  pl.BlockSpec((128, wl), lambda b, h, i: (0, h)),
                  pl.BlockSpec((256, wl), lambda b, h, i: (0, h))],
        out_specs=pl.BlockSpec((1, tc, wl), lambda b, h, i: (b, i, h)),
        out_shape=jax.ShapeDtypeStruct((bsz, t, RW), BF16),
        scratch_shapes=[pltpu.VMEM((RWKV_PAIRS, LANES, LANES), F32),
                        tile, tile, tile, tile, tile, tile,
                        per_chain(LANES, BF16), per_chain(LANES, BF16), per_chain(LANES, BF16),
                        per_chain(2 * LANES, BF16), per_chain(2 * LANES, BF16),
                        per_chain(LANES, F32), per_chain(LANES, F32)],
        compiler_params=_cparams(3),
        name="rwkv",
    )(proj, proj, proj, proj, row(w0), row(a0), row(k_k), row(k_a), row(r_k), row(ln_w),
      row(ln_b), wup_p, aup_p, gup_p)


def _ssd_kernel(z_ref, x_ref, bc_ref, lora_ref, cwx_ref, cbx_ref, cwb_ref, cbb_ref, dtb_ref,
                alog_ref, dsk_ref, nw_ref, ex_ref, o_ref, extx, extb, st_scr):
    L = SSD_CHUNK

    @pl.when(pl.program_id(1) == 0)
    def _():
        extx[0:8, :] = jnp.zeros((8, extx.shape[1]), F32)
        extb[0:8, :] = jnp.zeros((8, extb.shape[1]), F32)
        st_scr[...] = jnp.zeros_like(st_scr)

    def conv(ext, cur, w_ref, b_ref):
        ext[8:8 + L, :] = cur
        last = SSM_CONV - 1
        acc = b_ref[...] + w_ref[last:last + 1, :] * cur
        for j in range(last):
            acc = acc + w_ref[j:j + 1, :] * ext[5 + j:5 + j + L, :]
        ext[0:8, :] = ext[L:L + 8, :]
        return _silu(acc)

    xs = conv(extx, x_ref[0], cwx_ref, cbx_ref)
    bc = conv(extb, bc_ref[0], cwb_ref, cbb_ref)

    row = lax.broadcasted_iota(jnp.int32, (L, L), 0)
    col = lax.broadcasted_iota(jnp.int32, (L, L), 1)
    incl = row >= col
    tri = jnp.where(incl, 1.0, 0.0).astype(BF16)
    lane = lax.broadcasted_iota(jnp.int32, (1, LANES), 1)
    m_par = [(lane < SSM_HEAD).astype(F32), (lane >= SSM_HEAD).astype(F32)]

    ex = ex_ref[...]
    dts = jax.nn.softplus(lora_ref[0][:, LORA_DT:LORA_DT + LANES] + dtb_ref[...])
    a_slab = dts * (-jnp.exp(alog_ref[...]))
    acs_slab = _dot_ex(tri, a_slab)
    acs_t = acs_slab.T
    dtx = _dot_xe(dts, ex)
    acs_x = _dot_xe(acs_slab, ex)
    last = acs_x[L - 1:L, :]
    xdt = xs * dtx
    xdte = xdt * jnp.exp(last - acs_x)
    eacs = jnp.exp(acs_x)
    elast = jnp.exp(last)

    ys = []
    for j in range(SSM_HEADS // 2):
        g = j // (SSM_HEADS // 4)
        bm = bc[:, g * LANES:(g + 1) * LANES]
        cm = bc[:, (2 + g) * LANES:(3 + g) * LANES]
        pr = slice(j * LANES, (j + 1) * LANES)
        cb = _dot_nt(cm, bm)
        st = st_scr[j]
        y = eacs[:, pr] * _dot(cm, st)
        xp = xdt[:, pr]
        for par in range(2):
            h = 2 * j + par
            diff = acs_slab[:, h:h + 1] - acs_t[h:h + 1, :]
            mh = jnp.exp(jnp.where(incl, diff, -jnp.inf)) * cb
            y = y + _dot(mh, xp * m_par[par])
        st_scr[j] = st * elast[:, pr] + _dot_tn(bm, xdte[:, pr])
        ys.append(y)
    y = jnp.concatenate(ys, axis=1) + xs * dsk_ref[...]
    y = y * _silu(z_ref[0])
    half = SSM_W // 2
    outs = []
    for g in range(2):
        seg = y[:, g * half:(g + 1) * half]
        outs.append(seg * lax.rsqrt(jnp.mean(seg * seg, axis=-1, keepdims=True) + SSM_EPS))
    o_ref[0] = (jnp.concatenate(outs, axis=1) * nw_ref[...]).astype(o_ref.dtype)


def _ssd(proj, conv_w, conv_b, dt_bias, a_log, d_skip, norm_w):
    bsz, t, _ = proj.shape
    L = SSD_CHUNK
    pad_slab = lambda p: jnp.concatenate([p, jnp.zeros((LANES - SSM_HEADS,), F32)]).reshape(1, LANES)
    hh = jnp.arange(LANES)[:, None]
    ll = jnp.arange(SSM_W)[None, :]
    ex = (hh == ll // SSM_HEAD).astype(BF16)
    cwx, cwb = conv_w[:, :SSM_W], conv_w[:, SSM_W:]
    cbx, cbb = conv_b[:SSM_W].reshape(1, -1), conv_b[SSM_W:].reshape(1, -1)
    full = lambda a: pl.BlockSpec(a.shape, lambda b, i: (0,) * a.ndim)
    args = [cwx, cbx, cwb, cbb, pad_slab(dt_bias), pad_slab(a_log),
            jnp.repeat(d_skip, SSM_HEAD).reshape(1, SSM_W), norm_w.reshape(1, SSM_W), ex]
    return pl.pallas_call(
        _ssd_kernel,
        grid=(bsz, t // L),
        in_specs=[pl.BlockSpec((1, L, SSM_W), lambda b, i: (b, i, P_Z // SSM_W)),
                  pl.BlockSpec((1, L, SSM_W), lambda b, i: (b, i, P_XS // SSM_W)),
                  pl.BlockSpec((1, L, 512), lambda b, i: (b, i, P_BC // 512)),
                  pl.BlockSpec((1, L, LORA_W), lambda b, i: (b, i, P_LORA // LORA_W))]
                 + [full(a) for a in args],
        out_specs=pl.BlockSpec((1, L, SSM_W), lambda b, i: (b, i, 0)),
        out_shape=jax.ShapeDtypeStruct((bsz, t, SSM_W), BF16),
        scratch_shapes=[pltpu.VMEM((L + 8, SSM_W), F32), pltpu.VMEM((L + 8, 512), F32),
                        pltpu.VMEM((SSM_HEADS // 2, SSM_STATE, LANES), F32)],
        compiler_params=_cparams(2),
        name="ssd",
    )(proj, proj, proj, proj, *args)


def _outproj_kernel(orw_ref, oss_ref, x_ref, g1_ref, nw_ref, sh_ref, sc_ref, wo1_ref, wo2_ref,
                    wq_ref, h1_ref, n2_ref, q_ref):
    mix = _dot(orw_ref[0], wo1_ref[...]) + _dot(oss_ref[0], wo2_ref[...])
    h1 = x_ref[0] + g1_ref[0] * mix
    h1_ref[0] = h1
    n2 = _rms_mod(h1, nw_ref[...], sh_ref[0], sc_ref[0]).astype(BF16)
    n2_ref[0] = n2
    q_ref[0] = _dot(n2, wq_ref[...]).astype(q_ref.dtype)


def _out_proj(o_rw, o_ss, x, gate1, norm_w, shift, scale, w_out, w_query_p, tm):
    bsz, t, d = x.shape
    wo1 = w_out[:RW].astype(BF16)
    wo2 = w_out[RW:].astype(BF16)
    tok = lambda w: pl.BlockSpec((1, tm, w), lambda b, i: (b, i, 0))
    per_b = pl.BlockSpec((1, 1, d), lambda b, i: (b, 0, 0))
    const = lambda a: pl.BlockSpec(a.shape, lambda b, i: (0,) * a.ndim, pipeline_mode=pl.Buffered(1))
    nw = norm_w.reshape(1, d)
    return pl.pallas_call(
        _outproj_kernel,
        grid=(bsz, t // tm),
        in_specs=[tok(RW), tok(SSM_W), tok(d), per_b, const(nw), per_b, per_b,
                  const(wo1), const(wo2), const(w_query_p)],
        out_specs=[tok(d), tok(d), tok(d)],
        out_shape=[jax.ShapeDtypeStruct((bsz, t, d), F32), jax.ShapeDtypeStruct((bsz, t, d), BF16),
                   jax.ShapeDtypeStruct((bsz, t, d), BF16)],
        compiler_params=_cparams(2),
        name="out_proj",
    )(o_rw, o_ss, x, gate1.reshape(bsz, 1, d), nw, shift.reshape(bsz, 1, d),
      scale.reshape(bsz, 1, d), wo1, wo2, w_query_p)


_CANDS = [(i, j) for i in range(PEER_TOPK) for j in range(PEER_TOPK) if (i + 1) * (j + 1) <= PEER_TOPK]


def _oddeven_merge(lo, hi, r):
    step = r * 2
    if step < hi - lo:
        yield from _oddeven_merge(lo, hi, step)
        yield from _oddeven_merge(lo + r, hi, step)
        yield from [(i, i + r) for i in range(lo + r, hi - r, step)]
    else:
        yield (lo, lo + r)


def _oddeven_sort(lo, hi):
    if hi - lo >= 1:
        mid = lo + (hi - lo) // 2
        yield from _oddeven_sort(lo, mid)
        yield from _oddeven_sort(mid + 1, hi)
        yield from _oddeven_merge(lo, hi, 1)


_SORT16 = list(_oddeven_sort(0, PEER_TOPK - 1))
_SORT_PAD = range(1 << (len(_CANDS) - 1).bit_length())
_SORT_CANDS = list(_oddeven_sort(0, len(_SORT_PAD) - 1))
_BITONIC16 = [(i, i + s) for s in (8, 4, 2, 1) for i in range(PEER_TOPK) if not i & s]


def _route_kernel(q_ref, k1_ref, k2_ref, cnt_ref, e1_ref, rk_ref, e2_ref):
    nh, hd, K = PEER_HEADS, PEER_HALF, PEER_TOPK
    q = q_ref[...]

    sub = 8
    assert PEER_KEYS == sub * K

    def top_sorted(s):
        v = [s[sub * i:sub * (i + 1), :] for i in range(K)]

        def exchange(i, j):
            hi, lo = jnp.maximum(v[i], v[j]), jnp.minimum(v[i], v[j])
            v[i], v[j] = hi, lo

        for i, j in _SORT16:
            exchange(i, j)
        for shift in (4, 2, 1):
            w = [pltpu.roll(x, shift, 0) for x in v]
            v = [jnp.maximum(v[i], w[K - 1 - i]) for i in range(K)]
            for i, j in _BITONIC16:
                exchange(i, j)
        return v

    def rank_of(x, b):
        t1 = x < b[7]
        t2 = x < jnp.where(t1, b[11], b[3])
        t3 = x < jnp.where(t1, jnp.where(t2, b[13], b[9]), jnp.where(t2, b[5], b[1]))
        lo = jnp.where(t2, jnp.where(t3, b[6], b[4]), jnp.where(t3, b[2], b[0]))
        hi = jnp.where(t2, jnp.where(t3, b[14], b[12]), jnp.where(t3, b[10], b[8]))
        t4 = x < jnp.where(t1, hi, lo)
        t5 = x < b[15]
        one = lambda t, val: jnp.where(t, val, 0.0)
        return one(t1, 8.0) + one(t2, 4.0) + one(t3, 2.0) + one(t4, 1.0) + one(t5, 1.0)

    s1 = [_dot_nt(k1_ref[h], q[:, h * hd:(h + 1) * hd]) for h in range(nh)]
    s2 = [_dot_nt(k2_ref[h], q[:, (nh + h) * hd:(nh + h + 1) * hd]) for h in range(nh)]
    a_full = [top_sorted(s) for s in s1]
    b_full = [top_sorted(s) for s in s2]
    a_h = [[x[0:1, :] for x in v] for v in a_full]
    b_h = [[x[0:1, :] for x in v] for v in b_full]
    rank2 = [jnp.concatenate([rank_of(s2[h][sub * i:sub * (i + 1), :], b_full[h]) for i in range(K)],
                             axis=0) for h in range(nh)]
    a = [jnp.concatenate([a_h[h][i] for h in range(nh)], axis=0) for i in range(K)]
    b = [jnp.concatenate([b_h[h][i] for h in range(nh)], axis=0) for i in range(K)]

    sums = {c: a[c[0]] + b[c[1]] for c in _CANDS}
    wires = [sums[c] for c in _CANDS] + [None] * (len(_SORT_PAD) - len(_CANDS))
    for i, j in _SORT_CANDS:
        hi, lo = wires[i], wires[j]
        if lo is None:
            continue
        if hi is None:
            wires[i], wires[j] = lo, None
        else:
            wires[i], wires[j] = jnp.maximum(hi, lo), jnp.minimum(hi, lo)
    tau = wires[K - 1]
    ea = [jnp.exp(ai - a[0]) for ai in a]
    eb = [jnp.exp(bj - b[0]) for bj in b]
    zsum = jnp.zeros_like(tau)
    cnt = [jnp.zeros_like(tau) for _ in range(K)]
    for (i, j) in _CANDS:
        sel = sums[(i, j)] >= tau
        zsum = zsum + jnp.where(sel, ea[i] * eb[j], 0.0)
        cnt[i] = cnt[i] + jnp.where(sel, 1.0, 0.0)
    zinv = 1.0 / zsum
    for h in range(nh):
        row = slice(h, h + 1)
        cfull = jnp.zeros_like(s1[h])
        for i in range(K):
            cfull = jnp.where(s1[h] == a_h[h][i], cnt[i][row, :], cfull)
        cnt_ref[h] = cfull
        e1_ref[h] = jnp.exp(s1[h] - a_h[h][0])
        rk_ref[h] = rank2[h].astype(BF16)
        e2_ref[h] = (jnp.exp(s2[h] - b_h[h][0]) * zinv[row, :]).astype(BF16)


def _peer_route(q, sub_keys, tm):
    ntok = q.shape[0]
    nk, nh = PEER_KEYS, PEER_HEADS
    k1 = sub_keys[:, 0].astype(BF16)
    k2 = sub_keys[:, 1].astype(BF16)
    blk = pl.BlockSpec((nh, nk, tm), lambda i: (0, 0, i))
    f32_out = jax.ShapeDtypeStruct((nh, nk, ntok), F32)
    bf16_out = jax.ShapeDtypeStruct((nh, nk, ntok), BF16)
    return pl.pallas_call(
        _route_kernel,
        grid=(ntok // tm,),
        in_specs=[pl.BlockSpec((tm, q.shape[1]), lambda i: (i, 0)),
                  pl.BlockSpec(k1.shape, lambda i: (0, 0, 0)),
                  pl.BlockSpec(k2.shape, lambda i: (0, 0, 0))],
        out_specs=[blk, blk, blk, blk],
        out_shape=[f32_out, f32_out, bf16_out, bf16_out],
        compiler_params=_cparams(1),
        name="peer_route",
    )(q, k1, k2)


def _dense_kernel(n2_ref, u_ref, vt_ref, cnt_ref, e1_ref, rk_ref, e2_ref, o_ref, w_scr):
    j = pl.program_id(1)
    nk = PEER_KEYS
    ec = u_ref.shape[0]
    tm = n2_ref.shape[0]
    n_piece = 4
    pc = ec // n_piece
    zero = jnp.zeros((), BF16)
    n2 = n2_ref[...]

    def weights(p, act):
        for s in range(pc // nk):
            sl = p * (pc // nk) + s
            gsum = jnp.zeros((nk, tm), BF16)
            for h in range(PEER_HEADS):
                cb = jnp.broadcast_to(cnt_ref[h, sl:sl + 1, :], (nk, tm)).astype(BF16)
                eb = jnp.broadcast_to(e1_ref[h, sl:sl + 1, :], (nk, tm)).astype(BF16)
                gsum = gsum + jnp.where(rk_ref[h] < cb, e2_ref[h], zero) * eb
            a = act[s * nk:(s + 1) * nk, :]
            gelu = 0.5 * a * (1.0 + lax.erf(a * (2.0 ** -0.5)))
            w_scr[sl * nk:(sl + 1) * nk, :] = gsum * gelu.astype(BF16)

    acts = [_dot_nt(u_ref[p * pc:(p + 1) * pc, :], n2) for p in range(n_piece)]
    for p in range(n_piece):
        weights(p, acts[p])
        part = _dot(vt_ref[:, p * pc:(p + 1) * pc], w_scr[p * pc:(p + 1) * pc, :])
        prev = jnp.where(j == 0, 0.0, o_ref[...]) if p == 0 else o_ref[...]
        o_ref[...] = prev + part


def _peer_dense(n2, u_bf, vt_bf, cnt, e1, rk, e2, tm, ec):
    ntok, d = n2.shape
    n_exp = u_bf.shape[0]
    ns = ec // PEER_KEYS
    rows = pl.BlockSpec((PEER_HEADS, ns, tm), lambda i, j: (0, j, i))
    slab = pl.BlockSpec((PEER_HEADS, PEER_KEYS, tm), lambda i, j: (0, 0, i))
    return pl.pallas_call(
        _dense_kernel,
        grid=(ntok // tm, n_exp // ec),
        in_specs=[pl.BlockSpec((tm, d), lambda i, j: (i, 0)),
                  pl.BlockSpec((ec, d), lambda i, j: (j, 0)),
                  pl.BlockSpec((d, ec), lambda i, j: (0, j)),
                  rows, rows, slab, slab],
        out_specs=pl.BlockSpec((d, tm), lambda i, j: (0, i)),
        out_shape=jax.ShapeDtypeStruct((d, ntok), F32),
        scratch_shapes=[pltpu.VMEM((ec, tm), BF16)],
        compiler_params=_cparams(2),
        name="peer_dense",
    )(n2, u_bf, vt_bf, cnt, e1, rk, e2)


def _final_kernel(h1_ref, pt_ref, g2_ref, nw_ref, o_ref):
    h = h1_ref[0] + g2_ref[0] * pt_ref[...].T
    ms = jnp.mean(h * h, axis=-1, keepdims=True)
    o_ref[0] = (h * lax.rsqrt(ms + RMS_EPS) * nw_ref[...]).astype(o_ref.dtype)


def _final(h1, peer_t, gate2, norm_w, out_dtype, tm):
    bsz, t, d = h1.shape
    nt = t // tm
    return pl.pallas_call(
        _final_kernel,
        grid=(bsz, nt),
        in_specs=[pl.BlockSpec((1, tm, d), lambda b, i: (b, i, 0)),
                  pl.BlockSpec((d, tm), lambda b, i: (0, b * nt + i)),
                  pl.BlockSpec((1, 1, d), lambda b, i: (b, 0, 0)),
                  pl.BlockSpec((1, d), lambda b, i: (0, 0))],
        out_specs=pl.BlockSpec((1, tm, d), lambda b, i: (b, i, 0)),
        out_shape=jax.ShapeDtypeStruct((bsz, t, d), out_dtype),
        compiler_params=_cparams(2),
        name="final_norm",
    )(h1, peer_t, gate2.reshape(bsz, 1, d), norm_w.reshape(1, d))


def _pack_in_proj(w_in, mu):
    d = w_in.shape[0]
    rp = 3 * RW + 64 + 64 + 160
    zc = lambda n: jnp.zeros((d, n), w_in.dtype)
    z0 = rp
    dt0 = rp + SSM_W + SSM_W + 512
    w = jnp.concatenate([
        w_in[:, :3 * RW],
        w_in[:, z0:z0 + SSM_W],
        w_in[:, z0 + SSM_W:z0 + 2 * SSM_W],
        w_in[:, z0 + 2 * SSM_W:z0 + 2 * SSM_W + 512],
        w_in[:, 3 * RW:rp], zc(LORA_DT - 288),
        w_in[:, dt0:dt0 + SSM_HEADS], zc(LORA_W - LORA_DT - SSM_HEADS)], axis=1)
    m = jnp.concatenate([mu[:3 * RW], jnp.zeros((P_LORA - 3 * RW,), mu.dtype), mu[3 * RW:rp],
                         jnp.zeros((LORA_W - 288,), mu.dtype)])
    return w.astype(BF16), m.reshape(1, P_TOT)


def _tile(t, pref):
    return pref if t % pref == 0 else t


def kernel(x, c, ada_w, ada_b, norm1_w, w_in, rwkv_mu, rwkv_w0, rwkv_w_up, rwkv_a0, rwkv_a_up, rwkv_g_up, rwkv_k_k, rwkv_k_a, rwkv_r_k, rwkv_ln_w, rwkv_ln_b, mamba_conv_w, mamba_conv_b, mamba_dt_bias, mamba_a_log, mamba_d, mamba_norm_w, w_out, norm2_w, peer_w_query, peer_sub_keys, peer_u, peer_v, final_norm_w):
    out_dtype = x.dtype
    bsz, t, d = x.shape
    ntok = bsz * t
    assert ada_w.shape[0] == 1, "single-layer trunk"
    h = x.astype(F32)
    mod = _ada_mod(c.astype(F32), ada_w[0], ada_b[0])
    shift1, scale1, gate1, shift2, scale2, gate2 = jnp.split(mod, 6, axis=-1)
    w_packed, mu_packed = _pack_in_proj(w_in[0], rwkv_mu[0])
    proj = _in_proj(h, norm1_w[0], shift1, scale1, w_packed, mu_packed, _tile(t, TM_IN_PROJ), TN_IN_PROJ)
    o_rw = _rwkv(proj, rwkv_w0[0], rwkv_a0[0], rwkv_k_k[0], rwkv_k_a[0], rwkv_r_k[0].reshape(-1),
                 rwkv_ln_w[0], rwkv_ln_b[0], rwkv_w_up[0], rwkv_a_up[0], rwkv_g_up[0],
                 _tile(t, TM_RWKV))
    o_ss = _ssd(proj, mamba_conv_w[0], mamba_conv_b[0], mamba_dt_bias[0], mamba_a_log[0],
                mamba_d[0], mamba_norm_w[0])
    wq = peer_w_query[0].reshape(d, PEER_HEADS, 2, PEER_HALF).transpose(0, 2, 1, 3)
    wq = wq.reshape(d, 2 * PEER_HEADS * PEER_HALF).astype(BF16)
    h1, n2, q = _out_proj(o_rw, o_ss, h, gate1, norm2_w[0], shift2, scale2, w_out[0], wq,
                          _tile(t, TM_OUT_PROJ))
    cnt, e1, rk, e2 = _peer_route(q.reshape(ntok, -1), peer_sub_keys[0], _tile(ntok, TM_ROUTE))
    peer_t = _peer_dense(n2.reshape(ntok, d), peer_u[0].astype(BF16), peer_v[0].T.astype(BF16),
                         cnt, e1, rk, e2, _tile(ntok, TM_DENSE), EC_DENSE)
    return _final(h1, peer_t, gate2, final_norm_w, out_dtype, _tile(t, TM_FINAL))
```

```python
import math

import jax
import jax.numpy as jnp
from jax import lax
from jax.experimental import pallas as pl
from jax.experimental.pallas import tpu as pltpu

F32 = jnp.float32
BF16 = jnp.bfloat16

D_MODEL = 2048
RW = 1024
RW_HEAD = 64
SSM_W = 1024
SSM_HEAD = 64
SSM_HEADS = 16
SSM_STATE = 128
SSM_CONV = 4
PEER_HEADS = 8
PEER_KEYS = 128
PEER_HALF = 128
PEER_TOPK = 16
RMS_EPS = 1e-6
GN_EPS = 64e-5
SSM_EPS = 1e-5

P_R, P_K, P_V, P_Z, P_XS, P_BC, P_LORA = 0, 1024, 2048, 3072, 4096, 5120, 5632
P_TOT = 6144
LORA_W = 512
LORA_DT = 384

RWKV_CHUNK = 64
SSD_CHUNK = 128
LANES = 128

V7X_VMEM_BYTES = 64 * 1024 * 1024
VMEM_LIMIT = V7X_VMEM_BYTES - 8 * 1024 * 1024

TM_IN_PROJ, TN_IN_PROJ = 1024, 768
TM_RWKV = 256
TM_OUT_PROJ = 256
TM_ROUTE = 256
TM_DENSE, EC_DENSE = 512, 1024
TM_FINAL = 512


def _cparams(n_axes):
    return pltpu.CompilerParams(
        dimension_semantics=("arbitrary",) * n_axes, vmem_limit_bytes=VMEM_LIMIT)


def _split3(x):
    h1 = x.astype(BF16)
    r1 = x - h1.astype(F32)
    h2 = r1.astype(BF16)
    r2 = r1 - h2.astype(F32)
    return h1, h2, r2.astype(BF16)


def _mm(a, b, dims):
    return lax.dot_general(a.astype(BF16), b.astype(BF16), (dims, ((), ())),
                           preferred_element_type=F32)


def _dot(a, b):
    return _mm(a, b, ((1,), (0,)))


def _dot_nt(a, b):
    return _mm(a, b, ((1,), (1,)))


def _dot_tn(a, b):
    return _mm(a, b, ((0,), (0,)))


def _dot_xe(x, e_bf16):
    h1, h2, h3 = _split3(x)
    return _dot(h1, e_bf16) + _dot(h2, e_bf16) + _dot(h3, e_bf16)


def _dot_ex(e_bf16, x):
    h1, h2, h3 = _split3(x)
    return _dot(e_bf16, h1) + _dot(e_bf16, h2) + _dot(e_bf16, h3)


def _dot_ex2(e_bf16, x):
    hi = x.astype(BF16)
    lo = (x - hi.astype(F32)).astype(BF16)
    return _dot(e_bf16, hi) + _dot(e_bf16, lo)


def _dot_hi(a, b):
    ah = a.astype(BF16)
    al = (a - ah.astype(F32)).astype(BF16)
    bh = b.astype(BF16)
    bl = (b - bh.astype(F32)).astype(BF16)
    return _dot(ah, bh) + _dot(al, bh) + _dot(ah, bl)


def _silu(x):
    return x * jax.nn.sigmoid(x)


def _ada_kernel(cb_ref, w_ref, b_ref, o_ref):
    w = w_ref[...]
    reps = w.shape[1] // LANES
    rows = []
    for b in range(cb_ref.shape[0]):
        cb = _silu(cb_ref[b])
        rows.append(jnp.sum(w * jnp.concatenate([cb] * reps, axis=1), axis=0, keepdims=True))
    o_ref[...] = jnp.concatenate(rows, axis=0) + b_ref[...]


def _ada_mod(c, ada_w, ada_b):
    bsz, d = c.shape
    n = ada_w.shape[1]
    tn = 1024
    cb = jnp.broadcast_to(c[:, :, None], (bsz, d, LANES))
    return pl.pallas_call(
        _ada_kernel,
        grid=(n // tn,),
        in_specs=[pl.BlockSpec((bsz, d, LANES), lambda j: (0, 0, 0)),
                  pl.BlockSpec((d, tn), lambda j: (0, j)),
                  pl.BlockSpec((1, tn), lambda j: (0, j))],
        out_specs=pl.BlockSpec((bsz, tn), lambda j: (0, j)),
        out_shape=jax.ShapeDtypeStruct((bsz, n), F32),
        compiler_params=_cparams(1),
        name="ada_mod",
    )(cb, ada_w, ada_b.reshape(1, n))


def _rms_mod(x, w, shift, scale):
    ms = jnp.mean(x * x, axis=-1, keepdims=True)
    return (x * lax.rsqrt(ms + RMS_EPS)) * (w * (1.0 + scale)) + shift


def _inproj_kernel(x_ref, nw_ref, sh_ref, sc_ref, w_ref, mu_ref, o_ref, n_scr, carry_scr):
    i = pl.program_id(1)
    j = pl.program_id(2)

    @pl.when(j == 0)
    def _():
        n_scr[...] = _rms_mod(x_ref[0], nw_ref[...], sh_ref[0], sc_ref[0]).astype(BF16)

    n = n_scr[...]
    tm = n.shape[0]
    pw = 256
    first = i == 0
    for c in range(w_ref.shape[1] // pw):
        cs = slice(c * pw, (c + 1) * pw)
        p = _dot(n, w_ref[:, cs])
        prev_last = jnp.where(first, 0.0, carry_scr[j, 7:8, cs])
        row = lax.broadcasted_iota(jnp.int32, p.shape, 0)
        shifted = jnp.where(row == 0, prev_last, pltpu.roll(p, 1, 0))
        carry_scr[j, :, cs] = p[tm - 8:tm, :]
        o_ref[0, :, cs] = p + (shifted - p) * mu_ref[:, cs]


def _in_proj(x, norm_w, shift, scale, w_packed, mu_packed, tm, tn):
    bsz, t, d = x.shape
    nb = P_TOT // tn
    return pl.pallas_call(
        _inproj_kernel,
        grid=(bsz, t // tm, nb),
        in_specs=[pl.BlockSpec((1, tm, d), lambda b, i, j: (b, i, 0)),
                  pl.BlockSpec((1, d), lambda b, i, j: (0, 0)),
                  pl.BlockSpec((1, 1, d), lambda b, i, j: (b, 0, 0)),
                  pl.BlockSpec((1, 1, d), lambda b, i, j: (b, 0, 0)),
                  pl.BlockSpec((d, tn), lambda b, i, j: (0, j)),
                  pl.BlockSpec((1, tn), lambda b, i, j: (0, j))],
        out_specs=pl.BlockSpec((1, tm, tn), lambda b, i, j: (b, i, j)),
        out_shape=jax.ShapeDtypeStruct((bsz, t, P_TOT), F32),
        scratch_shapes=[pltpu.VMEM((tm, d), BF16), pltpu.VMEM((nb, 8, tn), F32)],
        compiler_params=_cparams(3),
        name="in_proj",
    )(x, norm_w.reshape(1, d), shift.reshape(bsz, 1, d), scale.reshape(bsz, 1, d),
      w_packed, mu_packed)


RWKV_PAIRS = 8
RWKV_A_CHUNKS = 2


def _rwkv_kernel(r_ref, k_ref, v_ref, lora_ref, w0_ref, a0_ref, kk_ref, ka_ref, rk_ref,
                 lnw_ref, lnb_ref, wup_ref, aup_ref, gup_ref, o_ref,
                 s_scr, kkn_scr, k2_scr, beta_scr, logw_scr, cum_scr, o_scr,
                 tinv_scr, arb_scr, vs_scr, lhs_scr, upd_scr, lkv_scr, arkv_scr):
    C = RWKV_CHUNK
    tc = r_ref.shape[1]
    n_chunks = tc // C
    n_pairs = r_ref.shape[2] // LANES

    @pl.when(pl.program_id(2) == 0)
    def _():
        s_scr[...] = jnp.zeros_like(s_scr)

    lane = lax.broadcasted_iota(jnp.int32, (1, LANES), 1)
    first = lane < RW_HEAD
    m0 = first.astype(F32)
    m1 = 1.0 - m0
    row = lax.broadcasted_iota(jnp.int32, (LANES, LANES), 0)
    col = lax.broadcasted_iota(jnp.int32, (LANES, LANES), 1)
    strict = row > col
    incl = row >= col
    blk16 = (row // 16) == (col // 16)
    eye = jnp.where(row == col, 1.0, 0.0)
    r64 = lax.broadcasted_iota(jnp.int32, (C, C), 0)
    c64 = lax.broadcasted_iota(jnp.int32, (C, C), 1)
    tri64 = jnp.where(r64 >= c64, 1.0, 0.0).astype(BF16)

    def per_head(x):
        cols = []
        for p in range(n_pairs):
            xp = x[:, p * LANES:(p + 1) * LANES]
            s0 = jnp.sum(xp * m0, axis=-1, keepdims=True)
            s1 = jnp.sum(xp * m1, axis=-1, keepdims=True)
            cols.append(jnp.where(first, s0, s1))
        return jnp.concatenate(cols, axis=1)

    head_ones = jnp.where((row // RW_HEAD) == (col // RW_HEAD), 1.0, 0.0).astype(BF16)

    def per_head_mxu(x):
        return jnp.concatenate([_dot(x[:, p * LANES:(p + 1) * LANES], head_ones)
                                for p in range(n_pairs)], axis=1)

    lora = lora_ref[0]
    wa = lora[:, 0:128]
    w_lin = w0_ref[...] + _dot_hi(jnp.tanh(wa), wup_ref[...])
    logw = -math.exp(-0.5) / (1.0 + jnp.exp(-w_lin))
    a = jax.nn.sigmoid(a0_ref[...] + _dot_hi(wa, aup_ref[...]))
    k = k_ref[0]
    kk = k * kk_ref[...]
    kkn = kk * lax.rsqrt(jnp.maximum(per_head(kk * kk), 1e-24))
    kkn_scr[...] = kkn
    k2_scr[...] = k * (1.0 + (a - 1.0) * ka_ref[...])
    beta_scr[...] = kkn * a
    logw_scr[...] = logw
    for c in range(n_chunks):
        cum_scr[c * C:(c + 1) * C, :] = _dot_ex2(tri64, logw[c * C:(c + 1) * C, :])

    def stack(x):
        return jnp.concatenate([x * m0, x * m1], axis=0)

    lns = [slice(p * LANES, (p + 1) * LANES) for p in range(n_pairs)]

    def indep(first_chunk):
        chains = [(first_chunk + u, p) for u in range(RWKV_A_CHUNKS) for p in range(n_pairs)]
        sls = [slice(c * C, (c + 1) * C) for c, _ in chains]
        lnc = [lns[p] for _, p in chains]
        cum = [cum_scr[sl, ln] for sl, ln in zip(sls, lnc)]
        ig = [jnp.exp(-c) for c in cum]
        d_end = [jnp.exp(c[C - 1:C, :] - c) for c in cum]
        k2 = [k2_scr[sl, ln] for sl, ln in zip(sls, lnc)]
        beta = [beta_scr[sl, ln] for sl, ln in zip(sls, lnc)]
        v_s = [stack(v_ref[0, sl, ln]).astype(BF16) for sl, ln in zip(sls, lnc)]
        yield
        rt_s = [stack(r_ref[0, sl, ln] * jnp.exp(c)) for sl, ln, c in zip(sls, lnc, cum)]
        kap_s = [stack(kkn_scr[sl, ln] * jnp.exp(c - logw_scr[sl, ln]))
                 for sl, ln, c in zip(sls, lnc, cum)]
        bt_s = [stack(b * g) for b, g in zip(beta, ig)]
        kt_s = [stack(k * g) for k, g in zip(k2, ig)]
        lhs = [jnp.concatenate([a, b], axis=0).astype(BF16) for a, b in zip(kap_s, rt_s)]
        yield
        for (c, p), k, b, d, l, vs in zip(chains, k2, beta, d_end, lhs, v_s):
            upd_scr[c, p] = jnp.concatenate([stack(k * d), -stack(b * d)], axis=0).astype(BF16)
            lhs_scr[c, p] = l
            vs_scr[c, p] = vs
        yield
        gram = [_dot_nt(l, jnp.concatenate([b, k], axis=0)) for l, b, k in zip(lhs, bt_s, kt_s)]
        yield
        gb = [g[:, :2 * C] for g in gram]
        gk = [g[:, 2 * C:] for g in gram]
        lb = [jnp.where(strict, g[:2 * C], 0.0) for g in gb]
        lk = [jnp.where(strict, g[:2 * C], 0.0) for g in gk]
        ark = [jnp.where(incl, g[2 * C:], 0.0) for g in gk]
        for (c, p), g in zip(chains, gb):
            arb_scr[c, p] = jnp.where(incl, g[2 * C:], 0.0).astype(BF16)
        yield

        ld = [jnp.where(blk16, x, 0.0) for x in lb]
        off = [x - d for x, d in zip(lb, ld)]
        m = [-d for d in ld]
        m2 = [_dot(a, a) for a in m]
        yield
        m34 = [_dot(jnp.concatenate([a, b], axis=0), b) for a, b in zip(m, m2)]
        m3 = [x[:2 * C] for x in m34]
        m4 = [x[2 * C:] for x in m34]
        yield
        lav = [_dot(jnp.concatenate([a, b], axis=0), v) for a, b, v in zip(lk, ark, v_s)]
        lkv = [x[:2 * C] for x in lav]
        arkv = [x[2 * C:] for x in lav]
        yield
        m8 = [_dot(a, a) for a in m4]
        yield
        m12 = [_dot(a, b) for a, b in zip(m4, m8)]
        yield
        dinv = [_dot(eye + a + b + c, eye + d + e + f)
                for a, b, c, d, e, f in zip(m, m2, m3, m4, m8, m12)]
        yield
        n1 = [_dot(d, o) for d, o in zip(dinv, off)]
        yield
        n2 = [_dot(a, a) for a in n1]
        yield
        x1 = [d + _dot(a, d) for d, a in zip(dinv, n2)]
        yield
        for (c, p), x, a, lv, av in zip(chains, x1, n1, lkv, arkv):
            tinv_scr[c, p] = (x - _dot(a, x)).astype(BF16)
            lkv_scr[c, p] = lv
            arkv_scr[c, p] = av
        yield

    def recur(chunks):
        P = range(n_pairs)
        for ci in chunks:
            sl = slice(ci * C, (ci + 1) * C)
            s = [s_scr[p] for p in P]
            ks = [_dot_nt(lhs_scr[ci, p], s[p]) for p in P]
            yield
            y = [_dot(tinv_scr[ci, p], ks[p][:2 * C] + lkv_scr[ci, p]) for p in P]
            yield
            sn = [_dot_tn(jnp.concatenate([vs_scr[ci, p], y[p].astype(BF16)], axis=0), upd_scr[ci, p])
                  for p in P]
            yield
            for p in P:
                s_scr[p] = s[p] * jnp.exp(cum_scr[sl, lns[p]][C - 1:C, :]) + sn[p]
            arby = [_dot(arb_scr[ci, p], y[p]) for p in P]
            yield
            for p in P:
                o_s = ks[p][2 * C:] + arkv_scr[ci, p] - arby[p]
                o_scr[sl, lns[p]] = o_s[:C] + o_s[C:]
            yield

    def drain(gen):
        for _ in gen:
            pass

    groups = [range(g, g + RWKV_A_CHUNKS) for g in range(0, n_chunks, RWKV_A_CHUNKS)]
    drain(indep(groups[0][0]))
    for prev, cur in zip(groups[:-1], groups[1:]):
        b_gen = recur(prev)
        for _ in indep(cur[0]):
            next(b_gen, None)
        drain(b_gen)
    drain(recur(groups[-1]))

    o = o_scr[...]
    mean = per_head_mxu(o) * (1.0 / RW_HEAD)
    dlt = o - mean
    var = per_head_mxu(dlt * dlt) * (1.0 / RW_HEAD)
    on = dlt * lax.rsqrt(var + GN_EPS) * lnw_ref[...] + lnb_ref[...]
    bonus = per_head(r_ref[0] * k2_scr[...] * rk_ref[...]) * v_ref[0]
    g = _dot_hi(jax.nn.sigmoid(lora[:, 128:384]), gup_ref[...])
    o_ref[0] = ((on + bonus) * g).astype(o_ref.dtype)


def _rwkv(proj, w0, a0, k_k, k_a, r_k, ln_w, ln_b, w_up, a_up, g_up, tc):
    bsz, t, _ = proj.shape
    wl = RWKV_PAIRS * LANES
    row = lambda p: p.reshape(1, RW)
    wup_p = jnp.concatenate([w_up, jnp.zeros_like(a_up)], axis=0)
    aup_p = jnp.concatenate([jnp.zeros_like(w_up), a_up], axis=0)
    gup_p = jnp.concatenate([g_up, jnp.zeros((256 - g_up.shape[0], RW), F32)], axis=0)
    pspec = lambda off: pl.BlockSpec((1, tc, wl), lambda b, h, i: (b, i, off // wl + h))
    vec = pl.BlockSpec((1, wl), lambda b, h, i: (0, h))
    tile = pltpu.VMEM((tc, wl), F32)
    per_chain = lambda rows, dt: pltpu.VMEM((tc // RWKV_CHUNK, RWKV_PAIRS, rows, LANES), dt)
    return pl.pallas_call(
        _rwkv_kernel,
        grid=(bsz, RW // wl, t // tc),
        in_specs=[pspec(P_R), pspec(P_K), pspec(P_V),
                  pl.BlockSpec((1, tc, LORA_W), lambda b, h, i: (b, i, P_LORA // LORA_W)),
                  vec, vec, vec, vec, vec, vec, vec,
                  pl.BlockSpec((128, wl), lambda b, h, i: (0, h)),
                  pl.BlockSpec((128, wl), lambda b, h, i: (0, h)),
                  pl.BlockSpec((256, wl), lambda b, h, i: (0, h))],
        out_specs=pl.BlockSpec((1, tc, wl), lambda b, h, i: (b, i, h)),
        out_shape=jax.ShapeDtypeStruct((bsz, t, RW), BF16),
        scratch_shapes=[pltpu.VMEM((RWKV_PAIRS, LANES, LANES), F32),
                        tile, tile, tile, tile, tile, tile,
                        per_chain(LANES, BF16), per_chain(LANES, BF16), per_chain(LANES, BF16),
                        per_chain(2 * LANES, BF16), per_chain(2 * LANES, BF16),
                        per_chain(LANES, F32), per_chain(LANES, F32)],
        compiler_params=_cparams(3),
        name="rwkv",
    )(proj, proj, proj, proj, row(w0), row(a0), row(k_k), row(k_a), row(r_k), row(ln_w),
      row(ln_b), wup_p, aup_p, gup_p)


def _ssd_kernel(z_ref, x_ref, bc_ref, lora_ref, cwx_ref, cbx_ref, cwb_ref, cbb_ref, dtb_ref,
                alog_ref, dsk_ref, nw_ref, ex_ref, o_ref, extx, extb, st_scr):
    L = SSD_CHUNK

    @pl.when(pl.program_id(1) == 0)
    def _():
        extx[0:8, :] = jnp.zeros((8, extx.shape[1]), F32)
        extb[0:8, :] = jnp.zeros((8, extb.shape[1]), F32)
        st_scr[...] = jnp.zeros_like(st_scr)

    def conv(ext, cur, w_ref, b_ref):
        ext[8:8 + L, :] = cur
        last = SSM_CONV - 1
        acc = b_ref[...] + w_ref[last:last + 1, :] * cur
        for j in range(last):
            acc = acc + w_ref[j:j + 1, :] * ext[5 + j:5 + j + L, :]
        ext[0:8, :] = ext[L:L + 8, :]
        return _silu(acc)

    xs = conv(extx, x_ref[0], cwx_ref, cbx_ref)
    bc = conv(extb, bc_ref[0], cwb_ref, cbb_ref)

    row = lax.broadcasted_iota(jnp.int32, (L, L), 0)
    col = lax.broadcasted_iota(jnp.int32, (L, L), 1)
    incl = row >= col
    tri = jnp.where(incl, 1.0, 0.0).astype(BF16)
    lane = lax.broadcasted_iota(jnp.int32, (1, LANES), 1)
    m_par = [(lane < SSM_HEAD).astype(F32), (lane >= SSM_HEAD).astype(F32)]

    ex = ex_ref[...]
    dts = jax.nn.softplus(lora_ref[0][:, LORA_DT:LORA_DT + LANES] + dtb_ref[...])
    a_slab = dts * (-jnp.exp(alog_ref[...]))
    acs_slab = _dot_ex(tri, a_slab)
    acs_t = acs_slab.T
    dtx = _dot_xe(dts, ex)
    acs_x = _dot_xe(acs_slab, ex)
    last = acs_x[L - 1:L, :]
    xdt = xs * dtx
    xdte = xdt * jnp.exp(last - acs_x)
    eacs = jnp.exp(acs_x)
    elast = jnp.exp(last)

    ys = []
    for j in range(SSM_HEADS // 2):
        g = j // (SSM_HEADS // 4)
        bm = bc[:, g * LANES:(g + 1) * LANES]
        cm = bc[:, (2 + g) * LANES:(3 + g) * LANES]
        pr = slice(j * LANES, (j + 1) * LANES)
        cb = _dot_nt(cm, bm)
        st = st_scr[j]
        y = eacs[:, pr] * _dot(cm, st)
        xp = xdt[:, pr]
        for par in range(2):
            h = 2 * j + par
            diff = acs_slab[:, h:h + 1] - acs_t[h:h + 1, :]
            mh = jnp.exp(jnp.where(incl, diff, -jnp.inf)) * cb
            y = y + _dot(mh, xp * m_par[par])
        st_scr[j] = st * elast[:, pr] + _dot_tn(bm, xdte[:, pr])
        ys.append(y)
    y = jnp.concatenate(ys, axis=1) + xs * dsk_ref[...]
    y = y * _silu(z_ref[0])
    half = SSM_W // 2
    outs = []
    for g in range(2):
        seg = y[:, g * half:(g + 1) * half]
        outs.append(seg * lax.rsqrt(jnp.mean(seg * seg, axis=-1, keepdims=True) + SSM_EPS))
    o_ref[0] = (jnp.concatenate(outs, axis=1) * nw_ref[...]).astype(o_ref.dtype)


def _ssd(proj, conv_w, conv_b, dt_bias, a_log, d_skip, norm_w):
    bsz, t, _ = proj.shape
    L = SSD_CHUNK
    pad_slab = lambda p: jnp.concatenate([p, jnp.zeros((LANES - SSM_HEADS,), F32)]).reshape(1, LANES)
    hh = jnp.arange(LANES)[:, None]
    ll = jnp.arange(SSM_W)[None, :]
    ex = (hh == ll // SSM_HEAD).astype(BF16)
    cwx, cwb = conv_w[:, :SSM_W], conv_w[:, SSM_W:]
    cbx, cbb = conv_b[:SSM_W].reshape(1, -1), conv_b[SSM_W:].reshape(1, -1)
    full = lambda a: pl.BlockSpec(a.shape, lambda b, i: (0,) * a.ndim)
    args = [cwx, cbx, cwb, cbb, pad_slab(dt_bias), pad_slab(a_log),
            jnp.repeat(d_skip, SSM_HEAD).reshape(1, SSM_W), norm_w.reshape(1, SSM_W), ex]
    return pl.pallas_call(
        _ssd_kernel,
        grid=(bsz, t // L),
        in_specs=[pl.BlockSpec((1, L, SSM_W), lambda b, i: (b, i, P_Z // SSM_W)),
                  pl.BlockSpec((1, L, SSM_W), lambda b, i: (b, i, P_XS // SSM_W)),
                  pl.BlockSpec((1, L, 512), lambda b, i: (b, i, P_BC // 512)),
                  pl.BlockSpec((1, L, LORA_W), lambda b, i: (b, i, P_LORA // LORA_W))]
                 + [full(a) for a in args],
        out_specs=pl.BlockSpec((1, L, SSM_W), lambda b, i: (b, i, 0)),
        out_shape=jax.ShapeDtypeStruct((bsz, t, SSM_W), BF16),
        scratch_shapes=[pltpu.VMEM((L + 8, SSM_W), F32), pltpu.VMEM((L + 8, 512), F32),
                        pltpu.VMEM((SSM_HEADS // 2, SSM_STATE, LANES), F32)],
        compiler_params=_cparams(2),
        name="ssd",
    )(proj, proj, proj, proj, *args)


def _outproj_kernel(orw_ref, oss_ref, x_ref, g1_ref, nw_ref, sh_ref, sc_ref, wo1_ref, wo2_ref,
                    wq_ref, h1_ref, n2_ref, q_ref):
    mix = _dot(orw_ref[0], wo1_ref[...]) + _dot(oss_ref[0], wo2_ref[...])
    h1 = x_ref[0] + g1_ref[0] * mix
    h1_ref[0] = h1
    n2 = _rms_mod(h1, nw_ref[...], sh_ref[0], sc_ref[0]).astype(BF16)
    n2_ref[0] = n2
    q_ref[0] = _dot(n2, wq_ref[...]).astype(q_ref.dtype)


def _out_proj(o_rw, o_ss, x, gate1, norm_w, shift, scale, w_out, w_query_p, tm):
    bsz, t, d = x.shape
    wo1 = w_out[:RW].astype(BF16)
    wo2 = w_out[RW:].astype(BF16)
    tok = lambda w: pl.BlockSpec((1, tm, w), lambda b, i: (b, i, 0))
    per_b = pl.BlockSpec((1, 1, d), lambda b, i: (b, 0, 0))
    const = lambda a: pl.BlockSpec(a.shape, lambda b, i: (0,) * a.ndim, pipeline_mode=pl.Buffered(1))
    nw = norm_w.reshape(1, d)
    return pl.pallas_call(
        _outproj_kernel,
        grid=(bsz, t // tm),
        in_specs=[tok(RW), tok(SSM_W), tok(d), per_b, const(nw), per_b, per_b,
                  const(wo1), const(wo2), const(w_query_p)],
        out_specs=[tok(d), tok(d), tok(d)],
        out_shape=[jax.ShapeDtypeStruct((bsz, t, d), F32), jax.ShapeDtypeStruct((bsz, t, d), BF16),
                   jax.ShapeDtypeStruct((bsz, t, d), BF16)],
        compiler_params=_cparams(2),
        name="out_proj",
    )(o_rw, o_ss, x, gate1.reshape(bsz, 1, d), nw, shift.reshape(bsz, 1, d),
      scale.reshape(bsz, 1, d), wo1, wo2, w_query_p)


_CANDS = [(i, j) for i in range(PEER_TOPK) for j in range(PEER_TOPK) if (i + 1) * (j + 1) <= PEER_TOPK]


def _oddeven_merge(lo, hi, r):
    step = r * 2
    if step < hi - lo:
        yield from _oddeven_merge(lo, hi, step)
        yield from _oddeven_merge(lo + r, hi, step)
        yield from [(i, i + r) for i in range(lo + r, hi - r, step)]
    else:
        yield (lo, lo + r)


def _oddeven_sort(lo, hi):
    if hi - lo >= 1:
        mid = lo + (hi - lo) // 2
        yield from _oddeven_sort(lo, mid)
        yield from _oddeven_sort(mid + 1, hi)
        yield from _oddeven_merge(lo, hi, 1)


_SORT16 = list(_oddeven_sort(0, PEER_TOPK - 1))
_SORT_PAD = range(1 << (len(_CANDS) - 1).bit_length())
_SORT_CANDS = list(_oddeven_sort(0, len(_SORT_PAD) - 1))
_BITONIC16 = [(i, i + s) for s in (8, 4, 2, 1) for i in range(PEER_TOPK) if not i & s]


def _route_kernel(q_ref, k1_ref, k2_ref, cnt_ref, e1_ref, rk_ref, e2_ref):
    nh, hd, K = PEER_HEADS, PEER_HALF, PEER_TOPK
    q = q_ref[...]

    sub = 8
    assert PEER_KEYS == sub * K

    def top_sorted(s):
        v = [s[sub * i:sub * (i + 1), :] for i in range(K)]

        def exchange(i, j):
            hi, lo = jnp.maximum(v[i], v[j]), jnp.minimum(v[i], v[j])
            v[i], v[j] = hi, lo

        for i, j in _SORT16:
            exchange(i, j)
        for shift in (4, 2, 1):
            w = [pltpu.roll(x, shift, 0) for x in v]
            v = [jnp.maximum(v[i], w[K - 1 - i]) for i in range(K)]
            for i, j in _BITONIC16:
                exchange(i, j)
        return v

    def rank_of(x, b):
        t1 = x < b[7]
        t2 = x < jnp.where(t1, b[11], b[3])
        t3 = x < jnp.where(t1, jnp.where(t2, b[13], b[9]), jnp.where(t2, b[5], b[1]))
        lo = jnp.where(t2, jnp.where(t3, b[6], b[4]), jnp.where(t3, b[2], b[0]))
        hi = jnp.where(t2, jnp.where(t3, b[14], b[12]), jnp.where(t3, b[10], b[8]))
        t4 = x < jnp.where(t1, hi, lo)
        t5 = x < b[15]
        one = lambda t, val: jnp.where(t, val, 0.0)
        return one(t1, 8.0) + one(t2, 4.0) + one(t3, 2.0) + one(t4, 1.0) + one(t5, 1.0)

    s1 = [_dot_nt(k1_ref[h], q[:, h * hd:(h + 1) * hd]) for h in range(nh)]
    s2 = [_dot_nt(k2_ref[h], q[:, (nh + h) * hd:(nh + h + 1) * hd]) for h in range(nh)]
    a_full = [top_sorted(s) for s in s1]
    b_full = [top_sorted(s) for s in s2]
    a_h = [[x[0:1, :] for x in v] for v in a_full]
    b_h = [[x[0:1, :] for x in v] for v in b_full]
    rank2 = [jnp.concatenate([rank_of(s2[h][sub * i:sub * (i + 1), :], b_full[h]) for i in range(K)],
                             axis=0) for h in range(nh)]
    a = [jnp.concatenate([a_h[h][i] for h in range(nh)], axis=0) for i in range(K)]
    b = [jnp.concatenate([b_h[h][i] for h in range(nh)], axis=0) for i in range(K)]

    sums = {c: a[c[0]] + b[c[1]] for c in _CANDS}
    wires = [sums[c] for c in _CANDS] + [None] * (len(_SORT_PAD) - len(_CANDS))
    for i, j in _SORT_CANDS:
        hi, lo = wires[i], wires[j]
        if lo is None:
            continue
        if hi is None:
            wires[i], wires[j] = lo, None
        else:
            wires[i], wires[j] = jnp.maximum(hi, lo), jnp.minimum(hi, lo)
    tau = wires[K - 1]
    ea = [jnp.exp(ai - a[0]) for ai in a]
    eb = [jnp.exp(bj - b[0]) for bj in b]
    zsum = jnp.zeros_like(tau)
    cnt = [jnp.zeros_like(tau) for _ in range(K)]
    for (i, j) in _CANDS:
        sel = sums[(i, j)] >= tau
        zsum = zsum + jnp.where(sel, ea[i] * eb[j], 0.0)
        cnt[i] = cnt[i] + jnp.where(sel, 1.0, 0.0)
    zinv = 1.0 / zsum
    for h in range(nh):
        row = slice(h, h + 1)
        cfull = jnp.zeros_like(s1[h])
        for i in range(K):
            cfull = jnp.where(s1[h] == a_h[h][i], cnt[i][row, :], cfull)
        cnt_ref[h] = cfull
        e1_ref[h] = jnp.exp(s1[h] - a_h[h][0])
        rk_ref[h] = rank2[h].astype(BF16)
        e2_ref[h] = (jnp.exp(s2[h] - b_h[h][0]) * zinv[row, :]).astype(BF16)


def _peer_route(q, sub_keys, tm):
    ntok = q.shape[0]
    nk, nh = PEER_KEYS, PEER_HEADS
    k1 = sub_keys[:, 0].astype(BF16)
    k2 = sub_keys[:, 1].astype(BF16)
    blk = pl.BlockSpec((nh, nk, tm), lambda i: (0, 0, i))
    f32_out = jax.ShapeDtypeStruct((nh, nk, ntok), F32)
    bf16_out = jax.ShapeDtypeStruct((nh, nk, ntok), BF16)
    return pl.pallas_call(
        _route_kernel,
        grid=(ntok // tm,),
        in_specs=[pl.BlockSpec((tm, q.shape[1]), lambda i: (i, 0)),
                  pl.BlockSpec(k1.shape, lambda i: (0, 0, 0)),
                  pl.BlockSpec(k2.shape, lambda i: (0, 0, 0))],
        out_specs=[blk, blk, blk, blk],
        out_shape=[f32_out, f32_out, bf16_out, bf16_out],
        compiler_params=_cparams(1),
        name="peer_route",
    )(q, k1, k2)


def _dense_kernel(n2_ref, u_ref, vt_ref, cnt_ref, e1_ref, rk_ref, e2_ref, o_ref, w_scr):
    j = pl.program_id(1)
    nk = PEER_KEYS
    ec = u_ref.shape[0]
    tm = n2_ref.shape[0]
    n_piece = 4
    pc = ec // n_piece
    zero = jnp.zeros((), BF16)
    n2 = n2_ref[...]

    def weights(p, act):
        for s in range(pc // nk):
            sl = p * (pc // nk) + s
            gsum = jnp.zeros((nk, tm), BF16)
            for h in range(PEER_HEADS):
                cb = jnp.broadcast_to(cnt_ref[h, sl:sl + 1, :], (nk, tm)).astype(BF16)
                eb = jnp.broadcast_to(e1_ref[h, sl:sl + 1, :], (nk, tm)).astype(BF16)
                gsum = gsum + jnp.where(rk_ref[h] < cb, e2_ref[h], zero) * eb
            a = act[s * nk:(s + 1) * nk, :]
            gelu = 0.5 * a * (1.0 + lax.erf(a * (2.0 ** -0.5)))
            w_scr[sl * nk:(sl + 1) * nk, :] = gsum * gelu.astype(BF16)

    acts = [_dot_nt(u_ref[p * pc:(p + 1) * pc, :], n2) for p in range(n_piece)]
    n_row = 4
    rh = vt_ref.shape[0] // n_row
    for p in range(n_piece):
        weights(p, acts[p])
        wp = w_scr[p * pc:(p + 1) * pc, :]
        for r in range(n_row):
            rows = slice(r * rh, (r + 1) * rh)
            part = _dot(vt_ref[rows, p * pc:(p + 1) * pc], wp)
            prev = jnp.where(j == 0, 0.0, o_ref[rows, :]) if p == 0 else o_ref[rows, :]
            o_ref[rows, :] = prev + part


def _peer_dense(n2, u_bf, vt_bf, cnt, e1, rk, e2, tm, ec):
    ntok, d = n2.shape
    n_exp = u_bf.shape[0]
    ns = ec // PEER_KEYS
    rows = pl.BlockSpec((PEER_HEADS, ns, tm), lambda i, j: (0, j, i))
    slab = pl.BlockSpec((PEER_HEADS, PEER_KEYS, tm), lambda i, j: (0, 0, i))
    return pl.pallas_call(
        _dense_kernel,
        grid=(ntok // tm, n_exp // ec),
        in_specs=[pl.BlockSpec((tm, d), lambda i, j: (i, 0)),
                  pl.BlockSpec((ec, d), lambda i, j: (j, 0)),
                  pl.BlockSpec((d, ec), lambda i, j: (0, j)),
                  rows, rows, slab, slab],
        out_specs=pl.BlockSpec((d, tm), lambda i, j: (0, i)),
        out_shape=jax.ShapeDtypeStruct((d, ntok), F32),
        scratch_shapes=[pltpu.VMEM((ec, tm), BF16)],
        compiler_params=_cparams(2),
        name="peer_dense",
    )(n2, u_bf, vt_bf, cnt, e1, rk, e2)


def _final_kernel(h1_ref, pt_ref, g2_ref, nw_ref, o_ref):
    h = h1_ref[0] + g2_ref[0] * pt_ref[...].T
    ms = jnp.mean(h * h, axis=-1, keepdims=True)
    o_ref[0] = (h * lax.rsqrt(ms + RMS_EPS) * nw_ref[...]).astype(o_ref.dtype)


def _final(h1, peer_t, gate2, norm_w, out_dtype, tm):
    bsz, t, d = h1.shape
    nt = t // tm
    return pl.pallas_call(
        _final_kernel,
        grid=(bsz, nt),
        in_specs=[pl.BlockSpec((1, tm, d), lambda b, i: (b, i, 0)),
                  pl.BlockSpec((d, tm), lambda b, i: (0, b * nt + i)),
                  pl.BlockSpec((1, 1, d), lambda b, i: (b, 0, 0)),
                  pl.BlockSpec((1, d), lambda b, i: (0, 0))],
        out_specs=pl.BlockSpec((1, tm, d), lambda b, i: (b, i, 0)),
        out_shape=jax.ShapeDtypeStruct((bsz, t, d), out_dtype),
        compiler_params=_cparams(2),
        name="final_norm",
    )(h1, peer_t, gate2.reshape(bsz, 1, d), norm_w.reshape(1, d))


def _pack_in_proj(w_in, mu):
    d = w_in.shape[0]
    rp = 3 * RW + 64 + 64 + 160
    zc = lambda n: jnp.zeros((d, n), w_in.dtype)
    z0 = rp
    dt0 = rp + SSM_W + SSM_W + 512
    w = jnp.concatenate([
        w_in[:, :3 * RW],
        w_in[:, z0:z0 + SSM_W],
        w_in[:, z0 + SSM_W:z0 + 2 * SSM_W],
        w_in[:, z0 + 2 * SSM_W:z0 + 2 * SSM_W + 512],
        w_in[:, 3 * RW:rp], zc(LORA_DT - 288),
        w_in[:, dt0:dt0 + SSM_HEADS], zc(LORA_W - LORA_DT - SSM_HEADS)], axis=1)
    m = jnp.concatenate([mu[:3 * RW], jnp.zeros((P_LORA - 3 * RW,), mu.dtype), mu[3 * RW:rp],
                         jnp.zeros((LORA_W - 288,), mu.dtype)])
    return w.astype(BF16), m.reshape(1, P_TOT)


def _tile(t, pref):
    return pref if t % pref == 0 else t


def kernel(x, c, ada_w, ada_b, norm1_w, w_in, rwkv_mu, rwkv_w0, rwkv_w_up, rwkv_a0, rwkv_a_up, rwkv_g_up, rwkv_k_k, rwkv_k_a, rwkv_r_k, rwkv_ln_w, rwkv_ln_b, mamba_conv_w, mamba_conv_b, mamba_dt_bias, mamba_a_log, mamba_d, mamba_norm_w, w_out, norm2_w, peer_w_query, peer_sub_keys, peer_u, peer_v, final_norm_w):
    out_dtype = x.dtype
    bsz, t, d = x.shape
    ntok = bsz * t
    assert ada_w.shape[0] == 1, "single-layer trunk"
    h = x.astype(F32)
    mod = _ada_mod(c.astype(F32), ada_w[0], ada_b[0])
    shift1, scale1, gate1, shift2, scale2, gate2 = jnp.split(mod, 6, axis=-1)
    w_packed, mu_packed = _pack_in_proj(w_in[0], rwkv_mu[0])
    proj = _in_proj(h, norm1_w[0], shift1, scale1, w_packed, mu_packed, _tile(t, TM_IN_PROJ), TN_IN_PROJ)
    o_rw = _rwkv(proj, rwkv_w0[0], rwkv_a0[0], rwkv_k_k[0], rwkv_k_a[0], rwkv_r_k[0].reshape(-1),
                 rwkv_ln_w[0], rwkv_ln_b[0], rwkv_w_up[0], rwkv_a_up[0], rwkv_g_up[0],
                 _tile(t, TM_RWKV))
    o_ss = _ssd(proj, mamba_conv_w[0], mamba_conv_b[0], mamba_dt_bias[0], mamba_a_log[0],
                mamba_d[0], mamba_norm_w[0])
    wq = peer_w_query[0].reshape(d, PEER_HEADS, 2, PEER_HALF).transpose(0, 2, 1, 3)
    wq = wq.reshape(d, 2 * PEER_HEADS * PEER_HALF).astype(BF16)
    h1, n2, q = _out_proj(o_rw, o_ss, h, gate1, norm2_w[0], shift2, scale2, w_out[0], wq,
                          _tile(t, TM_OUT_PROJ))
    cnt, e1, rk, e2 = _peer_route(q.reshape(ntok, -1), peer_sub_keys[0], _tile(ntok, TM_ROUTE))
    peer_t = _peer_dense(n2.reshape(ntok, d), peer_u[0].astype(BF16), peer_v[0].T.astype(BF16),
                         cnt, e1, rk, e2, _tile(ntok, TM_DENSE), EC_DENSE)
    return _final(h1, peer_t, gate2, final_norm_w, out_dtype, _tile(t, TM_FINAL))
```
